```python
import math
import jax
import jax.numpy as jnp
from jax import lax
import numpy as np

D_MODEL = 4096
BATCH = 2
SEQ = 4096
DEPTH = 4

GRID_W = 64
CTX_LEN = 256
N_MOD = 6
ADA_RANK = 256
EPS = 1e-6
BRANCH_W = 1024
N_BRANCH = 3
CONV_W = BRANCH_W
CONV_K = 3
ATT_HEADS = 8
ATT_HD = 64
ATT_VD = 2 * ATT_HD
ATT_W = ATT_HEADS * ATT_VD
ATT_SCALE = ATT_HD ** -0.5
Q_BLOCK = 128
ROPE_THETA = 10000.0
ROPE_AXIS_DIM = ATT_HD // 2
ROPE_PAIRS = ROPE_AXIS_DIM // 2
LAMBDA_INIT_BASE = 0.8
LAMBDA_INIT_AMP = 0.6
LAMBDA_INIT_RATE = 0.3
LRU_W = BRANCH_W
LRU_BLOCKS = 8
LRU_BD = LRU_W // LRU_BLOCKS
LRU_CONV_K = 4
LRU_C = 8.0
N_EXPERTS = 32
TOP_K = 4
EXPERT_FF = 256
SWIGLU_ALPHA = 1.702
SWIGLU_LIMIT = 7.0

A_OFF = 0
A_END = A_OFF + 3 * CONV_W
Q_OFF = A_END
K_OFF = Q_OFF + ATT_W
V_OFF = K_OFF + ATT_W
V_END = V_OFF + ATT_W
LX_OFF = V_END
LY_OFF = LX_OFF + LRU_W
LY_END = LY_OFF + LRU_W
G_OFF = LY_END
IN_COLS = G_OFF + N_BRANCH * D_MODEL
MIX_W = CONV_W + ATT_W + LRU_W

kernel_name = "hybrid_diffusion_gated_parallel_moe"


def rmsnorm(u, g):
    uf = u.astype(jnp.float32)
    y = uf * lax.rsqrt(jnp.mean(uf * uf, axis=-1, keepdims=True) + EPS)
    return (y * g.astype(jnp.float32)).astype(u.dtype)


def ada_modulation(cond, a, b, bias):
    m = (jax.nn.silu(cond) @ a) @ b + bias
    return jnp.split(m, N_MOD, axis=-1)


def modulate(u, shift, scale):
    return u * (1 + scale[:, None, :]) + shift[:, None, :]


def dwconv(u, w, pad_l, pad_r):
    return lax.conv_general_dilated(u, w[:, None, :].astype(u.dtype), window_strides=(1,),
                                    padding=[(pad_l, pad_r)],
                                    dimension_numbers=('NWC', 'WIO', 'NWC'),
                                    feature_group_count=u.shape[-1])


def short_conv_mixer(p, w):
    b_gate, c_gate, h = jnp.split(p, 3, axis=-1)
    return b_gate * dwconv(c_gate * h, w, 1, 1)


def axial_rope_tables(rows):
    t = jnp.arange(rows * GRID_W)
    pos = jnp.stack([t // GRID_W, t % GRID_W], axis=-1).astype(jnp.float32)
    inv = ROPE_THETA ** (-jnp.arange(ROPE_PAIRS, dtype=jnp.float32) * 2.0 / ROPE_AXIS_DIM)
    ang = pos[:, :, None] * inv
    return jnp.cos(ang), jnp.sin(ang)


def apply_axial_rope(u, cos, sin):
    sh = u.shape
    uf = u.astype(jnp.float32).reshape(*sh[:-1], 2, 2, ROPE_PAIRS)
    u1, u2 = uf[..., 0, :], uf[..., 1, :]
    cb = cos[None, :, None, None]
    sb = sin[None, :, None, None]
    out = jnp.stack([u1 * cb - u2 * sb, u1 * sb + u2 * cb], axis=-2)
    return out.reshape(sh).astype(u.dtype)


def split_qk_heads(u):
    return u.reshape(*u.shape[:-1], ATT_HEADS, 2, ATT_HD)


def split_v_heads(u):
    return u.reshape(*u.shape[:-1], ATT_HEADS, ATT_VD)


def diff_lambda(lam_qk, lam_init):
    lq = lam_qk.astype(jnp.float32)
    return jnp.exp(jnp.sum(lq[0] * lq[1])) - jnp.exp(jnp.sum(lq[2] * lq[3])) + lam_init


def diff_attend(q, k, v, lam):
    s = jnp.einsum('bqhmd,bkhmd->bhmqk', q.astype(jnp.float32), k.astype(jnp.float32)) * ATT_SCALE
    p = jax.nn.softmax(s, axis=-1)
    w = p[:, :, 0] - lam * p[:, :, 1]
    return jnp.einsum('bhqk,bkhe->bqhe', w, v.astype(jnp.float32))


def diff_attention_latent(q, k, v, k_ctx, v_ctx, lam):
    bsz, t = q.shape[0], q.shape[1]
    k_all = jnp.concatenate([k_ctx, k], axis=1)
    v_all = jnp.concatenate([v_ctx, v], axis=1)
    nb = t // Q_BLOCK
    qb = jnp.moveaxis(q.reshape(bsz, nb, Q_BLOCK, *q.shape[2:]), 1, 0)
    ob = lax.map(lambda qi: diff_attend(qi, k_all, v_all, lam), qb)
    return jnp.moveaxis(ob, 0, 1).reshape(bsz, t, *ob.shape[3:])


def diff_head_norm(o, g, lam_init):
    y = o * lax.rsqrt(jnp.mean(o * o, axis=-1, keepdims=True) + EPS) * g.astype(jnp.float32) * (1.0 - lam_init)
    return y.reshape(*o.shape[:2], ATT_W)


def rglru_gates(u, w_a, b_a, w_x, b_x, lam_p):
    ub = u.reshape(*u.shape[:-1], LRU_BLOCKS, LRU_BD)
    r = jax.nn.sigmoid(jnp.einsum('btnd,nde->btne', ub, w_a.astype(jnp.float32)).reshape(u.shape) + b_a.astype(jnp.float32))
    i = jax.nn.sigmoid(jnp.einsum('btnd,nde->btne', ub, w_x.astype(jnp.float32)).reshape(u.shape) + b_x.astype(jnp.float32))
    log_a = -LRU_C * r * jax.nn.softplus(-lam_p.astype(jnp.float32))
    return log_a, i * u


def linear_scan(log_a, b):
    def combine(lhs, rhs):
        return lhs[0] * rhs[0], rhs[0] * lhs[1] + rhs[1]
    return lax.associative_scan(combine, (jnp.exp(log_a), b), axis=1)


def rglru_direction(uc, ul, w_a, b_a, w_x, b_x, lam_p, reverse):
    if reverse:
        uc, ul = jnp.flip(uc, axis=1), jnp.flip(ul, axis=1)
    log_ac, gxc = rglru_gates(uc, w_a, b_a, w_x, b_x, lam_p)
    mult_c = jnp.sqrt(-jnp.expm1(2.0 * log_ac)).at[:, 0].set(1.0)
    _, hc = linear_scan(log_ac, mult_c * gxc)
    log_al, gxl = rglru_gates(ul, w_a, b_a, w_x, b_x, lam_p)
    a_cum, hl = linear_scan(log_al, jnp.sqrt(-jnp.expm1(2.0 * log_al)) * gxl)
    hl = hl + a_cum * hc[:, -1:]
    if reverse:
        hc, hl = jnp.flip(hc, axis=1), jnp.flip(hl, axis=1)
    return hc, hl


def rglru_mixer(lx_ctx, ly_ctx, lx_lat, ly_lat, conv_w, conv_b, w_a, b_a, w_x, b_x, lam_p):
    uc = (dwconv(lx_ctx, conv_w, 2, 1) + conv_b).astype(jnp.float32)
    ul = (dwconv(lx_lat, conv_w, 2, 1) + conv_b).astype(jnp.float32)
    hc_f, hl_f = rglru_direction(uc, ul, w_a[0], b_a[0], w_x[0], b_x[0], lam_p[0], False)
    hc_b, hl_b = rglru_direction(uc, ul, w_a[1], b_a[1], w_x[1], b_x[1], lam_p[1], True)
    o_lat = ((hl_f + hl_b) * jax.nn.gelu(ly_lat.astype(jnp.float32))).astype(lx_lat.dtype)
    if ly_ctx is None:
        return None, o_lat
    o_ctx = ((hc_f + hc_b) * jax.nn.gelu(ly_ctx.astype(jnp.float32))).astype(lx_ctx.dtype)
    return o_ctx, o_lat


def merge_branches(o_a, o_b, o_c, gate_pre, w_branch, w_out):
    g = jax.nn.sigmoid(gate_pre.astype(jnp.float32)).astype(o_a.dtype)
    y = (g[..., :D_MODEL] * (o_a @ w_branch[:CONV_W])
         + g[..., D_MODEL:2 * D_MODEL] * (o_b @ w_branch[CONV_W:CONV_W + ATT_W])
         + g[..., 2 * D_MODEL:] * (o_c @ w_branch[CONV_W + ATT_W:]))
    return y @ w_out


def moe_ffn(u, w_r, b_r, w_gu, b_gu, w_dn, b_dn):
    logits = (u @ w_r + b_r).astype(jnp.float32)
    top_v, top_i = lax.top_k(logits, TOP_K)
    top_w = jax.nn.softmax(top_v, axis=-1)
    gates = jnp.einsum('btk,btke->bte', top_w, jax.nn.one_hot(top_i, N_EXPERTS, dtype=jnp.float32))
    h = (jnp.einsum('btd,edf->btef', u, w_gu) + b_gu).astype(jnp.float32)
    h_glu = jnp.minimum(h[..., 0::2], SWIGLU_LIMIT)
    h_lin = jnp.clip(h[..., 1::2], -SWIGLU_LIMIT, SWIGLU_LIMIT)
    act = h_glu * jax.nn.sigmoid(SWIGLU_ALPHA * h_glu) * (h_lin + 1.0) * gates[..., None]
    y = jnp.einsum('btef,efd->btd', act.astype(u.dtype), w_dn) + gates.astype(u.dtype) @ b_dn
    return y.astype(u.dtype)


def setup_inputs(seed: int = 0) -> dict:
    key = jax.random.key(seed)
    ks = jax.random.split(key, 32)
    f32 = jnp.float32
    L, D = DEPTH, D_MODEL

    def nrm(k, shape, s):
        return jax.random.normal(k, shape, f32) * s

    a_c = jax.random.uniform(ks[19], (L, 2, LRU_W), f32, 0.9, 0.999)
    s_l = a_c ** (1.0 / LRU_C)
    return {
        "x": nrm(ks[0], (BATCH, SEQ, D), 1.0),
        "c": nrm(ks[1], (BATCH, D), 1.0),
        "ctx": nrm(ks[2], (BATCH, CTX_LEN, D), 1.0),
        "c_ctx": nrm(ks[3], (D,), 1.0),
        "ada_a": nrm(ks[4], (L, D, ADA_RANK), D ** -0.5),
        "ada_b": nrm(ks[5], (L, ADA_RANK, N_MOD * D), 0.3 * ADA_RANK ** -0.5),
        "ada_bias": nrm(ks[6], (L, N_MOD * D), 0.02),
        "norm1_g": 1.0 + nrm(ks[7], (L, D), 0.02),
        "norm2_g": 1.0 + nrm(ks[8], (L, D), 0.02),
        "w_in": nrm(ks[9], (L, D, IN_COLS), D ** -0.5),
        "conv_a_w": nrm(ks[10], (L, CONV_K, CONV_W), CONV_K ** -0.5),
        "lam_qk": nrm(ks[11], (L, 4, ATT_HD), 0.1),
        "subln_g": 1.0 + nrm(ks[12], (L, ATT_VD), 0.02),
        "lru_conv_w": nrm(ks[13], (L, LRU_CONV_K, LRU_W), LRU_CONV_K ** -0.5),
        "lru_conv_b": nrm(ks[14], (L, LRU_W), 0.02),
        "lru_w_a": nrm(ks[15], (L, 2, LRU_BLOCKS, LRU_BD, LRU_BD), LRU_BD ** -0.5),
        "lru_b_a": nrm(ks[16], (L, 2, LRU_W), 0.02),
        "lru_w_x": nrm(ks[17], (L, 2, LRU_BLOCKS, LRU_BD, LRU_BD), LRU_BD ** -0.5),
        "lru_b_x": nrm(ks[18], (L, 2, LRU_W), 0.02),
        "lru_lambda": jnp.log(s_l) - jnp.log1p(-s_l),
        "w_branch": nrm(ks[20], (L, MIX_W, D), BRANCH_W ** -0.5),
        "w_out": nrm(ks[21], (L, D, D), D ** -0.5),
        "router_w": nrm(ks[22], (L, D, N_EXPERTS), D ** -0.5),
        "router_b": nrm(ks[23], (L, N_EXPERTS), 0.01),
        "w_gu": nrm(ks[24], (L, N_EXPERTS, D, 2 * EXPERT_FF), D ** -0.5),
        "b_gu": nrm(ks[25], (L, N_EXPERTS, 2 * EXPERT_FF), 0.02),
        "w_dn": nrm(ks[26], (L, N_EXPERTS, EXPERT_FF, D), EXPERT_FF ** -0.5),
        "b_dn": nrm(ks[27], (L, N_EXPERTS, D), 0.02),
        "final_g": 1.0 + nrm(ks[28], (D,), 0.02),
    }


def reference(x, c, ctx, c_ctx, ada_a, ada_b, ada_bias, norm1_g, norm2_g, w_in, conv_a_w, lam_qk,
              subln_g, lru_conv_w, lru_conv_b, lru_w_a, lru_b_a, lru_w_x, lru_b_x, lru_lambda,
              w_branch, w_out, router_w, router_b, w_gu, b_gu, w_dn, b_dn, final_g):
    rows = x.shape[1] // GRID_W
    cos, sin = axial_rope_tables(rows)
    h_ctx = ctx
    for l in range(DEPTH):
        last = l == DEPTH - 1
        lam_init = LAMBDA_INIT_BASE - LAMBDA_INIT_AMP * math.exp(-LAMBDA_INIT_RATE * l)
        mx = ada_modulation(c, ada_a[l], ada_b[l], ada_bias[l])
        mc = ada_modulation(c_ctx[None, :], ada_a[l], ada_b[l], ada_bias[l])
        win = w_in[l]

        xn = modulate(rmsnorm(x, norm1_g[l]), mx[0], mx[1])
        cn = modulate(rmsnorm(h_ctx, norm1_g[l]), mc[0], mc[1])
        p_lat = xn @ win
        if last:
            kv_ctx = cn @ win[:, K_OFF:V_END]
            lx_ctx = cn @ win[:, LX_OFF:LY_OFF]
            p_ctx = None
        else:
            p_ctx = cn @ win
            kv_ctx = p_ctx[..., K_OFF:V_END]
            lx_ctx = p_ctx[..., LX_OFF:LY_OFF]
        lam = diff_lambda(lam_qk[l], lam_init)

        oa_lat = short_conv_mixer(p_lat[..., A_OFF:A_END], conv_a_w[l])

        q_lat = apply_axial_rope(split_qk_heads(p_lat[..., Q_OFF:K_OFF]), cos, sin)
        k_lat = apply_axial_rope(split_qk_heads(p_lat[..., K_OFF:V_OFF]), cos, sin)
        v_lat = split_v_heads(p_lat[..., V_OFF:V_END])
        k_ctx = split_qk_heads(kv_ctx[..., :ATT_W])
        v_ctx = split_v_heads(kv_ctx[..., ATT_W:])
        ob_lat = diff_head_norm(diff_attention_latent(q_lat, k_lat, v_lat, k_ctx, v_ctx, lam),
                                subln_g[l], lam_init).astype(x.dtype)

        ly_ctx = None if last else p_ctx[..., LY_OFF:LY_END]
        oc_ctx, oc_lat = rglru_mixer(lx_ctx, ly_ctx, p_lat[..., LX_OFF:LY_OFF], p_lat[..., LY_OFF:LY_END],
                                     lru_conv_w[l], lru_conv_b[l], lru_w_a[l], lru_b_a[l],
                                     lru_w_x[l], lru_b_x[l], lru_lambda[l])

        mix_lat = merge_branches(oa_lat, ob_lat, oc_lat, p_lat[..., G_OFF:], w_branch[l], w_out[l])
        x = x + mx[2][:, None, :] * mix_lat

        if not last:
            oa_ctx = short_conv_mixer(p_ctx[..., A_OFF:A_END], conv_a_w[l])
            q_ctx = split_qk_heads(p_ctx[..., Q_OFF:K_OFF])
            ob_ctx = diff_head_norm(diff_attend(q_ctx, k_ctx, v_ctx, lam), subln_g[l], lam_init).astype(h_ctx.dtype)
            mix_ctx = merge_branches(oa_ctx, ob_ctx, oc_ctx, p_ctx[..., G_OFF:], w_branch[l], w_out[l])
            h_ctx = h_ctx + mc[2][:, None, :] * mix_ctx

        xn2 = modulate(rmsnorm(x, norm2_g[l]), mx[3], mx[4])
        x = x + mx[5][:, None, :] * moe_ffn(xn2, router_w[l], router_b[l], w_gu[l], b_gu[l], w_dn[l], b_dn[l])
        if not last:
            cn2 = modulate(rmsnorm(h_ctx, norm2_g[l]), mc[3], mc[4])
            h_ctx = h_ctx + mc[5][:, None, :] * moe_ffn(cn2, router_w[l], router_b[l], w_gu[l], b_gu[l], w_dn[l], b_dn[l])
    return rmsnorm(x, final_g)
```

```python
import functools
import math
from typing import NamedTuple

import jax
import jax.numpy as jnp
from jax import lax
from jax.experimental import pallas as pl
from jax.experimental.pallas import tpu as pltpu

F32 = jnp.float32
BF16 = jnp.bfloat16

LANES = 128
SUBLANES = 8
EPS = 1e-6
ATT_HD = 64
ATT_VD = 2 * ATT_HD
ROPE_THETA = 10000.0
ROPE_AXIS_DIM = ATT_HD // 2
ROPE_PAIRS = ROPE_AXIS_DIM // 2
LAMBDA_INIT_BASE = 0.8
LAMBDA_INIT_AMP = 0.6
LAMBDA_INIT_RATE = 0.3
LRU_C = 8.0
SWIGLU_ALPHA = 1.702
SWIGLU_LIMIT = 7.0
N_MOD = 6
VMEM_LIMIT = 56 * 1024 * 1024


class _Cfg(NamedTuple):
    d_model: int = 4096
    batch: int = 2
    seq: int = 4096
    depth: int = 4
    grid_w: int = 64
    ctx_len: int = 256
    branch_w: int = 1024
    n_experts: int = 32
    top_k: int = 4
    expert_ff: int = 256

    @property
    def heads(self):
        return self.branch_w // ATT_VD

    @property
    def m_lat(self):
        return self.batch * self.seq

    @property
    def m_all(self):
        return self.batch * (self.seq + self.ctx_len)

    @property
    def in_cols(self):
        return 8 * self.branch_w + 3 * self.d_model


def _params(n_grid_dims):
    return pltpu.CompilerParams(
        dimension_semantics=("arbitrary",) * n_grid_dims, vmem_limit_bytes=VMEM_LIMIT)


def _dot(a, b):
    return jnp.dot(a, b, preferred_element_type=F32)


def _split_bf16(x):
    hi = x.astype(BF16)
    lo = (x - hi.astype(F32)).astype(BF16)
    return hi, lo


def _dot3(x, w):
    xh, xl = _split_bf16(x)
    wh, wl = _split_bf16(w)
    return _dot(xh, wh) + (_dot(xl, wh) + _dot(xh, wl))


def _sigmoid(x):
    return 1.0 / (1.0 + jnp.exp(-x))


def _ada_kernel(cond_ref, a_ref, b_ref, bias_ref, o_ref):
    cond = cond_ref[...]
    t = _dot3(cond * _sigmoid(cond), a_ref[...])
    o_ref[...] = _dot3(t, b_ref[...]) + bias_ref[...]


def _ada(cfg, cond8, a, b, bias):
    d = cfg.d_model
    rank = a.shape[1]
    out = pl.pallas_call(
        _ada_kernel,
        out_shape=jax.ShapeDtypeStruct((SUBLANES, N_MOD * d), F32),
        grid=(N_MOD,),
        in_specs=[
            pl.BlockSpec((SUBLANES, d), lambda j: (0, 0)),
            pl.BlockSpec((d, rank), lambda j: (0, 0)),
            pl.BlockSpec((rank, d), lambda j: (0, j)),
            pl.BlockSpec((1, d), lambda j: (0, j)),
        ],
        out_specs=pl.BlockSpec((SUBLANES, d), lambda j: (0, j)),
        compiler_params=_params(1),
        name="ada_modulation",
    )(cond8, a, b, bias.reshape(1, N_MOD * d))
    return out.reshape(SUBLANES, N_MOD, d)[: cfg.batch + 1]


def _rms(x, g):
    return x * lax.rsqrt(jnp.mean(x * x, axis=-1, keepdims=True) + EPS) * g


def _norm_mod_kernel(x_ref, g_ref, mod_ref, o_ref, *, shift, scale):
    y = _rms(x_ref[...], g_ref[...])
    y = y * (1.0 + mod_ref[0, scale:scale + 1, :]) + mod_ref[0, shift:shift + 1, :]
    o_ref[...] = y.astype(o_ref.dtype)


def _norm_plain_kernel(x_ref, g_ref, o_ref):
    o_ref[...] = _rms(x_ref[...], g_ref[...]).astype(o_ref.dtype)


def _norm_router_kernel(x_ref, g_ref, mod_ref, wr_ref, br_ref, o_ref, gates_ref, *, shift, scale, top_k):
    y = _rms(x_ref[...], g_ref[...])
    y = y * (1.0 + mod_ref[0, scale:scale + 1, :]) + mod_ref[0, shift:shift + 1, :]
    o_ref[...] = y.astype(o_ref.dtype)
    work = _dot3(y, wr_ref[...]) + br_ref[...]
    lane = lax.broadcasted_iota(jnp.int32, work.shape, 1).astype(F32)
    n_lane = float(work.shape[1])
    vals, hots = [], []
    for _ in range(top_k):
        m = jnp.max(work, axis=-1, keepdims=True)
        idx = jnp.min(jnp.where(work == m, lane, n_lane), axis=-1, keepdims=True)
        hot = lane == idx
        vals.append(m)
        hots.append(hot)
        work = jnp.where(hot, -jnp.inf, work)
    exps = [jnp.exp(v - vals[0]) for v in vals]
    denom = exps[0]
    for e in exps[1:]:
        denom = denom + e
    gates = jnp.zeros(work.shape, F32)
    for e, hot in zip(exps, hots):
        gates = jnp.where(hot, e / denom, gates)
    gates_ref[...] = gates


def _seg_of_block(cfg, i, bm):
    return (i * bm) // cfg.seq


def _norm_mod(cfg, x, g, mod, shift, scale, bm=256):
    m, d = x.shape
    return pl.pallas_call(
        functools.partial(_norm_mod_kernel, shift=shift, scale=scale),
        out_shape=jax.ShapeDtypeStruct((m, d), BF16),
        grid=(m // bm,),
        in_specs=[
            pl.BlockSpec((bm, d), lambda i: (i, 0)),
            pl.BlockSpec((1, d), lambda i: (0, 0)),
            pl.BlockSpec((1, N_MOD, d), lambda i: (_seg_of_block(cfg, i, bm), 0, 0)),
        ],
        out_specs=pl.BlockSpec((bm, d), lambda i: (i, 0)),
        compiler_params=_params(1),
        name="norm_mod",
    )(x, g.reshape(1, d), mod)


def _norm_plain(x, g, m_rows, bm=256):
    d = x.shape[1]
    return pl.pallas_call(
        _norm_plain_kernel,
        out_shape=jax.ShapeDtypeStruct((m_rows, d), F32),
        grid=(m_rows // bm,),
        in_specs=[
            pl.BlockSpec((bm, d), lambda i: (i, 0)),
            pl.BlockSpec((1, d), lambda i: (0, 0)),
        ],
        out_specs=pl.BlockSpec((bm, d), lambda i: (i, 0)),
        compiler_params=_params(1),
        name="final_norm",
    )(x, g.reshape(1, d))


def _norm_router(cfg, x, g, mod, shift, scale, wr_pad, br_pad, bm=256):
    m, d = x.shape
    ep = wr_pad.shape[1]
    return pl.pallas_call(
        functools.partial(_norm_router_kernel, shift=shift, scale=scale, top_k=cfg.top_k),
        out_shape=(jax.ShapeDtypeStruct((m, d), BF16), jax.ShapeDtypeStruct((m, ep), F32)),
        grid=(m // bm,),
        in_specs=[
            pl.BlockSpec((bm, d), lambda i: (i, 0)),
            pl.BlockSpec((1, d), lambda i: (0, 0)),
            pl.BlockSpec((1, N_MOD, d), lambda i: (_seg_of_block(cfg, i, bm), 0, 0)),
            pl.BlockSpec((d, ep), lambda i: (0, 0)),
            pl.BlockSpec((1, ep), lambda i: (0, 0)),
        ],
        out_specs=(pl.BlockSpec((bm, d), lambda i: (i, 0)), pl.BlockSpec((bm, ep), lambda i: (i, 0))),
        compiler_params=_params(1),
        name="norm_router",
    )(x, g.reshape(1, d), mod, wr_pad, br_pad)


def _mm_kernel(x_ref, w_ref, o_ref):
    o_ref[...] = _dot(x_ref[...], w_ref[...]).astype(o_ref.dtype)


def _mm_res_kernel(x_ref, w_ref, res_ref, mod_ref, o_ref, *, gate):
    o_ref[...] = res_ref[...] + mod_ref[0, gate:gate + 1, :] * _dot(x_ref[...], w_ref[...])


def _matmul(x, w, bm, bn, out_dtype=BF16):
    m, k = x.shape
    n = w.shape[1]
    return pl.pallas_call(
        _mm_kernel,
        out_shape=jax.ShapeDtypeStruct((m, n), out_dtype),
        grid=(n // bn, m // bm),
        in_specs=[
            pl.BlockSpec((bm, k), lambda j, i: (i, 0)),
            pl.BlockSpec((k, bn), lambda j, i: (0, j)),
        ],
        out_specs=pl.BlockSpec((bm, bn), lambda j, i: (i, j)),
        compiler_params=_params(2),
        name="matmul",
    )(x, w)


def _matmul_residual(cfg, x, w, res, mod, gate, bm, bn):
    m, k = x.shape
    n = w.shape[1]
    return pl.pallas_call(
        functools.partial(_mm_res_kernel, gate=gate),
        out_shape=jax.ShapeDtypeStruct((m, n), F32),
        grid=(n // bn, m // bm),
        in_specs=[
            pl.BlockSpec((bm, k), lambda j, i: (i, 0)),
            pl.BlockSpec((k, bn), lambda j, i: (0, j)),
            pl.BlockSpec((bm, bn), lambda j, i: (i, j)),
            pl.BlockSpec((1, N_MOD, bn), lambda j, i: (_seg_of_block(cfg, i, bm), 0, j)),
        ],
        out_specs=pl.BlockSpec((bm, bn), lambda j, i: (i, j)),
        compiler_params=_params(2),
        name="matmul_residual",
    )(x, w, res, mod)


def _conv_a_kernel(b_ref, c_ref, h_ref, w_ref, o_ref, *, tt):
    t_len = b_ref.shape[0]
    w = w_ref[...]
    halo = 16

    def prod(r0, n):
        return c_ref[pl.ds(r0, n), :].astype(F32) * h_ref[pl.ds(r0, n), :].astype(F32)

    def chunk(ci, carry):
        r0 = pl.multiple_of(ci * tt, tt)
        u = prod(r0, tt)
        prev = prod(pl.multiple_of(jnp.maximum(r0 - halo, 0), halo), halo)[halo - 1:halo, :]
        prev = jnp.where(r0 == 0, 0.0, prev)
        nxt = prod(pl.multiple_of(jnp.minimum(r0 + tt, t_len - halo), halo), halo)[0:1, :]
        nxt = jnp.where(r0 + tt == t_len, 0.0, nxt)
        row = lax.broadcasted_iota(jnp.int32, u.shape, 0)
        up = jnp.where(row == 0, prev, pltpu.roll(u, 1, 0))
        dn = jnp.where(row == tt - 1, nxt, pltpu.roll(u, tt - 1, 0))
        conv = w[0:1, :] * up + w[1:2, :] * u + w[2:3, :] * dn
        o_ref[pl.ds(r0, tt), :] = (b_ref[pl.ds(r0, tt), :].astype(F32) * conv).astype(o_ref.dtype)
        return carry

    lax.fori_loop(0, t_len // tt, chunk, 0)


def _conv_mixer(cfg, p, w, t_len, row_blk0, n_seq, out_rows, bc=256):
    wd = cfg.branch_w
    ncb = wd // bc
    tt = min(256, t_len)
    spec = lambda off: pl.BlockSpec((t_len, bc), lambda s, j: (row_blk0 + s, off * ncb + j))
    return pl.pallas_call(
        functools.partial(_conv_a_kernel, tt=tt),
        out_shape=jax.ShapeDtypeStruct((out_rows, wd), BF16),
        grid=(n_seq, ncb),
        in_specs=[spec(0), spec(1), spec(2), pl.BlockSpec((3, bc), lambda s, j: (0, j))],
        out_specs=pl.BlockSpec((t_len, bc), lambda s, j: (s, j)),
        compiler_params=_params(2),
        name="conv_mixer",
    )(p, p, p, w)


def _rope_tables(cfg, bt):
    t = jnp.arange(cfg.seq)
    pos = jnp.stack([t // cfg.grid_w, t % cfg.grid_w], axis=-1).astype(F32)
    inv = ROPE_THETA ** (-jnp.arange(ROPE_PAIRS, dtype=F32) * 2.0 / ROPE_AXIS_DIM)
    ang = pos[:, :, None] * inv
    cos, sin = jnp.cos(ang), jnp.sin(ang)
    cmap = jnp.concatenate([cos[:, 0], cos[:, 0], cos[:, 1], cos[:, 1]], axis=-1)
    smap = jnp.concatenate([-sin[:, 0], sin[:, 0], -sin[:, 1], sin[:, 1]], axis=-1)
    ctab = jnp.concatenate([jnp.tile(cmap, (1, 2)), jnp.ones((bt, ATT_VD), F32)], axis=0)
    stab = jnp.concatenate([jnp.tile(smap, (1, 2)), jnp.zeros((bt, ATT_VD), F32)], axis=0)
    return ctab, stab


def _att_prep_kernel(q_ref, k_ref, v_ref, c_ref, s_ref, qt_ref, kr_ref, vt_ref, *, heads):
    cos = c_ref[...]
    sin = s_ref[...]
    lane = lax.broadcasted_iota(jnp.int32, cos.shape, 1)
    low = (lane % ROPE_AXIS_DIM) < ROPE_PAIRS

    def rope(u):
        swapped = jnp.where(low, pltpu.roll(u, ATT_VD - ROPE_PAIRS, 1), pltpu.roll(u, ROPE_PAIRS, 1))
        return u * cos + swapped * sin

    for h in range(heads):
        sl = slice(h * ATT_VD, (h + 1) * ATT_VD)
        qt_ref[sl, :] = rope(q_ref[:, sl].astype(F32)).T.astype(qt_ref.dtype)
        kr_ref[:, sl] = rope(k_ref[:, sl].astype(F32)).astype(kr_ref.dtype)
        vt_ref[sl, :] = v_ref[:, sl].astype(F32).T.astype(vt_ref.dtype)


def _att_prep(cfg, p, ctab, stab, bt=256):
    m, wd = cfg.m_all, cfg.branch_w
    n_lat_blk = cfg.m_lat // bt
    n_seq_blk = cfg.seq // bt
    tab_map = lambda i: (jnp.where(i < n_lat_blk, i % n_seq_blk, n_seq_blk), 0)
    return pl.pallas_call(
        functools.partial(_att_prep_kernel, heads=cfg.heads),
        out_shape=(jax.ShapeDtypeStruct((wd, m), BF16), jax.ShapeDtypeStruct((m, wd), BF16),
                   jax.ShapeDtypeStruct((wd, m), BF16)),
        grid=(m // bt,),
        in_specs=[
            pl.BlockSpec((bt, wd), lambda i: (i, 3)),
            pl.BlockSpec((bt, wd), lambda i: (i, 4)),
            pl.BlockSpec((bt, wd), lambda i: (i, 5)),
            pl.BlockSpec((bt, ATT_VD), tab_map),
            pl.BlockSpec((bt, ATT_VD), tab_map),
        ],
        out_specs=(pl.BlockSpec((wd, bt), lambda i: (0, i)), pl.BlockSpec((bt, wd), lambda i: (i, 0)),
                   pl.BlockSpec((wd, bt), lambda i: (0, i))),
        compiler_params=_params(1),
        name="att_prep",
    )(p, p, p, ctab, stab)


def _attn_kernel(lam_ref, g_ref, qt_ref, *refs, seg_lens, tk, lam_init):
    n_seg = len(seg_lens)
    kv = refs[:2 * n_seg]
    o_ref = refs[2 * n_seg]
    s1_ref, s2_ref, acc1_ref, acc2_ref = refs[2 * n_seg + 1:]
    bq = qt_ref.shape[1]

    qt = qt_ref[...].astype(F32) * ATT_HD ** -0.5
    sub = lax.broadcasted_iota(jnp.int32, qt.shape, 0)
    q1 = jnp.where(sub < ATT_HD, qt, 0.0).astype(qt_ref.dtype)
    q2 = jnp.where(sub >= ATT_HD, qt, 0.0).astype(qt_ref.dtype)

    def chunks(seg):
        tkk = min(tk, seg_lens[seg])
        return tkk, seg_lens[seg] // tkk

    m1 = jnp.full((1, bq), -jnp.inf, F32)
    m2 = jnp.full((1, bq), -jnp.inf, F32)
    off = 0
    for seg in range(n_seg):
        k_ref = kv[2 * seg]
        tkk, n_chunk = chunks(seg)

        def body_a(c, carry, k_ref=k_ref, tkk=tkk, off=off):
            m1, m2 = carry
            r0 = pl.multiple_of(c * tkk, tkk)
            kc = k_ref[pl.ds(r0, tkk), :]
            s1 = _dot(kc, q1)
            s2 = _dot(kc, q2)
            s1_ref[pl.ds(off + r0, tkk), :] = s1
            s2_ref[pl.ds(off + r0, tkk), :] = s2
            return (jnp.maximum(m1, jnp.max(s1, axis=0, keepdims=True)),
                    jnp.maximum(m2, jnp.max(s2, axis=0, keepdims=True)))

        m1, m2 = lax.fori_loop(0, n_chunk, body_a, (m1, m2))
        off += seg_lens[seg]

    acc1_ref[...] = jnp.zeros_like(acc1_ref)
    acc2_ref[...] = jnp.zeros_like(acc2_ref)
    l1 = jnp.zeros((1, bq), F32)
    l2 = jnp.zeros((1, bq), F32)
    off = 0
    for seg in range(n_seg):
        vt_ref = kv[2 * seg + 1]
        tkk, n_chunk = chunks(seg)

        def body_b(c, carry, vt_ref=vt_ref, tkk=tkk, off=off):
            l1, l2 = carry
            r0 = pl.multiple_of(c * tkk, tkk)
            p1 = jnp.exp(s1_ref[pl.ds(off + r0, tkk), :] - m1)
            p2 = jnp.exp(s2_ref[pl.ds(off + r0, tkk), :] - m2)
            vt = vt_ref[:, pl.ds(r0, tkk)]
            acc1_ref[...] += _dot(vt, p1.astype(vt.dtype))
            acc2_ref[...] += _dot(vt, p2.astype(vt.dtype))
            return (l1 + jnp.sum(p1, axis=0, keepdims=True), l2 + jnp.sum(p2, axis=0, keepdims=True))

        l1, l2 = lax.fori_loop(0, n_chunk, body_b, (l1, l2))
        off += seg_lens[seg]

    lq = lam_ref[...]
    lam = (jnp.exp(jnp.sum(lq[0:1] * lq[1:2], axis=-1, keepdims=True))
           - jnp.exp(jnp.sum(lq[2:3] * lq[3:4], axis=-1, keepdims=True)) + lam_init)
    ot = acc1_ref[...] * (1.0 / l1) - acc2_ref[...] * (lam / l2)
    ms = jnp.mean(ot * ot, axis=0, keepdims=True)
    y = ot * lax.rsqrt(ms + EPS) * g_ref[...] * (1.0 - lam_init)
    o_ref[...] = y.T.astype(o_ref.dtype)


def _attention(cfg, lam_qk, subln_g, qt, segs, q_col_blk0, n_q_per_batch, out_rows, lam_init, bq=256, tk=512):
    wd = cfg.branch_w
    in_specs = [
        pl.BlockSpec((4, ATT_HD), lambda b, h, i: (0, 0)),
        pl.BlockSpec((ATT_VD, 1), lambda b, h, i: (0, 0)),
        pl.BlockSpec((ATT_VD, bq), lambda b, h, i: (h, q_col_blk0 + b * n_q_per_batch + i)),
    ]
    args = [lam_qk, subln_g.reshape(ATT_VD, 1), qt]
    seg_lens = []
    for k_arr, k_blk, k_col0, vt_arr, length in segs:
        in_specs.append(pl.BlockSpec((length, ATT_VD), lambda b, h, i, f=k_blk, c0=k_col0: (f(b), c0 + h)))
        in_specs.append(pl.BlockSpec((ATT_VD, length), lambda b, h, i, f=k_blk: (h, f(b))))
        args += [k_arr, vt_arr]
        seg_lens.append(length)
    l_tot = sum(seg_lens)
    return pl.pallas_call(
        functools.partial(_attn_kernel, seg_lens=tuple(seg_lens), tk=tk, lam_init=lam_init),
        out_shape=jax.ShapeDtypeStruct((out_rows, wd), BF16),
        grid=(cfg.batch, cfg.heads, n_q_per_batch),
        in_specs=in_specs,
        out_specs=pl.BlockSpec((bq, ATT_VD), lambda b, h, i: (b * n_q_per_batch + i, h)),
        scratch_shapes=[pltpu.VMEM((l_tot, bq), F32), pltpu.VMEM((l_tot, bq), F32),
                        pltpu.VMEM((ATT_VD, bq), F32), pltpu.VMEM((ATT_VD, bq), F32)],
        compiler_params=_params(3),
        name="diff_attention",
    )(*args)


SCAN_PAD = SUBLANES


def _scan_levels(n):
    plan = []
    while n > 1:
        fan = SUBLANES if n % SUBLANES == 0 else n
        assert fan <= SUBLANES, "sequence length must factor into groups of at most 8"
        plan.append((n, fan, n // fan))
        n //= fan
    return plan


def _scan_scratch_rows(n):
    rows = 0
    for _, _, parts in _scan_levels(n):
        rows += 2 * SCAN_PAD + -(-parts // SUBLANES) * SUBLANES
    return max(rows, SUBLANES)


def _scan_inplace(a_ref, b_ref, base, n, rev, sa_ref, sb_ref, off=0):
    if n == 1:
        return
    fan = SUBLANES if n % SUBLANES == 0 else n
    parts = n // fan
    order = list(range(fan))
    if rev:
        order.reverse()

    def rows(r):
        return pl.ds(base + r, parts, stride=fan) if parts > 1 else pl.ds(base + r, 1)

    acc_a = acc_b = None
    for r in order:
        a_r = a_ref[rows(r), :]
        b_r = b_ref[rows(r), :]
        if acc_a is None:
            acc_a, acc_b = a_r, b_r
        else:
            acc_b = a_r * acc_b + b_r
            acc_a = a_r * acc_a
            a_ref[rows(r), :] = acc_a
            b_ref[rows(r), :] = acc_b
    if parts == 1:
        return
    data = off + SCAN_PAD
    ones = jnp.ones((SCAN_PAD, LANES), F32)
    zeros = jnp.zeros((SCAN_PAD, LANES), F32)
    sa_ref[pl.ds(off, SCAN_PAD), :] = ones
    sb_ref[pl.ds(off, SCAN_PAD), :] = zeros
    sa_ref[pl.ds(data + parts, SCAN_PAD), :] = ones
    sb_ref[pl.ds(data + parts, SCAN_PAD), :] = zeros
    sa_ref[pl.ds(data, parts), :] = acc_a
    sb_ref[pl.ds(data, parts), :] = acc_b
    nxt = off + 2 * SCAN_PAD + -(-parts // SUBLANES) * SUBLANES
    _scan_inplace(sa_ref, sb_ref, data, parts, rev, sa_ref, sb_ref, nxt)
    shift = data + 1 if rev else data - 1
    carry_a = sa_ref[pl.ds(shift, parts), :]
    carry_b = sb_ref[pl.ds(shift, parts), :]
    for r in order:
        a_r = a_ref[rows(r), :]
        b_ref[rows(r), :] = b_ref[rows(r), :] + a_r * carry_b
        a_ref[rows(r), :] = a_r * carry_a


def _softplus(x):
    return jnp.maximum(x, 0.0) + jnp.log1p(jnp.exp(-jnp.abs(x)))


def _gelu_tanh(x):
    return 0.5 * x * (1.0 + jnp.tanh(math.sqrt(2.0 / math.pi) * (x + 0.044715 * (x * x * x))))


def _lru_kernel(lxl_ref, lyl_ref, lxc_ref, lyc_ref, cw_ref, cb_ref, wa_ref, ba_ref, wx_ref, bx_ref, lam_ref,
                ol_ref, oc_ref, pad_ref, a0_ref, b0_ref, a1_ref, b1_ref, sa_ref, sb_ref, *, tt):
    l_ctx = lxc_ref.shape[0]
    l_lat = lxl_ref.shape[0]
    cw = cw_ref[...]
    cb = cb_ref[...]
    ab_refs = ((a0_ref, b0_ref), (a1_ref, b1_ref))
    neg_sp = [-LRU_C * _softplus(-lam_ref[d:d + 1, :]) for d in range(2)]
    wa = [wa_ref[d].astype(BF16) for d in range(2)]
    wx = [wx_ref[d].astype(BF16) for d in range(2)]

    for lx_ref, base, length, is_ctx in ((lxc_ref, 0, l_ctx, True), (lxl_ref, l_ctx, l_lat, False)):
        ttk = min(tt, length)
        pad_ref[pl.ds(0, SUBLANES), :] = jnp.zeros((SUBLANES, LANES), F32)
        pad_ref[pl.ds(SUBLANES + length, SUBLANES), :] = jnp.zeros((SUBLANES, LANES), F32)

        def fill(ci, carry, lx_ref=lx_ref, ttk=ttk):
            r0 = pl.multiple_of(ci * ttk, ttk)
            pad_ref[pl.ds(SUBLANES + r0, ttk), :] = lx_ref[pl.ds(r0, ttk), :].astype(F32)
            return carry

        lax.fori_loop(0, length // ttk, fill, 0)

        def gates(ci, carry, base=base, length=length, is_ctx=is_ctx, ttk=ttk):
            r0 = pl.multiple_of(ci * ttk, ttk)
            xx = pad_ref[pl.ds(r0, ttk + 2 * SUBLANES), :]
            u = (cw[0:1, :] * xx[6:6 + ttk] + cw[1:2, :] * xx[7:7 + ttk]
                 + cw[2:3, :] * xx[8:8 + ttk] + cw[3:4, :] * xx[9:9 + ttk]) + cb
            ub = u.astype(BF16)
            row = r0 + lax.broadcasted_iota(jnp.int32, u.shape, 0)
            for d in range(2):
                r_gate = _sigmoid(_dot(ub, wa[d]) + ba_ref[d:d + 1, :])
                i_gate = _sigmoid(_dot(ub, wx[d]) + bx_ref[d:d + 1, :])
                log_a = neg_sp[d] * r_gate
                a = jnp.exp(log_a)
                mult = jnp.sqrt(-jnp.tanh(log_a) * (a * a + 1.0))
                if is_ctx:
                    mult = jnp.where(row == (length - 1 if d == 1 else 0), 1.0, mult)
                a_ref, b_ref = ab_refs[d]
                a_ref[pl.ds(base + r0, ttk), :] = a
                b_ref[pl.ds(base + r0, ttk), :] = mult * (i_gate * u)
            return carry

        lax.fori_loop(0, length // ttk, gates, 0)

    for d, rev in ((0, False), (1, True)):
        a_ref, b_ref = ab_refs[d]
        _scan_inplace(a_ref, b_ref, 0, l_ctx, rev, sa_ref, sb_ref)
        _scan_inplace(a_ref, b_ref, l_ctx, l_lat, rev, sa_ref, sb_ref)

    h0 = (b0_ref[pl.ds(l_ctx - 1, 1), :], b1_ref[pl.ds(0, 1), :])

    for ly_ref, o_ref, base, length, is_ctx in ((lyc_ref, oc_ref, 0, l_ctx, True), (lyl_ref, ol_ref, l_ctx, l_lat, False)):
        ttk = min(tt, length)

        def emit(ci, carry, ly_ref=ly_ref, o_ref=o_ref, base=base, is_ctx=is_ctx, ttk=ttk):
            r0 = pl.multiple_of(ci * ttk, ttk)
            h = b0_ref[pl.ds(base + r0, ttk), :] + b1_ref[pl.ds(base + r0, ttk), :]
            if not is_ctx:
                h = h + a0_ref[pl.ds(base + r0, ttk), :] * h0[0] + a1_ref[pl.ds(base + r0, ttk), :] * h0[1]
            o_ref[pl.ds(r0, ttk), :] = (h * _gelu_tanh(ly_ref[pl.ds(r0, ttk), :].astype(F32))).astype(o_ref.dtype)
            return carry

        lax.fori_loop(0, length // ttk, emit, 0)


def _lru(cfg, p, conv_w, conv_b, w_a, b_a, w_x, b_x, lam_p, tt=256):
    wd, s, lc = cfg.branch_w, cfg.seq, cfg.ctx_len
    nb = wd // LANES
    l_tot = s + lc
    ctx_blk0 = cfg.m_lat // lc
    lx_col, ly_col = 6 * nb, 7 * nb
    vec2 = pl.BlockSpec((2, LANES), lambda b, n: (0, n))
    wspec = pl.BlockSpec((2, None, LANES, LANES), lambda b, n: (0, n, 0, 0))
    scan_rows = max(_scan_scratch_rows(s), _scan_scratch_rows(lc))
    return pl.pallas_call(
        functools.partial(_lru_kernel, tt=tt),
        out_shape=(jax.ShapeDtypeStruct((cfg.m_lat, wd), BF16), jax.ShapeDtypeStruct((cfg.batch * lc, wd), BF16)),
        grid=(cfg.batch, nb),
        in_specs=[
            pl.BlockSpec((s, LANES), lambda b, n: (b, lx_col + n)),
            pl.BlockSpec((s, LANES), lambda b, n: (b, ly_col + n)),
            pl.BlockSpec((lc, LANES), lambda b, n: (ctx_blk0 + b, lx_col + n)),
            pl.BlockSpec((lc, LANES), lambda b, n: (ctx_blk0 + b, ly_col + n)),
            pl.BlockSpec((4, LANES), lambda b, n: (0, n)),
            pl.BlockSpec((1, LANES), lambda b, n: (0, n)),
            wspec, vec2, wspec, vec2, vec2,
        ],
        out_specs=(pl.BlockSpec((s, LANES), lambda b, n: (b, n)), pl.BlockSpec((lc, LANES), lambda b, n: (b, n))),
        scratch_shapes=[pltpu.VMEM((s + 2 * SUBLANES, LANES), F32)]
        + [pltpu.VMEM((l_tot, LANES), F32)] * 4
        + [pltpu.VMEM((scan_rows, LANES), F32)] * 2,
        compiler_params=_params(2),
        name="rglru",
    )(p, p, p, p, conv_w, conv_b.reshape(1, wd), w_a, b_a, w_x, b_x, lam_p)


def _merge_kernel(oa_ref, ob_ref, oc_ref, ga_ref, gb_ref, gc_ref, w_ref, o_ref, *, wd):
    acc = None
    for i, (x_ref, g_ref) in enumerate(((oa_ref, ga_ref), (ob_ref, gb_ref), (oc_ref, gc_ref))):
        g = _sigmoid(g_ref[...].astype(F32))
        term = g * _dot(x_ref[...], w_ref[i * wd:(i + 1) * wd, :])
        acc = term if acc is None else acc + term
    o_ref[...] = acc.astype(o_ref.dtype)


def _merge(cfg, oa, ob, oc, p, wb, bm, bn):
    m, wd, d = cfg.m_all, cfg.branch_w, cfg.d_model
    g_blk0 = 8 * wd // bn
    nd = d // bn
    xspec = pl.BlockSpec((bm, wd), lambda j, i: (i, 0))
    gspec = lambda br: pl.BlockSpec((bm, bn), lambda j, i: (i, g_blk0 + br * nd + j))
    return pl.pallas_call(
        functools.partial(_merge_kernel, wd=wd),
        out_shape=jax.ShapeDtypeStruct((m, d), BF16),
        grid=(nd, m // bm),
        in_specs=[xspec, xspec, xspec, gspec(0), gspec(1), gspec(2),
                  pl.BlockSpec((3 * wd, bn), lambda j, i: (0, j))],
        out_specs=pl.BlockSpec((bm, bn), lambda j, i: (i, j)),
        compiler_params=_params(2),
        name="merge_branches",
    )(oa, ob, oc, p, p, p, wb)


def _moe_kernel(x_ref, gates_ref, wg_ref, wl_ref, bg_ref, bl_ref, wd_ref, bd_ref, res_ref, mod_ref, o_ref,
                acc_ref, *, gate, n_experts):
    e = pl.program_id(1)

    @pl.when(e == 0)
    def _():
        acc_ref[...] = _dot3(gates_ref[...], bd_ref[...])

    x = x_ref[...]
    lane = lax.broadcasted_iota(jnp.int32, gates_ref.shape, 1)
    g_e = jnp.sum(jnp.where(lane == e, gates_ref[...], 0.0), axis=-1, keepdims=True)
    h_glu = jnp.minimum(_dot(x, wg_ref[...]) + bg_ref[...], SWIGLU_LIMIT)
    h_lin = jnp.clip(_dot(x, wl_ref[...]) + bl_ref[...], -SWIGLU_LIMIT, SWIGLU_LIMIT)
    act = h_glu * _sigmoid(SWIGLU_ALPHA * h_glu) * (h_lin + 1.0) * g_e
    acc_ref[...] += _dot(act.astype(x.dtype), wd_ref[...])

    @pl.when(e == n_experts - 1)
    def _():
        o_ref[...] = res_ref[...] + mod_ref[0, gate:gate + 1, :] * acc_ref[...]


def _moe_dense(cfg, xn, gates, w_g, w_l, b_g, b_l, w_dn, b_dn_pad, res, mod, gate, bm=512):
    m, d = xn.shape
    ne, ff = cfg.n_experts, cfg.expert_ff
    ep = gates.shape[1]
    return pl.pallas_call(
        functools.partial(_moe_kernel, gate=gate, n_experts=ne),
        out_shape=jax.ShapeDtypeStruct((m, d), F32),
        grid=(m // bm, ne),
        in_specs=[
            pl.BlockSpec((bm, d), lambda i, e: (i, 0)),
            pl.BlockSpec((bm, ep), lambda i, e: (i, 0)),
            pl.BlockSpec((None, d, ff), lambda i, e: (e, 0, 0)),
            pl.BlockSpec((None, d, ff), lambda i, e: (e, 0, 0)),
            pl.BlockSpec((None, 1, ff), lambda i, e: (e, 0, 0)),
            pl.BlockSpec((None, 1, ff), lambda i, e: (e, 0, 0)),
            pl.BlockSpec((None, ff, d), lambda i, e: (e, 0, 0)),
            pl.BlockSpec((ep, d), lambda i, e: (0, 0)),
            pl.BlockSpec((bm, d), lambda i, e: (i, 0)),
            pl.BlockSpec((1, N_MOD, d), lambda i, e: (_seg_of_block(cfg, i, bm), 0, 0)),
        ],
        out_specs=pl.BlockSpec((bm, d), lambda i, e: (i, 0)),
        scratch_shapes=[pltpu.VMEM((bm, d), F32)],
        compiler_params=_params(2),
        name="moe_dense",
    )(xn, gates, w_g, w_l, b_g, b_l, w_dn, b_dn_pad, res, mod)


def _forward(cfg, x, c, ctx, c_ctx, ada_a, ada_b, ada_bias, norm1_g, norm2_g, w_in, conv_a_w, lam_qk,
             subln_g, lru_conv_w, lru_conv_b, lru_w_a, lru_b_a, lru_w_x, lru_b_x, lru_lambda,
             w_branch, w_out, router_w, router_b, w_gu, b_gu, w_dn, b_dn, final_g):
    d, wd, bsz, s, lc = cfg.d_model, cfg.branch_w, cfg.batch, cfg.seq, cfg.ctx_len
    m_lat, m_all, ne = cfg.m_lat, cfg.m_all, cfg.n_experts
    assert bsz * lc <= s and s % lc == 0 and bsz + 1 <= SUBLANES
    bm = min(512, bsz * lc)
    bn = min(1024, d)
    bt = min(256, lc)
    ep = -(-ne // LANES) * LANES

    xs = jnp.concatenate([x.reshape(m_lat, d), ctx.reshape(bsz * lc, d)], axis=0)
    cond8 = jnp.zeros((SUBLANES, d), F32).at[:bsz].set(c).at[bsz].set(c_ctx)
    ctab, stab = _rope_tables(cfg, bt)
    ctx_blk = lambda b: m_lat // lc + b

    for l in range(cfg.depth):
        lam_init = LAMBDA_INIT_BASE - LAMBDA_INIT_AMP * math.exp(-LAMBDA_INIT_RATE * l)
        mod = _ada(cfg, cond8, ada_a[l], ada_b[l], ada_bias[l])

        xn = _norm_mod(cfg, xs, norm1_g[l], mod, 0, 1)
        p = _matmul(xn, w_in[l].astype(BF16), bm, bn)
        oa = jnp.concatenate([
            _conv_mixer(cfg, p, conv_a_w[l], s, 0, bsz, m_lat),
            _conv_mixer(cfg, p, conv_a_w[l], lc, m_lat // lc, bsz, bsz * lc)], axis=0)
        qt, kr, vt = _att_prep(cfg, p, ctab, stab, bt)
        nh = wd // ATT_VD
        ctx_seg = (p, ctx_blk, 4 * nh, vt, lc)
        lat_seg = (kr, lambda b: b, 0, vt, s)
        ob_lat = _attention(cfg, lam_qk[l], subln_g[l], qt, [ctx_seg, lat_seg], 0, s // bt, m_lat, lam_init, bq=bt)
        ob_ctx = _attention(cfg, lam_qk[l], subln_g[l], qt, [ctx_seg], m_lat // bt, lc // bt, bsz * lc, lam_init, bq=bt)
        ob = jnp.concatenate([ob_lat, ob_ctx], axis=0)
        oc = jnp.concatenate(_lru(cfg, p, lru_conv_w[l], lru_conv_b[l], lru_w_a[l], lru_b_a[l],
                                  lru_w_x[l], lru_b_x[l], lru_lambda[l]), axis=0)
        y = _merge(cfg, oa, ob, oc, p, w_branch[l].astype(BF16), bm, bn)
        xs = _matmul_residual(cfg, y, w_out[l].astype(BF16), xs, mod, 2, bm, bn)

        wr_pad = jnp.zeros((d, ep), F32).at[:, :ne].set(router_w[l])
        br_pad = jnp.full((1, ep), -jnp.inf, F32).at[0, :ne].set(router_b[l])
        xn2, gates = _norm_router(cfg, xs, norm2_g[l], mod, 3, 4, wr_pad, br_pad)
        w_g = w_gu[l][..., 0::2].astype(BF16)
        w_l = w_gu[l][..., 1::2].astype(BF16)
        b_g = b_gu[l][:, None, 0::2]
        b_l = b_gu[l][:, None, 1::2]
        b_dn_pad = jnp.zeros((ep, d), F32).at[:ne].set(b_dn[l])
        xs = _moe_dense(cfg, xn2, gates, w_g, w_l, b_g, b_l, w_dn[l].astype(BF16), b_dn_pad, xs, mod, 5,
                        min(256, bm))

    return _norm_plain(xs, final_g, m_lat).reshape(bsz, s, d)


def kernel(x, c, ctx, c_ctx, ada_a, ada_b, ada_bias, norm1_g, norm2_g, w_in, conv_a_w, lam_qk, subln_g, lru_conv_w, lru_conv_b, lru_w_a, lru_b_a, lru_w_x, lru_b_x, lru_lambda, w_branch, w_out, router_w, router_b, w_gu, b_gu, w_dn, b_dn, final_g):
    return _forward(_Cfg(), x, c, ctx, c_ctx, ada_a, ada_b, ada_bias, norm1_g, norm2_g, w_in, conv_a_w, lam_qk,
                    subln_g, lru_conv_w, lru_conv_b, lru_w_a, lru_b_a, lru_w_x, lru_b_x, lru_lambda,
                    w_branch, w_out, router_w, router_b, w_gu, b_gu, w_dn, b_dn, final_g)
```

```python
import functools
import math
from typing import NamedTuple

import jax
import jax.numpy as jnp
from jax import lax
from jax.experimental import pallas as pl
from jax.experimental.pallas import tpu as pltpu

F32 = jnp.float32
BF16 = jnp.bfloat16

LANES = 128
SUBLANES = 8
EPS = 1e-6
ATT_HD = 64
ATT_VD = 2 * ATT_HD
ROPE_THETA = 10000.0
ROPE_AXIS_DIM = ATT_HD // 2
ROPE_PAIRS = ROPE_AXIS_DIM // 2
LAMBDA_INIT_BASE = 0.8
LAMBDA_INIT_AMP = 0.6
LAMBDA_INIT_RATE = 0.3
LRU_C = 8.0
SWIGLU_ALPHA = 1.702
SWIGLU_LIMIT = 7.0
N_MOD = 6
VMEM_LIMIT = 56 * 1024 * 1024


class _Cfg(NamedTuple):
    d_model: int = 4096
    batch: int = 2
    seq: int = 4096
    depth: int = 4
    grid_w: int = 64
    ctx_len: int = 256
    branch_w: int = 1024
    n_experts: int = 32
    top_k: int = 4
    expert_ff: int = 256

    @property
    def heads(self):
        return self.branch_w // ATT_VD

    @property
    def m_lat(self):
        return self.batch * self.seq

    @property
    def m_all(self):
        return self.batch * (self.seq + self.ctx_len)

    @property
    def in_cols(self):
        return 8 * self.branch_w + 3 * self.d_model


def _params(n_grid_dims):
    return pltpu.CompilerParams(
        dimension_semantics=("arbitrary",) * n_grid_dims, vmem_limit_bytes=VMEM_LIMIT)


def _dot(a, b):
    return jnp.dot(a, b, preferred_element_type=F32)


def _split_bf16(x):
    hi = x.astype(BF16)
    lo = (x - hi.astype(F32)).astype(BF16)
    return hi, lo


def _dot3(x, w):
    xh, xl = _split_bf16(x)
    wh, wl = _split_bf16(w)
    return _dot(xh, wh) + (_dot(xl, wh) + _dot(xh, wl))


def _sigmoid(x):
    return 1.0 / (1.0 + jnp.exp(-x))


def _ada_kernel(cond_ref, a_ref, b_ref, bias_ref, o_ref):
    cond = cond_ref[...]
    t = _dot3(cond * _sigmoid(cond), a_ref[...])
    o_ref[...] = _dot3(t, b_ref[...]) + bias_ref[...]


def _ada(cfg, cond8, a, b, bias):
    d = cfg.d_model
    rank = a.shape[1]
    out = pl.pallas_call(
        _ada_kernel,
        out_shape=jax.ShapeDtypeStruct((SUBLANES, N_MOD * d), F32),
        grid=(N_MOD,),
        in_specs=[
            pl.BlockSpec((SUBLANES, d), lambda j: (0, 0)),
            pl.BlockSpec((d, rank), lambda j: (0, 0)),
            pl.BlockSpec((rank, d), lambda j: (0, j)),
            pl.BlockSpec((1, d), lambda j: (0, j)),
        ],
        out_specs=pl.BlockSpec((SUBLANES, d), lambda j: (0, j)),
        compiler_params=_params(1),
        name="ada_modulation",
    )(cond8, a, b, bias.reshape(1, N_MOD * d))
    return out.reshape(SUBLANES, N_MOD, d)[: cfg.batch + 1]


def _rms(x, g):
    return x * lax.rsqrt(jnp.mean(x * x, axis=-1, keepdims=True) + EPS) * g


def _norm_mod_kernel(x_ref, g_ref, mod_ref, o_ref, *, shift, scale):
    y = _rms(x_ref[...], g_ref[...])
    y = y * (1.0 + mod_ref[0, scale:scale + 1, :]) + mod_ref[0, shift:shift + 1, :]
    o_ref[...] = y.astype(o_ref.dtype)


def _norm_plain_kernel(x_ref, g_ref, o_ref):
    o_ref[...] = _rms(x_ref[...], g_ref[...]).astype(o_ref.dtype)


def _pack_pair(lo, hi):
    lo_bits = lax.bitcast_convert_type(lo.astype(BF16).astype(F32), jnp.uint32)
    hi_bits = lax.bitcast_convert_type(hi.astype(BF16).astype(F32), jnp.uint32)
    return (lo_bits >> 16) | (hi_bits & jnp.uint32(0xFFFF0000))


def _unpack_pair(word):
    lo = lax.bitcast_convert_type(word << 16, F32)
    hi = lax.bitcast_convert_type(word & jnp.uint32(0xFFFF0000), F32)
    return lo, hi


def _norm_router_kernel(x_ref, g_ref, mod_ref, wr_ref, br_ref, xq_ref, route_ref, counts_ref, carry_ref,
                        *, shift, scale, top_k):
    @pl.when(pl.program_id(0) == 0)
    def _():
        carry_ref[...] = jnp.zeros_like(carry_ref)

    y = _rms(x_ref[...], g_ref[...])
    y = y * (1.0 + mod_ref[0, scale:scale + 1, :]) + mod_ref[0, shift:shift + 1, :]
    half = y.shape[1] // 2
    xq_ref[...] = _pack_pair(y[:, :half], y[:, half:])
    work = _dot3(y, wr_ref[...]) + br_ref[...]
    bm, n_lane = work.shape
    lane = lax.broadcasted_iota(jnp.int32, work.shape, 1).astype(F32)
    vals, hots, ids = [], [], []
    for _ in range(top_k):
        m = jnp.max(work, axis=-1, keepdims=True)
        idx = jnp.min(jnp.where(work == m, lane, float(n_lane)), axis=-1, keepdims=True)
        hot = lane == idx
        vals.append(m)
        hots.append(hot)
        ids.append(idx)
        work = jnp.where(hot, -jnp.inf, work)
    exps = [jnp.exp(v - vals[0]) for v in vals]
    denom = exps[0]
    for e in exps[1:]:
        denom = denom + e
    picked = jnp.zeros(work.shape, F32)
    for hot in hots:
        picked = jnp.where(hot, 1.0, picked)
    tri = (lax.broadcasted_iota(jnp.int32, (bm, bm), 0) > lax.broadcasted_iota(jnp.int32, (bm, bm), 1))
    before = _dot(jnp.where(tri, 1.0, 0.0).astype(BF16), picked.astype(BF16)) + carry_ref[0:1, :]
    carry_ref[...] = carry_ref[...] + jnp.sum(picked, axis=0, keepdims=True)
    counts_ref[...] = carry_ref[...]
    route = jnp.zeros(work.shape, F32)
    for k in range(top_k):
        rank = jnp.sum(jnp.where(hots[k], before, 0.0), axis=-1, keepdims=True)
        route = jnp.where(lane == float(k), ids[k], route)
        route = jnp.where(lane == float(top_k + k), exps[k] / denom, route)
        route = jnp.where(lane == float(2 * top_k + k), rank, route)
    route_ref[...] = route


def _seg_of_block(cfg, i, bm):
    return (i * bm) // cfg.seq


def _norm_mod(cfg, x, g, mod, shift, scale, bm=256):
    m, d = x.shape
    return pl.pallas_call(
        functools.partial(_norm_mod_kernel, shift=shift, scale=scale),
        out_shape=jax.ShapeDtypeStruct((m, d), BF16),
        grid=(m // bm,),
        in_specs=[
            pl.BlockSpec((bm, d), lambda i: (i, 0)),
            pl.BlockSpec((1, d), lambda i: (0, 0)),
            pl.BlockSpec((1, N_MOD, d), lambda i: (_seg_of_block(cfg, i, bm), 0, 0)),
        ],
        out_specs=pl.BlockSpec((bm, d), lambda i: (i, 0)),
        compiler_params=_params(1),
        name="norm_mod",
    )(x, g.reshape(1, d), mod)


def _norm_plain(x, g, m_rows, bm=256):
    d = x.shape[1]
    return pl.pallas_call(
        _norm_plain_kernel,
        out_shape=jax.ShapeDtypeStruct((m_rows, d), F32),
        grid=(m_rows // bm,),
        in_specs=[
            pl.BlockSpec((bm, d), lambda i: (i, 0)),
            pl.BlockSpec((1, d), lambda i: (0, 0)),
        ],
        out_specs=pl.BlockSpec((bm, d), lambda i: (i, 0)),
        compiler_params=_params(1),
        name="final_norm",
    )(x, g.reshape(1, d))


def _norm_router(cfg, x, g, mod, shift, scale, wr_pad, br_pad, bm=256):
    m, d = x.shape
    ep = wr_pad.shape[1]
    return pl.pallas_call(
        functools.partial(_norm_router_kernel, shift=shift, scale=scale, top_k=cfg.top_k),
        out_shape=(jax.ShapeDtypeStruct((m, d // 2), jnp.uint32), jax.ShapeDtypeStruct((m, ep), F32),
                   jax.ShapeDtypeStruct((SUBLANES, ep), F32)),
        grid=(m // bm,),
        in_specs=[
            pl.BlockSpec((bm, d), lambda i: (i, 0)),
            pl.BlockSpec((1, d), lambda i: (0, 0)),
            pl.BlockSpec((1, N_MOD, d), lambda i: (_seg_of_block(cfg, i, bm), 0, 0)),
            pl.BlockSpec((d, ep), lambda i: (0, 0)),
            pl.BlockSpec((1, ep), lambda i: (0, 0)),
        ],
        out_specs=(pl.BlockSpec((bm, d // 2), lambda i: (i, 0)), pl.BlockSpec((bm, ep), lambda i: (i, 0)),
                   pl.BlockSpec((SUBLANES, ep), lambda i: (0, 0))),
        scratch_shapes=[pltpu.VMEM((SUBLANES, ep), F32)],
        compiler_params=_params(1),
        name="norm_router",
    )(x, g.reshape(1, d), mod, wr_pad, br_pad)


def _mm_kernel(x_ref, w_ref, o_ref):
    o_ref[...] = _dot(x_ref[...], w_ref[...]).astype(o_ref.dtype)


def _mm_res_kernel(x_ref, w_ref, res_ref, mod_ref, o_ref, *, gate):
    o_ref[...] = res_ref[...] + mod_ref[0, gate:gate + 1, :] * _dot(x_ref[...], w_ref[...])


def _matmul(x, w, bm, bn, out_dtype=BF16):
    m, k = x.shape
    n = w.shape[1]
    return pl.pallas_call(
        _mm_kernel,
        out_shape=jax.ShapeDtypeStruct((m, n), out_dtype),
        grid=(n // bn, m // bm),
        in_specs=[
            pl.BlockSpec((bm, k), lambda j, i: (i, 0)),
            pl.BlockSpec((k, bn), lambda j, i: (0, j)),
        ],
        out_specs=pl.BlockSpec((bm, bn), lambda j, i: (i, j)),
        compiler_params=_params(2),
        name="matmul",
    )(x, w)


def _matmul_residual(cfg, x, w, res, mod, gate, bm, bn):
    m, k = x.shape
    n = w.shape[1]
    return pl.pallas_call(
        functools.partial(_mm_res_kernel, gate=gate),
        out_shape=jax.ShapeDtypeStruct((m, n), F32),
        grid=(n // bn, m // bm),
        in_specs=[
            pl.BlockSpec((bm, k), lambda j, i: (i, 0)),
            pl.BlockSpec((k, bn), lambda j, i: (0, j)),
            pl.BlockSpec((bm, bn), lambda j, i: (i, j)),
            pl.BlockSpec((1, N_MOD, bn), lambda j, i: (_seg_of_block(cfg, i, bm), 0, j)),
        ],
        out_specs=pl.BlockSpec((bm, bn), lambda j, i: (i, j)),
        compiler_params=_params(2),
        name="matmul_residual",
    )(x, w, res, mod)


def _conv_a_kernel(b_ref, c_ref, h_ref, w_ref, o_ref, *, tt):
    t_len = b_ref.shape[0]
    w = w_ref[...]
    halo = 16

    def prod(r0, n):
        return c_ref[pl.ds(r0, n), :].astype(F32) * h_ref[pl.ds(r0, n), :].astype(F32)

    def chunk(ci, carry):
        r0 = pl.multiple_of(ci * tt, tt)
        u = prod(r0, tt)
        prev = prod(pl.multiple_of(jnp.maximum(r0 - halo, 0), halo), halo)[halo - 1:halo, :]
        prev = jnp.where(r0 == 0, 0.0, prev)
        nxt = prod(pl.multiple_of(jnp.minimum(r0 + tt, t_len - halo), halo), halo)[0:1, :]
        nxt = jnp.where(r0 + tt == t_len, 0.0, nxt)
        row = lax.broadcasted_iota(jnp.int32, u.shape, 0)
        up = jnp.where(row == 0, prev, pltpu.roll(u, 1, 0))
        dn = jnp.where(row == tt - 1, nxt, pltpu.roll(u, tt - 1, 0))
        conv = w[0:1, :] * up + w[1:2, :] * u + w[2:3, :] * dn
        o_ref[pl.ds(r0, tt), :] = (b_ref[pl.ds(r0, tt), :].astype(F32) * conv).astype(o_ref.dtype)
        return carry

    lax.fori_loop(0, t_len // tt, chunk, 0)


def _conv_mixer(cfg, p, w, t_len, row_blk0, n_seq, out_rows, bc=256):
    wd = cfg.branch_w
    ncb = wd // bc
    tt = min(256, t_len)
    spec = lambda off: pl.BlockSpec((t_len, bc), lambda s, j: (row_blk0 + s, off * ncb + j))
    return pl.pallas_call(
        functools.partial(_conv_a_kernel, tt=tt),
        out_shape=jax.ShapeDtypeStruct((out_rows, wd), BF16),
        grid=(n_seq, ncb),
        in_specs=[spec(0), spec(1), spec(2), pl.BlockSpec((3, bc), lambda s, j: (0, j))],
        out_specs=pl.BlockSpec((t_len, bc), lambda s, j: (s, j)),
        compiler_params=_params(2),
        name="conv_mixer",
    )(p, p, p, w)


def _rope_tables(cfg, bt):
    t = jnp.arange(cfg.seq)
    pos = jnp.stack([t // cfg.grid_w, t % cfg.grid_w], axis=-1).astype(F32)
    inv = ROPE_THETA ** (-jnp.arange(ROPE_PAIRS, dtype=F32) * 2.0 / ROPE_AXIS_DIM)
    ang = pos[:, :, None] * inv
    cos, sin = jnp.cos(ang), jnp.sin(ang)
    cmap = jnp.concatenate([cos[:, 0], cos[:, 0], cos[:, 1], cos[:, 1]], axis=-1)
    smap = jnp.concatenate([-sin[:, 0], sin[:, 0], -sin[:, 1], sin[:, 1]], axis=-1)
    ctab = jnp.concatenate([jnp.tile(cmap, (1, 2)), jnp.ones((bt, ATT_VD), F32)], axis=0)
    stab = jnp.concatenate([jnp.tile(smap, (1, 2)), jnp.zeros((bt, ATT_VD), F32)], axis=0)
    return ctab, stab


def _att_prep_kernel(q_ref, k_ref, v_ref, c_ref, s_ref, qt_ref, kr_ref, vt_ref, *, heads):
    cos = c_ref[...]
    sin = s_ref[...]
    lane = lax.broadcasted_iota(jnp.int32, cos.shape, 1)
    low = (lane % ROPE_AXIS_DIM) < ROPE_PAIRS

    def rope(u):
        swapped = jnp.where(low, pltpu.roll(u, ATT_VD - ROPE_PAIRS, 1), pltpu.roll(u, ROPE_PAIRS, 1))
        return u * cos + swapped * sin

    for h in range(heads):
        sl = slice(h * ATT_VD, (h + 1) * ATT_VD)
        qt_ref[sl, :] = rope(q_ref[:, sl].astype(F32)).T.astype(qt_ref.dtype)
        kr_ref[:, sl] = rope(k_ref[:, sl].astype(F32)).astype(kr_ref.dtype)
        vt_ref[sl, :] = v_ref[:, sl].astype(F32).T.astype(vt_ref.dtype)


def _att_prep(cfg, p, ctab, stab, bt=256):
    m, wd = cfg.m_all, cfg.branch_w
    n_lat_blk = cfg.m_lat // bt
    n_seq_blk = cfg.seq // bt
    tab_map = lambda i: (jnp.where(i < n_lat_blk, i % n_seq_blk, n_seq_blk), 0)
    return pl.pallas_call(
        functools.partial(_att_prep_kernel, heads=cfg.heads),
        out_shape=(jax.ShapeDtypeStruct((wd, m), BF16), jax.ShapeDtypeStruct((m, wd), BF16),
                   jax.ShapeDtypeStruct((wd, m), BF16)),
        grid=(m // bt,),
        in_specs=[
            pl.BlockSpec((bt, wd), lambda i: (i, 3)),
            pl.BlockSpec((bt, wd), lambda i: (i, 4)),
            pl.BlockSpec((bt, wd), lambda i: (i, 5)),
            pl.BlockSpec((bt, ATT_VD), tab_map),
            pl.BlockSpec((bt, ATT_VD), tab_map),
        ],
        out_specs=(pl.BlockSpec((wd, bt), lambda i: (0, i)), pl.BlockSpec((bt, wd), lambda i: (i, 0)),
                   pl.BlockSpec((wd, bt), lambda i: (0, i))),
        compiler_params=_params(1),
        name="att_prep",
    )(p, p, p, ctab, stab)


def _attn_kernel(lam_ref, g_ref, qt_ref, *refs, seg_lens, tk, lam_init):
    n_seg = len(seg_lens)
    kv = refs[:2 * n_seg]
    o_ref = refs[2 * n_seg]
    s1_ref, s2_ref, acc1_ref, acc2_ref = refs[2 * n_seg + 1:]
    bq = qt_ref.shape[1]

    qt = qt_ref[...].astype(F32) * ATT_HD ** -0.5
    sub = lax.broadcasted_iota(jnp.int32, qt.shape, 0)
    q1 = jnp.where(sub < ATT_HD, qt, 0.0).astype(qt_ref.dtype)
    q2 = jnp.where(sub >= ATT_HD, qt, 0.0).astype(qt_ref.dtype)

    def chunks(seg):
        tkk = min(tk, seg_lens[seg])
        return tkk, seg_lens[seg] // tkk

    m1 = jnp.full((1, bq), -jnp.inf, F32)
    m2 = jnp.full((1, bq), -jnp.inf, F32)
    off = 0
    for seg in range(n_seg):
        k_ref = kv[2 * seg]
        tkk, n_chunk = chunks(seg)

        def body_a(c, carry, k_ref=k_ref, tkk=tkk, off=off):
            m1, m2 = carry
            r0 = pl.multiple_of(c * tkk, tkk)
            kc = k_ref[pl.ds(r0, tkk), :]
            s1 = _dot(kc, q1)
            s2 = _dot(kc, q2)
            s1_ref[pl.ds(off + r0, tkk), :] = s1
            s2_ref[pl.ds(off + r0, tkk), :] = s2
            return (jnp.maximum(m1, jnp.max(s1, axis=0, keepdims=True)),
                    jnp.maximum(m2, jnp.max(s2, axis=0, keepdims=True)))

        m1, m2 = lax.fori_loop(0, n_chunk, body_a, (m1, m2))
        off += seg_lens[seg]

    acc1_ref[...] = jnp.zeros_like(acc1_ref)
    acc2_ref[...] = jnp.zeros_like(acc2_ref)
    l1 = jnp.zeros((1, bq), F32)
    l2 = jnp.zeros((1, bq), F32)
    off = 0
    for seg in range(n_seg):
        vt_ref = kv[2 * seg + 1]
        tkk, n_chunk = chunks(seg)

        def body_b(c, carry, vt_ref=vt_ref, tkk=tkk, off=off):
            l1, l2 = carry
            r0 = pl.multiple_of(c * tkk, tkk)
            p1 = jnp.exp(s1_ref[pl.ds(off + r0, tkk), :] - m1)
            p2 = jnp.exp(s2_ref[pl.ds(off + r0, tkk), :] - m2)
            vt = vt_ref[:, pl.ds(r0, tkk)]
            acc1_ref[...] += _dot(vt, p1.astype(vt.dtype))
            acc2_ref[...] += _dot(vt, p2.astype(vt.dtype))
            return (l1 + jnp.sum(p1, axis=0, keepdims=True), l2 + jnp.sum(p2, axis=0, keepdims=True))

        l1, l2 = lax.fori_loop(0, n_chunk, body_b, (l1, l2))
        off += seg_lens[seg]

    lq = lam_ref[...]
    lam = (jnp.exp(jnp.sum(lq[0:1] * lq[1:2], axis=-1, keepdims=True))
           - jnp.exp(jnp.sum(lq[2:3] * lq[3:4], axis=-1, keepdims=True)) + lam_init)
    ot = acc1_ref[...] * (1.0 / l1) - acc2_ref[...] * (lam / l2)
    ms = jnp.mean(ot * ot, axis=0, keepdims=True)
    y = ot * lax.rsqrt(ms + EPS) * g_ref[...] * (1.0 - lam_init)
    o_ref[...] = y.T.astype(o_ref.dtype)


def _attention(cfg, lam_qk, subln_g, qt, segs, q_col_blk0, n_q_per_batch, out_rows, lam_init, bq=256, tk=512):
    wd = cfg.branch_w
    in_specs = [
        pl.BlockSpec((4, ATT_HD), lambda b, h, i: (0, 0)),
        pl.BlockSpec((ATT_VD, 1), lambda b, h, i: (0, 0)),
        pl.BlockSpec((ATT_VD, bq), lambda b, h, i: (h, q_col_blk0 + b * n_q_per_batch + i)),
    ]
    args = [lam_qk, subln_g.reshape(ATT_VD, 1), qt]
    seg_lens = []
    for k_arr, k_blk, k_col0, vt_arr, length in segs:
        in_specs.append(pl.BlockSpec((length, ATT_VD), lambda b, h, i, f=k_blk, c0=k_col0: (f(b), c0 + h)))
        in_specs.append(pl.BlockSpec((ATT_VD, length), lambda b, h, i, f=k_blk: (h, f(b))))
        args += [k_arr, vt_arr]
        seg_lens.append(length)
    l_tot = sum(seg_lens)
    return pl.pallas_call(
        functools.partial(_attn_kernel, seg_lens=tuple(seg_lens), tk=tk, lam_init=lam_init),
        out_shape=jax.ShapeDtypeStruct((out_rows, wd), BF16),
        grid=(cfg.batch, cfg.heads, n_q_per_batch),
        in_specs=in_specs,
        out_specs=pl.BlockSpec((bq, ATT_VD), lambda b, h, i: (b * n_q_per_batch + i, h)),
        scratch_shapes=[pltpu.VMEM((l_tot, bq), F32), pltpu.VMEM((l_tot, bq), F32),
                        pltpu.VMEM((ATT_VD, bq), F32), pltpu.VMEM((ATT_VD, bq), F32)],
        compiler_params=_params(3),
        name="diff_attention",
    )(*args)


SCAN_PAD = SUBLANES


def _scan_levels(n):
    plan = []
    while n > 1:
        fan = SUBLANES if n % SUBLANES == 0 else n
        assert fan <= SUBLANES, "sequence length must factor into groups of at most 8"
        plan.append((n, fan, n // fan))
        n //= fan
    return plan


def _scan_scratch_rows(n):
    rows = 0
    for _, _, parts in _scan_levels(n):
        rows += 2 * SCAN_PAD + -(-parts // SUBLANES) * SUBLANES
    return max(rows, SUBLANES)


def _scan_inplace(a_ref, b_ref, base, n, rev, sa_ref, sb_ref, off=0):
    if n == 1:
        return
    fan = SUBLANES if n % SUBLANES == 0 else n
    parts = n // fan
    order = list(range(fan))
    if rev:
        order.reverse()

    def rows(r):
        return pl.ds(base + r, parts, stride=fan) if parts > 1 else pl.ds(base + r, 1)

    acc_a = acc_b = None
    for r in order:
        a_r = a_ref[rows(r), :]
        b_r = b_ref[rows(r), :]
        if acc_a is None:
            acc_a, acc_b = a_r, b_r
        else:
            acc_b = a_r * acc_b + b_r
            acc_a = a_r * acc_a
            a_ref[rows(r), :] = acc_a
            b_ref[rows(r), :] = acc_b
    if parts == 1:
        return
    data = off + SCAN_PAD
    ones = jnp.ones((SCAN_PAD, LANES), F32)
    zeros = jnp.zeros((SCAN_PAD, LANES), F32)
    sa_ref[pl.ds(off, SCAN_PAD), :] = ones
    sb_ref[pl.ds(off, SCAN_PAD), :] = zeros
    sa_ref[pl.ds(data + parts, SCAN_PAD), :] = ones
    sb_ref[pl.ds(data + parts, SCAN_PAD), :] = zeros
    sa_ref[pl.ds(data, parts), :] = acc_a
    sb_ref[pl.ds(data, parts), :] = acc_b
    nxt = off + 2 * SCAN_PAD + -(-parts // SUBLANES) * SUBLANES
    _scan_inplace(sa_ref, sb_ref, data, parts, rev, sa_ref, sb_ref, nxt)
    shift = data + 1 if rev else data - 1
    carry_a = sa_ref[pl.ds(shift, parts), :]
    carry_b = sb_ref[pl.ds(shift, parts), :]
    for r in order:
        a_r = a_ref[rows(r), :]
        b_ref[rows(r), :] = b_ref[rows(r), :] + a_r * carry_b
        a_ref[rows(r), :] = a_r * carry_a


def _softplus(x):
    return jnp.maximum(x, 0.0) + jnp.log1p(jnp.exp(-jnp.abs(x)))


def _gelu_tanh(x):
    return 0.5 * x * (1.0 + jnp.tanh(math.sqrt(2.0 / math.pi) * (x + 0.044715 * (x * x * x))))


def _lru_kernel(lxl_ref, lyl_ref, lxc_ref, lyc_ref, cw_ref, cb_ref, wa_ref, ba_ref, wx_ref, bx_ref, lam_ref,
                ol_ref, oc_ref, pad_ref, a0_ref, b0_ref, a1_ref, b1_ref, sa_ref, sb_ref, *, tt):
    l_ctx = lxc_ref.shape[0]
    l_lat = lxl_ref.shape[0]
    cw = cw_ref[...]
    cb = cb_ref[...]
    ab_refs = ((a0_ref, b0_ref), (a1_ref, b1_ref))
    neg_sp = [-LRU_C * _softplus(-lam_ref[d:d + 1, :]) for d in range(2)]
    wa = [wa_ref[d].astype(BF16) for d in range(2)]
    wx = [wx_ref[d].astype(BF16) for d in range(2)]

    for lx_ref, base, length, is_ctx in ((lxc_ref, 0, l_ctx, True), (lxl_ref, l_ctx, l_lat, False)):
        ttk = min(tt, length)
        pad_ref[pl.ds(0, SUBLANES), :] = jnp.zeros((SUBLANES, LANES), F32)
        pad_ref[pl.ds(SUBLANES + length, SUBLANES), :] = jnp.zeros((SUBLANES, LANES), F32)

        def fill(ci, carry, lx_ref=lx_ref, ttk=ttk):
            r0 = pl.multiple_of(ci * ttk, ttk)
            pad_ref[pl.ds(SUBLANES + r0, ttk), :] = lx_ref[pl.ds(r0, ttk), :].astype(F32)
            return carry

        lax.fori_loop(0, length // ttk, fill, 0)

        def gates(ci, carry, base=base, length=length, is_ctx=is_ctx, ttk=ttk):
            r0 = pl.multiple_of(ci * ttk, ttk)
            xx = pad_ref[pl.ds(r0, ttk + 2 * SUBLANES), :]
            u = (cw[0:1, :] * xx[6:6 + ttk] + cw[1:2, :] * xx[7:7 + ttk]
                 + cw[2:3, :] * xx[8:8 + ttk] + cw[3:4, :] * xx[9:9 + ttk]) + cb
            ub = u.astype(BF16)
            row = r0 + lax.broadcasted_iota(jnp.int32, u.shape, 0)
            for d in range(2):
                r_gate = _sigmoid(_dot(ub, wa[d]) + ba_ref[d:d + 1, :])
                i_gate = _sigmoid(_dot(ub, wx[d]) + bx_ref[d:d + 1, :])
                log_a = neg_sp[d] * r_gate
                a = jnp.exp(log_a)
                mult = jnp.sqrt(-jnp.tanh(log_a) * (a * a + 1.0))
                if is_ctx:
                    mult = jnp.where(row == (length - 1 if d == 1 else 0), 1.0, mult)
                a_ref, b_ref = ab_refs[d]
                a_ref[pl.ds(base + r0, ttk), :] = a
                b_ref[pl.ds(base + r0, ttk), :] = mult * (i_gate * u)
            return carry

        lax.fori_loop(0, length // ttk, gates, 0)

    for d, rev in ((0, False), (1, True)):
        a_ref, b_ref = ab_refs[d]
        _scan_inplace(a_ref, b_ref, 0, l_ctx, rev, sa_ref, sb_ref)
        _scan_inplace(a_ref, b_ref, l_ctx, l_lat, rev, sa_ref, sb_ref)

    h0 = (b0_ref[pl.ds(l_ctx - 1, 1), :], b1_ref[pl.ds(0, 1), :])

    for ly_ref, o_ref, base, length, is_ctx in ((lyc_ref, oc_ref, 0, l_ctx, True), (lyl_ref, ol_ref, l_ctx, l_lat, False)):
        ttk = min(tt, length)

        def emit(ci, carry, ly_ref=ly_ref, o_ref=o_ref, base=base, is_ctx=is_ctx, ttk=ttk):
            r0 = pl.multiple_of(ci * ttk, ttk)
            h = b0_ref[pl.ds(base + r0, ttk), :] + b1_ref[pl.ds(base + r0, ttk), :]
            if not is_ctx:
                h = h + a0_ref[pl.ds(base + r0, ttk), :] * h0[0] + a1_ref[pl.ds(base + r0, ttk), :] * h0[1]
            o_ref[pl.ds(r0, ttk), :] = (h * _gelu_tanh(ly_ref[pl.ds(r0, ttk), :].astype(F32))).astype(o_ref.dtype)
            return carry

        lax.fori_loop(0, length // ttk, emit, 0)


def _lru(cfg, p, conv_w, conv_b, w_a, b_a, w_x, b_x, lam_p, tt=256):
    wd, s, lc = cfg.branch_w, cfg.seq, cfg.ctx_len
    nb = wd // LANES
    l_tot = s + lc
    ctx_blk0 = cfg.m_lat // lc
    lx_col, ly_col = 6 * nb, 7 * nb
    vec2 = pl.BlockSpec((2, LANES), lambda b, n: (0, n))
    wspec = pl.BlockSpec((2, None, LANES, LANES), lambda b, n: (0, n, 0, 0))
    scan_rows = max(_scan_scratch_rows(s), _scan_scratch_rows(lc))
    return pl.pallas_call(
        functools.partial(_lru_kernel, tt=tt),
        out_shape=(jax.ShapeDtypeStruct((cfg.m_lat, wd), BF16), jax.ShapeDtypeStruct((cfg.batch * lc, wd), BF16)),
        grid=(cfg.batch, nb),
        in_specs=[
            pl.BlockSpec((s, LANES), lambda b, n: (b, lx_col + n)),
            pl.BlockSpec((s, LANES), lambda b, n: (b, ly_col + n)),
            pl.BlockSpec((lc, LANES), lambda b, n: (ctx_blk0 + b, lx_col + n)),
            pl.BlockSpec((lc, LANES), lambda b, n: (ctx_blk0 + b, ly_col + n)),
            pl.BlockSpec((4, LANES), lambda b, n: (0, n)),
            pl.BlockSpec((1, LANES), lambda b, n: (0, n)),
            wspec, vec2, wspec, vec2, vec2,
        ],
        out_specs=(pl.BlockSpec((s, LANES), lambda b, n: (b, n)), pl.BlockSpec((lc, LANES), lambda b, n: (b, n))),
        scratch_shapes=[pltpu.VMEM((s + 2 * SUBLANES, LANES), F32)]
        + [pltpu.VMEM((l_tot, LANES), F32)] * 4
        + [pltpu.VMEM((scan_rows, LANES), F32)] * 2,
        compiler_params=_params(2),
        name="rglru",
    )(p, p, p, p, conv_w, conv_b.reshape(1, wd), w_a, b_a, w_x, b_x, lam_p)


def _merge_kernel(oa_ref, ob_ref, oc_ref, ga_ref, gb_ref, gc_ref, w_ref, o_ref, *, wd):
    acc = None
    for i, (x_ref, g_ref) in enumerate(((oa_ref, ga_ref), (ob_ref, gb_ref), (oc_ref, gc_ref))):
        g = _sigmoid(g_ref[...].astype(F32))
        term = g * _dot(x_ref[...], w_ref[i * wd:(i + 1) * wd, :])
        acc = term if acc is None else acc + term
    o_ref[...] = acc.astype(o_ref.dtype)


def _merge(cfg, oa, ob, oc, p, wb, bm, bn):
    m, wd, d = cfg.m_all, cfg.branch_w, cfg.d_model
    g_blk0 = 8 * wd // bn
    nd = d // bn
    xspec = pl.BlockSpec((bm, wd), lambda j, i: (i, 0))
    gspec = lambda br: pl.BlockSpec((bm, bn), lambda j, i: (i, g_blk0 + br * nd + j))
    return pl.pallas_call(
        functools.partial(_merge_kernel, wd=wd),
        out_shape=jax.ShapeDtypeStruct((m, d), BF16),
        grid=(nd, m // bm),
        in_specs=[xspec, xspec, xspec, gspec(0), gspec(1), gspec(2),
                  pl.BlockSpec((3 * wd, bn), lambda j, i: (0, j))],
        out_specs=pl.BlockSpec((bm, bn), lambda j, i: (i, j)),
        compiler_params=_params(2),
        name="merge_branches",
    )(oa, ob, oc, p, p, p, wb)


MOE_TILE = 256
MOE_BLOCK = 256


def _moe_slots(cfg):
    n_tiles = cfg.m_all * cfg.top_k // MOE_TILE + cfg.n_experts
    return n_tiles, n_tiles * MOE_TILE


def _dispatch_kernel(fill_ref, nchunk_ref, nv_ref, pos_ref, xq_hbm, xs_hbm, zero_ref, pos_smem, sem, idx_sem,
                     *, n_experts, top_k, n_tiles):
    i = pl.program_id(0)
    n_assign = pos_ref.shape[0] * pos_ref.shape[1]
    bt = n_assign // top_k

    @pl.when(i == 0)
    def _():
        zero_ref[...] = jnp.zeros_like(zero_ref)
        piece = zero_ref.at[pl.ds(0, SUBLANES)]
        total = 0
        for e in range(n_experts):
            def fill_piece(c, carry, e=e):
                start = pl.multiple_of(fill_ref[e] + c * SUBLANES, SUBLANES)
                pltpu.make_async_copy(piece, xs_hbm.at[pl.ds(start, SUBLANES)], sem).start()
                return carry

            lax.fori_loop(0, nchunk_ref[e], fill_piece, 0)
            total = total + nchunk_ref[e]

        def idle_tile(j, carry):
            start = pl.multiple_of(j * MOE_TILE, MOE_TILE)
            pltpu.make_async_copy(zero_ref, xs_hbm.at[pl.ds(start, MOE_TILE)], sem).start()
            return carry

        lax.fori_loop(nv_ref[0], n_tiles, idle_tile, 0)

        def drain_piece(c, carry):
            pltpu.make_async_copy(piece, xs_hbm.at[pl.ds(0, SUBLANES)], sem).wait()
            return carry

        lax.fori_loop(0, total, drain_piece, 0)

        def drain_tile(j, carry):
            pltpu.make_async_copy(zero_ref, xs_hbm.at[pl.ds(0, MOE_TILE)], sem).wait()
            return carry

        lax.fori_loop(nv_ref[0], n_tiles, drain_tile, 0)

    idx_copy = pltpu.make_async_copy(pos_ref, pos_smem, idx_sem)
    idx_copy.start()
    idx_copy.wait()

    def issue(a, carry):
        t = i * bt + a // top_k
        dst = pos_smem[a // LANES, a % LANES]
        pltpu.make_async_copy(xq_hbm.at[pl.ds(t, 1)], xs_hbm.at[pl.ds(dst, 1)], sem).start()
        return carry

    lax.fori_loop(0, n_assign, issue, 0, unroll=8)
    pltpu.make_async_copy(xs_hbm.at[pl.ds(0, n_assign)], xs_hbm.at[pl.ds(0, n_assign)], sem).wait()


def _dispatch(cfg, fill, nchunk, n_valid, pos2d, xq):
    m, half = xq.shape
    n_tiles, slot_rows = _moe_slots(cfg)
    rows_per_step = MOE_BLOCK * cfg.top_k // LANES
    return pl.pallas_call(
        functools.partial(_dispatch_kernel, n_experts=cfg.n_experts, top_k=cfg.top_k, n_tiles=n_tiles),
        out_shape=jax.ShapeDtypeStruct((slot_rows, half), jnp.uint32),
        grid_spec=pltpu.PrefetchScalarGridSpec(
            num_scalar_prefetch=3,
            grid=(m // MOE_BLOCK,),
            in_specs=[pl.BlockSpec((rows_per_step, LANES), lambda i, *_: (i, 0)),
                      pl.BlockSpec(memory_space=pl.ANY)],
            out_specs=pl.BlockSpec(memory_space=pl.ANY),
            scratch_shapes=[pltpu.VMEM((MOE_TILE, half), jnp.uint32),
                            pltpu.SMEM((rows_per_step, LANES), jnp.int32),
                            pltpu.SemaphoreType.DMA(()), pltpu.SemaphoreType.DMA(())]),
        compiler_params=_params(1),
        name="moe_dispatch",
    )(fill, nchunk, n_valid, pos2d, xq)


def _expert_kernel(te_ref, nv_ref, xs_ref, perm_ref, wgu_ref, bgu_ref, wdn_ref, bdn_ref, ys_ref,
                   wgu_bf, wdn_bf, *, ff):
    j = pl.program_id(0)
    d = wgu_ref.shape[0]
    half = d // 2
    e = te_ref[j]
    e_prev = te_ref[jnp.maximum(j - 1, 0)]

    @pl.when((j == 0) | (e != e_prev))
    def _():
        rows = min(512, d)

        def cast_gu(c, carry):
            r0 = pl.multiple_of(c * rows, rows)
            w = wgu_ref[pl.ds(r0, rows), :].astype(BF16)
            wgu_bf[pl.ds(r0, rows), :] = _dot(w, perm_ref[...]).astype(BF16)
            return carry

        lax.fori_loop(0, d // rows, cast_gu, 0)
        rows_dn = 32

        def cast_dn(c, carry):
            r0 = pl.multiple_of(c * rows_dn, rows_dn)
            wdn_bf[pl.ds(r0, rows_dn), :] = wdn_ref[pl.ds(r0, rows_dn), :].astype(BF16)
            return carry

        lax.fori_loop(0, ff // rows_dn, cast_dn, 0)

    @pl.when(j < nv_ref[0])
    def _():
        lo, hi = _unpack_pair(xs_ref[...])
        h = (_dot(lo.astype(BF16), wgu_bf[pl.ds(0, half), :]) + _dot(hi.astype(BF16), wgu_bf[pl.ds(half, half), :])
             + bgu_ref[...])
        h_glu = jnp.minimum(h[:, :ff], SWIGLU_LIMIT)
        h_lin = jnp.clip(h[:, ff:], -SWIGLU_LIMIT, SWIGLU_LIMIT)
        act = (h_glu * _sigmoid(SWIGLU_ALPHA * h_glu) * (h_lin + 1.0)).astype(BF16)
        cw = min(512, half)
        for c in range(half // cw):
            lo_sl = slice(c * cw, (c + 1) * cw)
            hi_sl = slice(half + c * cw, half + (c + 1) * cw)
            y_lo = _dot(act, wdn_bf[:, lo_sl]) + bdn_ref[:, lo_sl]
            y_hi = _dot(act, wdn_bf[:, hi_sl]) + bdn_ref[:, hi_sl]
            ys_ref[:, lo_sl] = _pack_pair(y_lo, y_hi)

    @pl.when(j >= nv_ref[0])
    def _():
        ys_ref[...] = jnp.zeros_like(ys_ref)


def _experts(cfg, tile_e, n_valid, xs, perm, w_gu, b_gu_perm, w_dn, b_dn):
    d, ff, ne = cfg.d_model, cfg.expert_ff, cfg.n_experts
    half = d // 2
    n_tiles, slot_rows = _moe_slots(cfg)
    tile_blk = lambda j, te, nv: (jnp.minimum(j, nv[0] - 1), 0)
    by_expert = lambda j, te, nv: (te[j], 0, 0)
    return pl.pallas_call(
        functools.partial(_expert_kernel, ff=ff),
        out_shape=jax.ShapeDtypeStruct((slot_rows, half), jnp.uint32),
        grid_spec=pltpu.PrefetchScalarGridSpec(
            num_scalar_prefetch=2,
            grid=(n_tiles,),
            in_specs=[
                pl.BlockSpec((MOE_TILE, half), tile_blk),
                pl.BlockSpec((2 * ff, 2 * ff), lambda j, te, nv: (0, 0)),
                pl.BlockSpec((None, d, 2 * ff), by_expert),
                pl.BlockSpec((None, 1, 2 * ff), by_expert),
                pl.BlockSpec((None, ff, d), by_expert),
                pl.BlockSpec((None, 1, d), by_expert),
            ],
            out_specs=pl.BlockSpec((MOE_TILE, half), lambda j, te, nv: (j, 0)),
            scratch_shapes=[pltpu.VMEM((d, 2 * ff), BF16), pltpu.VMEM((ff, d), BF16)]),
        compiler_params=_params(1),
        name="moe_experts",
    )(tile_e, n_valid, xs, perm, w_gu, b_gu_perm.reshape(ne, 1, 2 * ff), w_dn, b_dn.reshape(ne, 1, d))


def _combine_kernel(pos_ref, route_ref, res_ref, mod_ref, ys_hbm, o_ref, buf_ref, pos_smem, sem, idx_sem,
                    *, gate, top_k):
    bt, d = res_ref.shape
    half = d // 2
    n_assign = bt * top_k

    idx_copy = pltpu.make_async_copy(pos_ref, pos_smem, idx_sem)
    idx_copy.start()
    idx_copy.wait()

    def issue(a, carry):
        src = pos_smem[a // LANES, a % LANES]
        pltpu.make_async_copy(ys_hbm.at[pl.ds(src, 1)], buf_ref.at[a % top_k, pl.ds(a // top_k, 1)], sem).start()
        return carry

    lax.fori_loop(0, n_assign, issue, 0, unroll=8)
    pltpu.make_async_copy(buf_ref, buf_ref, sem).wait()

    weights = [route_ref[:, top_k + k:top_k + k + 1] for k in range(top_k)]
    cw = min(512, half)
    for c in range(half // cw):
        lo_sl = slice(c * cw, (c + 1) * cw)
        hi_sl = slice(half + c * cw, half + (c + 1) * cw)
        acc_lo = acc_hi = None
        for k in range(top_k):
            lo, hi = _unpack_pair(buf_ref[k, :, lo_sl])
            acc_lo = weights[k] * lo if acc_lo is None else acc_lo + weights[k] * lo
            acc_hi = weights[k] * hi if acc_hi is None else acc_hi + weights[k] * hi
        o_ref[:, lo_sl] = res_ref[:, lo_sl] + mod_ref[0, gate:gate + 1, lo_sl] * acc_lo
        o_ref[:, hi_sl] = res_ref[:, hi_sl] + mod_ref[0, gate:gate + 1, hi_sl] * acc_hi


def _combine(cfg, pos2d, route, res, mod, ys, gate):
    m, d = res.shape
    half = d // 2
    ep = route.shape[1]
    bt = MOE_BLOCK
    rows_per_step = bt * cfg.top_k // LANES
    return pl.pallas_call(
        functools.partial(_combine_kernel, gate=gate, top_k=cfg.top_k),
        out_shape=jax.ShapeDtypeStruct((m, d), F32),
        grid=(m // bt,),
        in_specs=[
            pl.BlockSpec((rows_per_step, LANES), lambda i: (i, 0)),
            pl.BlockSpec((bt, ep), lambda i: (i, 0)),
            pl.BlockSpec((bt, d), lambda i: (i, 0)),
            pl.BlockSpec((1, N_MOD, d), lambda i: (_seg_of_block(cfg, i, bt), 0, 0)),
            pl.BlockSpec(memory_space=pl.ANY),
        ],
        out_specs=pl.BlockSpec((bt, d), lambda i: (i, 0)),
        scratch_shapes=[pltpu.VMEM((cfg.top_k, bt, half), jnp.uint32),
                        pltpu.SMEM((rows_per_step, LANES), jnp.int32),
                        pltpu.SemaphoreType.DMA(()), pltpu.SemaphoreType.DMA(())],
        compiler_params=_params(1),
        name="moe_combine",
    )(pos2d, route, res, mod, ys)


def _moe_plan(cfg, route, counts8):
    k, ne = cfg.top_k, cfg.n_experts
    n_tiles, _ = _moe_slots(cfg)
    expert = route[:, :k].astype(jnp.int32)
    rank = route[:, 2 * k:3 * k].astype(jnp.int32)
    counts = counts8[0, :ne].astype(jnp.int32)
    padded = (counts + MOE_TILE - 1) // MOE_TILE * MOE_TILE
    region_end = jnp.cumsum(padded)
    region_start = region_end - padded
    pos = region_start[expert] + rank
    fill = (region_start + counts) // SUBLANES * SUBLANES
    nchunk = (region_end - fill) // SUBLANES
    n_valid = region_end[-1] // MOE_TILE
    tile_e = jnp.searchsorted(region_end, jnp.arange(n_tiles, dtype=jnp.int32) * MOE_TILE, side="right")
    tile_e = jnp.minimum(tile_e, ne - 1).astype(jnp.int32)
    tile_e = tile_e[jnp.minimum(jnp.arange(n_tiles), n_valid - 1)]
    return (pos.reshape(-1, LANES), fill.astype(jnp.int32), nchunk.astype(jnp.int32), tile_e,
            n_valid.reshape(1).astype(jnp.int32))


def _deinterleave_perm(ff):
    col = jnp.arange(2 * ff)
    src = jnp.where(col < ff, 2 * col, 2 * (col - ff) + 1)
    return (jnp.arange(2 * ff)[:, None] == src[None, :]).astype(BF16)


def _forward(cfg, x, c, ctx, c_ctx, ada_a, ada_b, ada_bias, norm1_g, norm2_g, w_in, conv_a_w, lam_qk,
             subln_g, lru_conv_w, lru_conv_b, lru_w_a, lru_b_a, lru_w_x, lru_b_x, lru_lambda,
             w_branch, w_out, router_w, router_b, w_gu, b_gu, w_dn, b_dn, final_g):
    d, wd, bsz, s, lc = cfg.d_model, cfg.branch_w, cfg.batch, cfg.seq, cfg.ctx_len
    m_lat, m_all, ne = cfg.m_lat, cfg.m_all, cfg.n_experts
    assert bsz * lc <= s and s % lc == 0 and bsz + 1 <= SUBLANES
    bm = min(512, bsz * lc)
    bn = min(1024, d)
    bt = min(256, lc)
    ep = -(-ne // LANES) * LANES

    xs = jnp.concatenate([x.reshape(m_lat, d), ctx.reshape(bsz * lc, d)], axis=0)
    cond8 = jnp.zeros((SUBLANES, d), F32).at[:bsz].set(c).at[bsz].set(c_ctx)
    ctab, stab = _rope_tables(cfg, bt)
    ctx_blk = lambda b: m_lat // lc + b
    perm = _deinterleave_perm(cfg.expert_ff)

    for l in range(cfg.depth):
        lam_init = LAMBDA_INIT_BASE - LAMBDA_INIT_AMP * math.exp(-LAMBDA_INIT_RATE * l)
        mod = _ada(cfg, cond8, ada_a[l], ada_b[l], ada_bias[l])

        xn = _norm_mod(cfg, xs, norm1_g[l], mod, 0, 1)
        p = _matmul(xn, w_in[l].astype(BF16), bm, bn)
        oa = jnp.concatenate([
            _conv_mixer(cfg, p, conv_a_w[l], s, 0, bsz, m_lat),
            _conv_mixer(cfg, p, conv_a_w[l], lc, m_lat // lc, bsz, bsz * lc)], axis=0)
        qt, kr, vt = _att_prep(cfg, p, ctab, stab, bt)
        nh = wd // ATT_VD
        ctx_seg = (p, ctx_blk, 4 * nh, vt, lc)
        lat_seg = (kr, lambda b: b, 0, vt, s)
        ob_lat = _attention(cfg, lam_qk[l], subln_g[l], qt, [ctx_seg, lat_seg], 0, s // bt, m_lat, lam_init, bq=bt)
        ob_ctx = _attention(cfg, lam_qk[l], subln_g[l], qt, [ctx_seg], m_lat // bt, lc // bt, bsz * lc, lam_init, bq=bt)
        ob = jnp.concatenate([ob_lat, ob_ctx], axis=0)
        oc = jnp.concatenate(_lru(cfg, p, lru_conv_w[l], lru_conv_b[l], lru_w_a[l], lru_b_a[l],
                                  lru_w_x[l], lru_b_x[l], lru_lambda[l]), axis=0)
        y = _merge(cfg, oa, ob, oc, p, w_branch[l].astype(BF16), bm, bn)
        xs = _matmul_residual(cfg, y, w_out[l].astype(BF16), xs, mod, 2, bm, bn)

        wr_pad = jnp.zeros((d, ep), F32).at[:, :ne].set(router_w[l])
        br_pad = jnp.full((1, ep), -jnp.inf, F32).at[0, :ne].set(router_b[l])
        xq, route, counts8 = _norm_router(cfg, xs, norm2_g[l], mod, 3, 4, wr_pad, br_pad)
        pos2d, fill, nchunk, tile_e, n_valid = _moe_plan(cfg, route, counts8)
        slots = _dispatch(cfg, fill, nchunk, n_valid, pos2d, xq)
        b_gu_perm = jnp.concatenate([b_gu[l][:, 0::2], b_gu[l][:, 1::2]], axis=-1)
        ys = _experts(cfg, tile_e, n_valid, slots, perm, w_gu[l], b_gu_perm, w_dn[l], b_dn[l])
        xs = _combine(cfg, pos2d, route, xs, mod, ys, 5)

    return _norm_plain(xs, final_g, m_lat).reshape(bsz, s, d)


def kernel(x, c, ctx, c_ctx, ada_a, ada_b, ada_bias, norm1_g, norm2_g, w_in, conv_a_w, lam_qk, subln_g, lru_conv_w, lru_conv_b, lru_w_a, lru_b_a, lru_w_x, lru_b_x, lru_lambda, w_branch, w_out, router_w, router_b, w_gu, b_gu, w_dn, b_dn, final_g):
    return _forward(_Cfg(), x, c, ctx, c_ctx, ada_a, ada_b, ada_bias, norm1_g, norm2_g, w_in, conv_a_w, lam_qk,
                    subln_g, lru_conv_w, lru_conv_b, lru_w_a, lru_b_a, lru_w_x, lru_b_x, lru_lambda,
                    w_branch, w_out, router_w, router_b, w_gu, b_gu, w_dn, b_dn, final_g)
```

```python
import functools
import math
from typing import NamedTuple

import jax
import jax.numpy as jnp
from jax import lax
from jax.experimental import pallas as pl
from jax.experimental.pallas import tpu as pltpu

F32 = jnp.float32
BF16 = jnp.bfloat16

LANES = 128
SUBLANES = 8
EPS = 1e-6
ATT_HD = 64
ATT_VD = 2 * ATT_HD
ATT_UNROLL = 8
ROPE_THETA = 10000.0
ROPE_AXIS_DIM = ATT_HD // 2
ROPE_PAIRS = ROPE_AXIS_DIM // 2
LAMBDA_INIT_BASE = 0.8
LAMBDA_INIT_AMP = 0.6
LAMBDA_INIT_RATE = 0.3
LRU_C = 8.0
SWIGLU_ALPHA = 1.702
SWIGLU_LIMIT = 7.0
N_MOD = 6
VMEM_LIMIT = 56 * 1024 * 1024


class _Cfg(NamedTuple):
    d_model: int = 4096
    batch: int = 2
    seq: int = 4096
    depth: int = 4
    grid_w: int = 64
    ctx_len: int = 256
    branch_w: int = 1024
    n_experts: int = 32
    top_k: int = 4
    expert_ff: int = 256

    @property
    def heads(self):
        return self.branch_w // ATT_VD

    @property
    def m_lat(self):
        return self.batch * self.seq

    @property
    def m_all(self):
        return self.batch * (self.seq + self.ctx_len)

    @property
    def in_cols(self):
        return 8 * self.branch_w + 3 * self.d_model


def _params(n_grid_dims):
    return pltpu.CompilerParams(
        dimension_semantics=("arbitrary",) * n_grid_dims, vmem_limit_bytes=VMEM_LIMIT)


def _dot(a, b):
    return jnp.dot(a, b, preferred_element_type=F32)


def _split_bf16(x):
    hi = x.astype(BF16)
    lo = (x - hi.astype(F32)).astype(BF16)
    return hi, lo


def _dot3(x, w):
    xh, xl = _split_bf16(x)
    wh, wl = _split_bf16(w)
    return _dot(xh, wh) + (_dot(xl, wh) + _dot(xh, wl))


def _sigmoid(x):
    return 1.0 / (1.0 + jnp.exp(-x))


def _ada_kernel(cond_ref, a_ref, b_ref, bias_ref, o_ref):
    cond = cond_ref[...]
    t = _dot3(cond * _sigmoid(cond), a_ref[...])
    o_ref[...] = _dot3(t, b_ref[...]) + bias_ref[...]


def _ada(cfg, cond8, a, b, bias):
    d = cfg.d_model
    rank = a.shape[1]
    out = pl.pallas_call(
        _ada_kernel,
        out_shape=jax.ShapeDtypeStruct((SUBLANES, N_MOD * d), F32),
        grid=(N_MOD,),
        in_specs=[
            pl.BlockSpec((SUBLANES, d), lambda j: (0, 0)),
            pl.BlockSpec((d, rank), lambda j: (0, 0)),
            pl.BlockSpec((rank, d), lambda j: (0, j)),
            pl.BlockSpec((1, d), lambda j: (0, j)),
        ],
        out_specs=pl.BlockSpec((SUBLANES, d), lambda j: (0, j)),
        compiler_params=_params(1),
        name="ada_modulation",
    )(cond8, a, b, bias.reshape(1, N_MOD * d))
    return out.reshape(SUBLANES, N_MOD, d)[: cfg.batch + 1]


def _rms(x, g):
    return x * lax.rsqrt(jnp.mean(x * x, axis=-1, keepdims=True) + EPS) * g


def _norm_mod_kernel(x_ref, g_ref, mod_ref, o_ref, *, shift, scale):
    y = _rms(x_ref[...], g_ref[...])
    y = y * (1.0 + mod_ref[0, scale:scale + 1, :]) + mod_ref[0, shift:shift + 1, :]
    o_ref[...] = y.astype(o_ref.dtype)


def _norm_plain_kernel(x_ref, g_ref, o_ref):
    o_ref[...] = _rms(x_ref[...], g_ref[...]).astype(o_ref.dtype)


def _pack_pair(lo, hi):
    lo_bits = lax.bitcast_convert_type(lo.astype(BF16).astype(F32), jnp.uint32)
    hi_bits = lax.bitcast_convert_type(hi.astype(BF16).astype(F32), jnp.uint32)
    return (lo_bits >> 16) | (hi_bits & jnp.uint32(0xFFFF0000))


def _unpack_pair(word):
    lo = lax.bitcast_convert_type(word << 16, F32)
    hi = lax.bitcast_convert_type(word & jnp.uint32(0xFFFF0000), F32)
    return lo, hi


def _norm_router_kernel(x_ref, g_ref, mod_ref, wr_ref, br_ref, xq_ref, route_ref, counts_ref, carry_ref,
                        *, shift, scale, top_k):
    @pl.when(pl.program_id(0) == 0)
    def _():
        carry_ref[...] = jnp.zeros_like(carry_ref)

    y = _rms(x_ref[...], g_ref[...])
    y = y * (1.0 + mod_ref[0, scale:scale + 1, :]) + mod_ref[0, shift:shift + 1, :]
    half = y.shape[1] // 2
    xq_ref[...] = _pack_pair(y[:, :half], y[:, half:])
    work = _dot3(y, wr_ref[...]) + br_ref[...]
    bm, n_lane = work.shape
    lane = lax.broadcasted_iota(jnp.int32, work.shape, 1).astype(F32)
    vals, hots, ids = [], [], []
    for _ in range(top_k):
        m = jnp.max(work, axis=-1, keepdims=True)
        idx = jnp.min(jnp.where(work == m, lane, float(n_lane)), axis=-1, keepdims=True)
        hot = lane == idx
        vals.append(m)
        hots.append(hot)
        ids.append(idx)
        work = jnp.where(hot, -jnp.inf, work)
    exps = [jnp.exp(v - vals[0]) for v in vals]
    denom = exps[0]
    for e in exps[1:]:
        denom = denom + e
    picked = jnp.zeros(work.shape, F32)
    for hot in hots:
        picked = jnp.where(hot, 1.0, picked)
    tri = (lax.broadcasted_iota(jnp.int32, (bm, bm), 0) > lax.broadcasted_iota(jnp.int32, (bm, bm), 1))
    before = _dot(jnp.where(tri, 1.0, 0.0).astype(BF16), picked.astype(BF16)) + carry_ref[0:1, :]
    carry_ref[...] = carry_ref[...] + jnp.sum(picked, axis=0, keepdims=True)
    counts_ref[...] = carry_ref[...]
    route = jnp.zeros(work.shape, F32)
    for k in range(top_k):
        rank = jnp.sum(jnp.where(hots[k], before, 0.0), axis=-1, keepdims=True)
        route = jnp.where(lane == float(k), ids[k], route)
        route = jnp.where(lane == float(top_k + k), exps[k] / denom, route)
        route = jnp.where(lane == float(2 * top_k + k), rank, route)
    route_ref[...] = route


def _seg_of_block(cfg, i, bm):
    return (i * bm) // cfg.seq


def _norm_mod(cfg, x, g, mod, shift, scale, bm=256):
    m, d = x.shape
    return pl.pallas_call(
        functools.partial(_norm_mod_kernel, shift=shift, scale=scale),
        out_shape=jax.ShapeDtypeStruct((m, d), BF16),
        grid=(m // bm,),
        in_specs=[
            pl.BlockSpec((bm, d), lambda i: (i, 0)),
            pl.BlockSpec((1, d), lambda i: (0, 0)),
            pl.BlockSpec((1, N_MOD, d), lambda i: (_seg_of_block(cfg, i, bm), 0, 0)),
        ],
        out_specs=pl.BlockSpec((bm, d), lambda i: (i, 0)),
        compiler_params=_params(1),
        name="norm_mod",
    )(x, g.reshape(1, d), mod)


def _norm_plain(x, g, m_rows, bm=256):
    d = x.shape[1]
    return pl.pallas_call(
        _norm_plain_kernel,
        out_shape=jax.ShapeDtypeStruct((m_rows, d), F32),
        grid=(m_rows // bm,),
        in_specs=[
            pl.BlockSpec((bm, d), lambda i: (i, 0)),
            pl.BlockSpec((1, d), lambda i: (0, 0)),
        ],
        out_specs=pl.BlockSpec((bm, d), lambda i: (i, 0)),
        compiler_params=_params(1),
        name="final_norm",
    )(x, g.reshape(1, d))


def _norm_router(cfg, x, g, mod, shift, scale, wr_pad, br_pad, bm=256):
    m, d = x.shape
    ep = wr_pad.shape[1]
    return pl.pallas_call(
        functools.partial(_norm_router_kernel, shift=shift, scale=scale, top_k=cfg.top_k),
        out_shape=(jax.ShapeDtypeStruct((m, d // 2), jnp.uint32), jax.ShapeDtypeStruct((m, ep), F32),
                   jax.ShapeDtypeStruct((SUBLANES, ep), F32)),
        grid=(m // bm,),
        in_specs=[
            pl.BlockSpec((bm, d), lambda i: (i, 0)),
            pl.BlockSpec((1, d), lambda i: (0, 0)),
            pl.BlockSpec((1, N_MOD, d), lambda i: (_seg_of_block(cfg, i, bm), 0, 0)),
            pl.BlockSpec((d, ep), lambda i: (0, 0)),
            pl.BlockSpec((1, ep), lambda i: (0, 0)),
        ],
        out_specs=(pl.BlockSpec((bm, d // 2), lambda i: (i, 0)), pl.BlockSpec((bm, ep), lambda i: (i, 0)),
                   pl.BlockSpec((SUBLANES, ep), lambda i: (0, 0))),
        scratch_shapes=[pltpu.VMEM((SUBLANES, ep), F32)],
        compiler_params=_params(1),
        name="norm_router",
    )(x, g.reshape(1, d), mod, wr_pad, br_pad)


CAST_ROWS = 256


def _cast_weight_block(w_ref, wbf_ref):
    @pl.when(pl.program_id(1) == 0)
    def _():
        rows = min(CAST_ROWS, w_ref.shape[0])

        def body(c, carry):
            r0 = pl.multiple_of(c * rows, rows)
            wbf_ref[pl.ds(r0, rows), :] = w_ref[pl.ds(r0, rows), :].astype(BF16)
            return carry

        lax.fori_loop(0, w_ref.shape[0] // rows, body, 0)


def _mm_kernel(x_ref, w_ref, o_ref, wbf_ref):
    _cast_weight_block(w_ref, wbf_ref)
    o_ref[...] = _dot(x_ref[...], wbf_ref[...]).astype(o_ref.dtype)


def _mm_res_kernel(x_ref, w_ref, res_ref, mod_ref, o_ref, wbf_ref, *, gate):
    _cast_weight_block(w_ref, wbf_ref)
    o_ref[...] = res_ref[...] + mod_ref[0, gate:gate + 1, :] * _dot(x_ref[...], wbf_ref[...])


def _matmul(x, w, layer, bm, bn, out_dtype=BF16):
    m, k = x.shape
    n = w.shape[2]
    return pl.pallas_call(
        _mm_kernel,
        out_shape=jax.ShapeDtypeStruct((m, n), out_dtype),
        grid=(n // bn, m // bm),
        in_specs=[
            pl.BlockSpec((bm, k), lambda j, i: (i, 0)),
            pl.BlockSpec((None, k, bn), lambda j, i: (layer, 0, j)),
        ],
        out_specs=pl.BlockSpec((bm, bn), lambda j, i: (i, j)),
        scratch_shapes=[pltpu.VMEM((k, bn), BF16)],
        compiler_params=_params(2),
        name="matmul",
    )(x, w)


def _matmul_residual(cfg, x, w, layer, res, mod, gate, bm, bn):
    m, k = x.shape
    n = w.shape[2]
    return pl.pallas_call(
        functools.partial(_mm_res_kernel, gate=gate),
        out_shape=jax.ShapeDtypeStruct((m, n), F32),
        grid=(n // bn, m // bm),
        in_specs=[
            pl.BlockSpec((bm, k), lambda j, i: (i, 0)),
            pl.BlockSpec((None, k, bn), lambda j, i: (layer, 0, j)),
            pl.BlockSpec((bm, bn), lambda j, i: (i, j)),
            pl.BlockSpec((1, N_MOD, bn), lambda j, i: (_seg_of_block(cfg, i, bm), 0, j)),
        ],
        out_specs=pl.BlockSpec((bm, bn), lambda j, i: (i, j)),
        scratch_shapes=[pltpu.VMEM((k, bn), BF16)],
        compiler_params=_params(2),
        name="matmul_residual",
    )(x, w, res, mod)


def _conv_a_kernel(b_ref, c_ref, h_ref, w_ref, o_ref, *, tt):
    t_len = b_ref.shape[0]
    w = w_ref[...]
    halo = 16

    def prod(r0, n):
        return c_ref[pl.ds(r0, n), :].astype(F32) * h_ref[pl.ds(r0, n), :].astype(F32)

    def chunk(ci, carry):
        r0 = pl.multiple_of(ci * tt, tt)
        u = prod(r0, tt)
        prev = prod(pl.multiple_of(jnp.maximum(r0 - halo, 0), halo), halo)[halo - 1:halo, :]
        prev = jnp.where(r0 == 0, 0.0, prev)
        nxt = prod(pl.multiple_of(jnp.minimum(r0 + tt, t_len - halo), halo), halo)[0:1, :]
        nxt = jnp.where(r0 + tt == t_len, 0.0, nxt)
        row = lax.broadcasted_iota(jnp.int32, u.shape, 0)
        up = jnp.where(row == 0, prev, pltpu.roll(u, 1, 0))
        dn = jnp.where(row == tt - 1, nxt, pltpu.roll(u, tt - 1, 0))
        conv = w[0:1, :] * up + w[1:2, :] * u + w[2:3, :] * dn
        o_ref[pl.ds(r0, tt), :] = (b_ref[pl.ds(r0, tt), :].astype(F32) * conv).astype(o_ref.dtype)
        return carry

    lax.fori_loop(0, t_len // tt, chunk, 0)


def _conv_mixer(cfg, p, w, t_len, row_blk0, n_seq, out_rows, bc=256):
    wd = cfg.branch_w
    ncb = wd // bc
    tt = min(256, t_len)
    spec = lambda off: pl.BlockSpec((t_len, bc), lambda s, j: (row_blk0 + s, off * ncb + j))
    return pl.pallas_call(
        functools.partial(_conv_a_kernel, tt=tt),
        out_shape=jax.ShapeDtypeStruct((out_rows, wd), BF16),
        grid=(n_seq, ncb),
        in_specs=[spec(0), spec(1), spec(2), pl.BlockSpec((3, bc), lambda s, j: (0, j))],
        out_specs=pl.BlockSpec((t_len, bc), lambda s, j: (s, j)),
        compiler_params=_params(2),
        name="conv_mixer",
    )(p, p, p, w)


def _rope_tables(cfg, bt):
    t = jnp.arange(cfg.seq)
    pos = jnp.stack([t // cfg.grid_w, t % cfg.grid_w], axis=-1).astype(F32)
    inv = ROPE_THETA ** (-jnp.arange(ROPE_PAIRS, dtype=F32) * 2.0 / ROPE_AXIS_DIM)
    ang = pos[:, :, None] * inv
    cos, sin = jnp.cos(ang), jnp.sin(ang)
    cmap = jnp.concatenate([cos[:, 0], cos[:, 0], cos[:, 1], cos[:, 1]], axis=-1)
    smap = jnp.concatenate([-sin[:, 0], sin[:, 0], -sin[:, 1], sin[:, 1]], axis=-1)
    ctab = jnp.concatenate([jnp.tile(cmap, (1, 2)), jnp.ones((bt, ATT_VD), F32)], axis=0)
    stab = jnp.concatenate([jnp.tile(smap, (1, 2)), jnp.zeros((bt, ATT_VD), F32)], axis=0)
    return ctab, stab


def _att_prep_kernel(q_ref, k_ref, v_ref, c_ref, s_ref, qt_ref, kr_ref, vt_ref, *, heads):
    cos = c_ref[...]
    sin = s_ref[...]
    lane = lax.broadcasted_iota(jnp.int32, cos.shape, 1)
    low = (lane % ROPE_AXIS_DIM) < ROPE_PAIRS

    def rope(u):
        swapped = jnp.where(low, pltpu.roll(u, ATT_VD - ROPE_PAIRS, 1), pltpu.roll(u, ROPE_PAIRS, 1))
        return u * cos + swapped * sin

    for h in range(heads):
        sl = slice(h * ATT_VD, (h + 1) * ATT_VD)
        qt_ref[sl, :] = rope(q_ref[:, sl].astype(F32)).T.astype(qt_ref.dtype)
        kr_ref[:, sl] = rope(k_ref[:, sl].astype(F32)).astype(kr_ref.dtype)
        vt_ref[sl, :] = v_ref[:, sl].astype(F32).T.astype(vt_ref.dtype)


def _att_prep(cfg, p, ctab, stab, bt=256):
    m, wd = cfg.m_all, cfg.branch_w
    n_lat_blk = cfg.m_lat // bt
    n_seq_blk = cfg.seq // bt
    tab_map = lambda i: (jnp.where(i < n_lat_blk, i % n_seq_blk, n_seq_blk), 0)
    return pl.pallas_call(
        functools.partial(_att_prep_kernel, heads=cfg.heads),
        out_shape=(jax.ShapeDtypeStruct((wd, m), BF16), jax.ShapeDtypeStruct((m, wd), BF16),
                   jax.ShapeDtypeStruct((wd, m), BF16)),
        grid=(m // bt,),
        in_specs=[
            pl.BlockSpec((bt, wd), lambda i: (i, 3)),
            pl.BlockSpec((bt, wd), lambda i: (i, 4)),
            pl.BlockSpec((bt, wd), lambda i: (i, 5)),
            pl.BlockSpec((bt, ATT_VD), tab_map),
            pl.BlockSpec((bt, ATT_VD), tab_map),
        ],
        out_specs=(pl.BlockSpec((wd, bt), lambda i: (0, i)), pl.BlockSpec((bt, wd), lambda i: (i, 0)),
                   pl.BlockSpec((wd, bt), lambda i: (0, i))),
        compiler_params=_params(1),
        name="att_prep",
    )(p, p, p, ctab, stab)


def _attn_kernel(lam_ref, g_ref, qt_ref, *refs, seg_lens, tk, lam_init):
    n_seg = len(seg_lens)
    kv = refs[:2 * n_seg]
    o_ref = refs[2 * n_seg]
    s1_ref, s2_ref, acc1_ref, acc2_ref = refs[2 * n_seg + 1:]
    bq = qt_ref.shape[1]

    qt = qt_ref[...].astype(F32) * ATT_HD ** -0.5
    sub = lax.broadcasted_iota(jnp.int32, qt.shape, 0)
    q1 = jnp.where(sub < ATT_HD, qt, 0.0).astype(qt_ref.dtype)
    q2 = jnp.where(sub >= ATT_HD, qt, 0.0).astype(qt_ref.dtype)

    def chunks(seg):
        tkk = min(tk, seg_lens[seg])
        return tkk, seg_lens[seg] // tkk

    m1 = jnp.full((1, bq), -jnp.inf, F32)
    m2 = jnp.full((1, bq), -jnp.inf, F32)
    off = 0
    for seg in range(n_seg):
        k_ref = kv[2 * seg]
        tkk, n_chunk = chunks(seg)

        def body_a(c, carry, k_ref=k_ref, tkk=tkk, off=off):
            m1, m2 = carry
            r0 = pl.multiple_of(c * tkk, tkk)
            kc = k_ref[pl.ds(r0, tkk), :]
            s1 = _dot(kc, q1)
            s2 = _dot(kc, q2)
            s1_ref[pl.ds(off + r0, tkk), :] = s1
            s2_ref[pl.ds(off + r0, tkk), :] = s2
            return (jnp.maximum(m1, jnp.max(s1, axis=0, keepdims=True)),
                    jnp.maximum(m2, jnp.max(s2, axis=0, keepdims=True)))

        m1, m2 = lax.fori_loop(0, n_chunk, body_a, (m1, m2), unroll=min(ATT_UNROLL, n_chunk))
        off += seg_lens[seg]

    acc1_ref[...] = jnp.zeros_like(acc1_ref)
    acc2_ref[...] = jnp.zeros_like(acc2_ref)
    l1 = jnp.zeros((1, bq), F32)
    l2 = jnp.zeros((1, bq), F32)
    off = 0
    for seg in range(n_seg):
        vt_ref = kv[2 * seg + 1]
        tkk, n_chunk = chunks(seg)

        def body_b(c, carry, vt_ref=vt_ref, tkk=tkk, off=off):
            l1, l2 = carry
            r0 = pl.multiple_of(c * tkk, tkk)
            p1 = jnp.exp(s1_ref[pl.ds(off + r0, tkk), :] - m1)
            p2 = jnp.exp(s2_ref[pl.ds(off + r0, tkk), :] - m2)
            vt = vt_ref[:, pl.ds(r0, tkk)]
            acc1_ref[...] += _dot(vt, p1.astype(vt.dtype))
            acc2_ref[...] += _dot(vt, p2.astype(vt.dtype))
            return (l1 + jnp.sum(p1, axis=0, keepdims=True), l2 + jnp.sum(p2, axis=0, keepdims=True))

        l1, l2 = lax.fori_loop(0, n_chunk, body_b, (l1, l2), unroll=min(ATT_UNROLL, n_chunk))
        off += seg_lens[seg]

    lq = lam_ref[...]
    lam = (jnp.exp(jnp.sum(lq[0:1] * lq[1:2], axis=-1, keepdims=True))
           - jnp.exp(jnp.sum(lq[2:3] * lq[3:4], axis=-1, keepdims=True)) + lam_init)
    ot = acc1_ref[...] * (1.0 / l1) - acc2_ref[...] * (lam / l2)
    ms = jnp.mean(ot * ot, axis=0, keepdims=True)
    y = ot * lax.rsqrt(ms + EPS) * g_ref[...] * (1.0 - lam_init)
    o_ref[...] = y.T.astype(o_ref.dtype)


def _attention(cfg, lam_qk, subln_g, qt, segs, q_col_blk0, n_q_per_batch, out_rows, lam_init, bq=256, tk=512):
    wd = cfg.branch_w
    in_specs = [
        pl.BlockSpec((4, ATT_HD), lambda b, h, i: (0, 0)),
        pl.BlockSpec((ATT_VD, 1), lambda b, h, i: (0, 0)),
        pl.BlockSpec((ATT_VD, bq), lambda b, h, i: (h, q_col_blk0 + b * n_q_per_batch + i)),
    ]
    args = [lam_qk, subln_g.reshape(ATT_VD, 1), qt]
    seg_lens = []
    for k_arr, k_blk, k_col0, vt_arr, length in segs:
        in_specs.append(pl.BlockSpec((length, ATT_VD), lambda b, h, i, f=k_blk, c0=k_col0: (f(b), c0 + h)))
        in_specs.append(pl.BlockSpec((ATT_VD, length), lambda b, h, i, f=k_blk: (h, f(b))))
        args += [k_arr, vt_arr]
        seg_lens.append(length)
    l_tot = sum(seg_lens)
    return pl.pallas_call(
        functools.partial(_attn_kernel, seg_lens=tuple(seg_lens), tk=tk, lam_init=lam_init),
        out_shape=jax.ShapeDtypeStruct((out_rows, wd), BF16),
        grid=(cfg.batch, cfg.heads, n_q_per_batch),
        in_specs=in_specs,
        out_specs=pl.BlockSpec((bq, ATT_VD), lambda b, h, i: (b * n_q_per_batch + i, h)),
        scratch_shapes=[pltpu.VMEM((l_tot, bq), F32), pltpu.VMEM((l_tot, bq), F32),
                        pltpu.VMEM((ATT_VD, bq), F32), pltpu.VMEM((ATT_VD, bq), F32)],
        compiler_params=_params(3),
        name="diff_attention",
    )(*args)


SCAN_PAD = SUBLANES


def _scan_levels(n):
    plan = []
    while n > 1:
        fan = SUBLANES if n % SUBLANES == 0 else n
        assert fan <= SUBLANES, "sequence length must factor into groups of at most 8"
        plan.append((n, fan, n // fan))
        n //= fan
    return plan


def _scan_scratch_rows(n):
    rows = 0
    for _, _, parts in _scan_levels(n):
        rows += 2 * SCAN_PAD + -(-parts // SUBLANES) * SUBLANES
    return max(rows, SUBLANES)


def _scan_inplace(a_ref, b_ref, base, n, rev, sa_ref, sb_ref, off=0):
    if n == 1:
        return
    fan = SUBLANES if n % SUBLANES == 0 else n
    parts = n // fan
    order = list(range(fan))
    if rev:
        order.reverse()

    def rows(r):
        return pl.ds(base + r, parts, stride=fan) if parts > 1 else pl.ds(base + r, 1)

    acc_a = acc_b = None
    for r in order:
        a_r = a_ref[rows(r), :]
        b_r = b_ref[rows(r), :]
        if acc_a is None:
            acc_a, acc_b = a_r, b_r
        else:
            acc_b = a_r * acc_b + b_r
            acc_a = a_r * acc_a
            a_ref[rows(r), :] = acc_a
            b_ref[rows(r), :] = acc_b
    if parts == 1:
        return
    data = off + SCAN_PAD
    ones = jnp.ones((SCAN_PAD, LANES), F32)
    zeros = jnp.zeros((SCAN_PAD, LANES), F32)
    sa_ref[pl.ds(off, SCAN_PAD), :] = ones
    sb_ref[pl.ds(off, SCAN_PAD), :] = zeros
    sa_ref[pl.ds(data + parts, SCAN_PAD), :] = ones
    sb_ref[pl.ds(data + parts, SCAN_PAD), :] = zeros
    sa_ref[pl.ds(data, parts), :] = acc_a
    sb_ref[pl.ds(data, parts), :] = acc_b
    nxt = off + 2 * SCAN_PAD + -(-parts // SUBLANES) * SUBLANES
    _scan_inplace(sa_ref, sb_ref, data, parts, rev, sa_ref, sb_ref, nxt)
    shift = data + 1 if rev else data - 1
    carry_a = sa_ref[pl.ds(shift, parts), :]
    carry_b = sb_ref[pl.ds(shift, parts), :]
    for r in order:
        a_r = a_ref[rows(r), :]
        b_ref[rows(r), :] = b_ref[rows(r), :] + a_r * carry_b
        a_ref[rows(r), :] = a_r * carry_a


def _softplus(x):
    return jnp.maximum(x, 0.0) + jnp.log1p(jnp.exp(-jnp.abs(x)))


def _gelu_tanh(x):
    return 0.5 * x * (1.0 + jnp.tanh(math.sqrt(2.0 / math.pi) * (x + 0.044715 * (x * x * x))))


def _lru_kernel(lxl_ref, lyl_ref, lxc_ref, lyc_ref, cw_ref, cb_ref, wa_ref, ba_ref, wx_ref, bx_ref, lam_ref,
                ol_ref, oc_ref, pad_ref, a0_ref, b0_ref, a1_ref, b1_ref, sa_ref, sb_ref, *, tt):
    l_ctx = lxc_ref.shape[0]
    l_lat = lxl_ref.shape[0]
    cw = cw_ref[...]
    cb = cb_ref[...]
    ab_refs = ((a0_ref, b0_ref), (a1_ref, b1_ref))
    neg_sp = [-LRU_C * _softplus(-lam_ref[d:d + 1, :]) for d in range(2)]
    wa = [wa_ref[d].astype(BF16) for d in range(2)]
    wx = [wx_ref[d].astype(BF16) for d in range(2)]

    for lx_ref, base, length, is_ctx in ((lxc_ref, 0, l_ctx, True), (lxl_ref, l_ctx, l_lat, False)):
        ttk = min(tt, length)
        pad_ref[pl.ds(0, SUBLANES), :] = jnp.zeros((SUBLANES, LANES), F32)
        pad_ref[pl.ds(SUBLANES + length, SUBLANES), :] = jnp.zeros((SUBLANES, LANES), F32)

        def fill(ci, carry, lx_ref=lx_ref, ttk=ttk):
            r0 = pl.multiple_of(ci * ttk, ttk)
            pad_ref[pl.ds(SUBLANES + r0, ttk), :] = lx_ref[pl.ds(r0, ttk), :].astype(F32)
            return carry

        lax.fori_loop(0, length // ttk, fill, 0)

        def gates(ci, carry, base=base, length=length, is_ctx=is_ctx, ttk=ttk):
            r0 = pl.multiple_of(ci * ttk, ttk)
            xx = pad_ref[pl.ds(r0, ttk + 2 * SUBLANES), :]
            u = (cw[0:1, :] * xx[6:6 + ttk] + cw[1:2, :] * xx[7:7 + ttk]
                 + cw[2:3, :] * xx[8:8 + ttk] + cw[3:4, :] * xx[9:9 + ttk]) + cb
            ub = u.astype(BF16)
            row = r0 + lax.broadcasted_iota(jnp.int32, u.shape, 0)
            for d in range(2):
                r_gate = _sigmoid(_dot(ub, wa[d]) + ba_ref[d:d + 1, :])
                i_gate = _sigmoid(_dot(ub, wx[d]) + bx_ref[d:d + 1, :])
                log_a = neg_sp[d] * r_gate
                a = jnp.exp(log_a)
                mult = jnp.sqrt(-jnp.tanh(log_a) * (a * a + 1.0))
                if is_ctx:
                    mult = jnp.where(row == (length - 1 if d == 1 else 0), 1.0, mult)
                a_ref, b_ref = ab_refs[d]
                a_ref[pl.ds(base + r0, ttk), :] = a
                b_ref[pl.ds(base + r0, ttk), :] = mult * (i_gate * u)
            return carry

        lax.fori_loop(0, length // ttk, gates, 0)

    for d, rev in ((0, False), (1, True)):
        a_ref, b_ref = ab_refs[d]
        _scan_inplace(a_ref, b_ref, 0, l_ctx, rev, sa_ref, sb_ref)
        _scan_inplace(a_ref, b_ref, l_ctx, l_lat, rev, sa_ref, sb_ref)

    h0 = (b0_ref[pl.ds(l_ctx - 1, 1), :], b1_ref[pl.ds(0, 1), :])

    for ly_ref, o_ref, base, length, is_ctx in ((lyc_ref, oc_ref, 0, l_ctx, True), (lyl_ref, ol_ref, l_ctx, l_lat, False)):
        ttk = min(tt, length)

        def emit(ci, carry, ly_ref=ly_ref, o_ref=o_ref, base=base, is_ctx=is_ctx, ttk=ttk):
            r0 = pl.multiple_of(ci * ttk, ttk)
            h = b0_ref[pl.ds(base + r0, ttk), :] + b1_ref[pl.ds(base + r0, ttk), :]
            if not is_ctx:
                h = h + a0_ref[pl.ds(base + r0, ttk), :] * h0[0] + a1_ref[pl.ds(base + r0, ttk), :] * h0[1]
            o_ref[pl.ds(r0, ttk), :] = (h * _gelu_tanh(ly_ref[pl.ds(r0, ttk), :].astype(F32))).astype(o_ref.dtype)
            return carry

        lax.fori_loop(0, length // ttk, emit, 0)


def _lru(cfg, p, conv_w, conv_b, w_a, b_a, w_x, b_x, lam_p, tt=256):
    wd, s, lc = cfg.branch_w, cfg.seq, cfg.ctx_len
    nb = wd // LANES
    l_tot = s + lc
    ctx_blk0 = cfg.m_lat // lc
    lx_col, ly_col = 6 * nb, 7 * nb
    vec2 = pl.BlockSpec((2, LANES), lambda b, n: (0, n))
    wspec = pl.BlockSpec((2, None, LANES, LANES), lambda b, n: (0, n, 0, 0))
    scan_rows = max(_scan_scratch_rows(s), _scan_scratch_rows(lc))
    return pl.pallas_call(
        functools.partial(_lru_kernel, tt=tt),
        out_shape=(jax.ShapeDtypeStruct((cfg.m_lat, wd), BF16), jax.ShapeDtypeStruct((cfg.batch * lc, wd), BF16)),
        grid=(cfg.batch, nb),
        in_specs=[
            pl.BlockSpec((s, LANES), lambda b, n: (b, lx_col + n)),
            pl.BlockSpec((s, LANES), lambda b, n: (b, ly_col + n)),
            pl.BlockSpec((lc, LANES), lambda b, n: (ctx_blk0 + b, lx_col + n)),
            pl.BlockSpec((lc, LANES), lambda b, n: (ctx_blk0 + b, ly_col + n)),
            pl.BlockSpec((4, LANES), lambda b, n: (0, n)),
            pl.BlockSpec((1, LANES), lambda b, n: (0, n)),
            wspec, vec2, wspec, vec2, vec2,
        ],
        out_specs=(pl.BlockSpec((s, LANES), lambda b, n: (b, n)), pl.BlockSpec((lc, LANES), lambda b, n: (b, n))),
        scratch_shapes=[pltpu.VMEM((s + 2 * SUBLANES, LANES), F32)]
        + [pltpu.VMEM((l_tot, LANES), F32)] * 4
        + [pltpu.VMEM((scan_rows, LANES), F32)] * 2,
        compiler_params=_params(2),
        name="rglru",
    )(p, p, p, p, conv_w, conv_b.reshape(1, wd), w_a, b_a, w_x, b_x, lam_p)


def _merge_kernel(oa_ref, ob_ref, oc_ref, ga_ref, gb_ref, gc_ref, w_ref, o_ref, wbf_ref, *, wd):
    _cast_weight_block(w_ref, wbf_ref)
    acc = None
    for i, (x_ref, g_ref) in enumerate(((oa_ref, ga_ref), (ob_ref, gb_ref), (oc_ref, gc_ref))):
        g = _sigmoid(g_ref[...].astype(F32))
        term = g * _dot(x_ref[...], wbf_ref[i * wd:(i + 1) * wd, :])
        acc = term if acc is None else acc + term
    o_ref[...] = acc.astype(o_ref.dtype)


def _merge(cfg, oa, ob, oc, p, wb, layer, bm, bn):
    m, wd, d = cfg.m_all, cfg.branch_w, cfg.d_model
    g_blk0 = 8 * wd // bn
    nd = d // bn
    xspec = pl.BlockSpec((bm, wd), lambda j, i: (i, 0))
    gspec = lambda br: pl.BlockSpec((bm, bn), lambda j, i: (i, g_blk0 + br * nd + j))
    return pl.pallas_call(
        functools.partial(_merge_kernel, wd=wd),
        out_shape=jax.ShapeDtypeStruct((m, d), BF16),
        grid=(nd, m // bm),
        in_specs=[xspec, xspec, xspec, gspec(0), gspec(1), gspec(2),
                  pl.BlockSpec((None, 3 * wd, bn), lambda j, i: (layer, 0, j))],
        out_specs=pl.BlockSpec((bm, bn), lambda j, i: (i, j)),
        scratch_shapes=[pltpu.VMEM((3 * wd, bn), BF16)],
        compiler_params=_params(2),
        name="merge_branches",
    )(oa, ob, oc, p, p, p, wb)


MOE_TILE = 256
MOE_BLOCK = 256


def _moe_slots(cfg):
    n_tiles = cfg.m_all * cfg.top_k // MOE_TILE + cfg.n_experts
    return n_tiles, n_tiles * MOE_TILE


def _assignment_index(t, top_k):
    per_row = LANES // top_k
    assert per_row * top_k == LANES and per_row & (per_row - 1) == 0
    shift = per_row.bit_length() - 1
    return lax.shift_right_logical(t, shift), (t & (per_row - 1)) * top_k


def _dispatch_kernel(fill_ref, nchunk_ref, nv_ref, pos_ref, xq_ref, xs_hbm, zero_ref, pos_smem, sem, idx_sem,
                     *, n_experts, top_k, n_tiles):
    i = pl.program_id(0)
    n_assign = pos_ref.shape[0] * pos_ref.shape[1]
    bt = n_assign // top_k

    @pl.when(i == 0)
    def _():
        zero_ref[...] = jnp.zeros_like(zero_ref)
        piece = zero_ref.at[pl.ds(0, SUBLANES)]
        total = 0
        for e in range(n_experts):
            def fill_piece(c, carry, e=e):
                start = pl.multiple_of(fill_ref[e] + c * SUBLANES, SUBLANES)
                pltpu.make_async_copy(piece, xs_hbm.at[pl.ds(start, SUBLANES)], sem).start()
                return carry

            lax.fori_loop(0, nchunk_ref[e], fill_piece, 0)
            total = total + nchunk_ref[e]

        def idle_tile(j, carry):
            start = pl.multiple_of(j * MOE_TILE, MOE_TILE)
            pltpu.make_async_copy(zero_ref, xs_hbm.at[pl.ds(start, MOE_TILE)], sem).start()
            return carry

        lax.fori_loop(nv_ref[0], n_tiles, idle_tile, 0)

        def drain_piece(c, carry):
            pltpu.make_async_copy(piece, xs_hbm.at[pl.ds(0, SUBLANES)], sem).wait()
            return carry

        lax.fori_loop(0, total, drain_piece, 0)

        def drain_tile(j, carry):
            pltpu.make_async_copy(zero_ref, xs_hbm.at[pl.ds(0, MOE_TILE)], sem).wait()
            return carry

        lax.fori_loop(nv_ref[0], n_tiles, drain_tile, 0)

    idx_copy = pltpu.make_async_copy(pos_ref, pos_smem, idx_sem)
    idx_copy.start()
    idx_copy.wait()

    def issue(t, carry):
        row, col = _assignment_index(t, top_k)
        for k in range(top_k):
            dst = pos_smem[row, col + k]
            pltpu.make_async_copy(xq_ref.at[pl.ds(t, 1)], xs_hbm.at[pl.ds(dst, 1)], sem).start()
        return carry

    lax.fori_loop(0, bt, issue, 0, unroll=2)
    pltpu.make_async_copy(xs_hbm.at[pl.ds(0, n_assign)], xs_hbm.at[pl.ds(0, n_assign)], sem).wait()


def _dispatch(cfg, fill, nchunk, n_valid, pos2d, xq):
    m, half = xq.shape
    n_tiles, slot_rows = _moe_slots(cfg)
    rows_per_step = MOE_BLOCK * cfg.top_k // LANES
    return pl.pallas_call(
        functools.partial(_dispatch_kernel, n_experts=cfg.n_experts, top_k=cfg.top_k, n_tiles=n_tiles),
        out_shape=jax.ShapeDtypeStruct((slot_rows, half), jnp.uint32),
        grid_spec=pltpu.PrefetchScalarGridSpec(
            num_scalar_prefetch=3,
            grid=(m // MOE_BLOCK,),
            in_specs=[pl.BlockSpec((rows_per_step, LANES), lambda i, *_: (i, 0)),
                      pl.BlockSpec((MOE_BLOCK, half), lambda i, *_: (i, 0))],
            out_specs=pl.BlockSpec(memory_space=pl.ANY),
            scratch_shapes=[pltpu.VMEM((MOE_TILE, half), jnp.uint32),
                            pltpu.SMEM((rows_per_step, LANES), jnp.int32),
                            pltpu.SemaphoreType.DMA(()), pltpu.SemaphoreType.DMA(())]),
        compiler_params=_params(1),
        name="moe_dispatch",
    )(fill, nchunk, n_valid, pos2d, xq)


def _expert_kernel(te_ref, nv_ref, xs_ref, perm_ref, wgu_ref, bgu_ref, wdn_ref, bdn_ref, ys_ref,
                   wgu_bf, wdn_bf, *, ff):
    j = pl.program_id(0)
    d = wgu_ref.shape[0]
    half = d // 2
    e = te_ref[j]
    e_prev = te_ref[jnp.maximum(j - 1, 0)]

    @pl.when((j == 0) | (e != e_prev))
    def _():
        rows = min(512, d)

        def cast_gu(c, carry):
            r0 = pl.multiple_of(c * rows, rows)
            w = wgu_ref[pl.ds(r0, rows), :].astype(BF16)
            wgu_bf[pl.ds(r0, rows), :] = _dot(w, perm_ref[...]).astype(BF16)
            return carry

        lax.fori_loop(0, d // rows, cast_gu, 0)
        rows_dn = 32

        def cast_dn(c, carry):
            r0 = pl.multiple_of(c * rows_dn, rows_dn)
            wdn_bf[pl.ds(r0, rows_dn), :] = wdn_ref[pl.ds(r0, rows_dn), :].astype(BF16)
            return carry

        lax.fori_loop(0, ff // rows_dn, cast_dn, 0)

    @pl.when(j < nv_ref[0])
    def _():
        lo, hi = _unpack_pair(xs_ref[...])
        h = (_dot(lo.astype(BF16), wgu_bf[pl.ds(0, half), :]) + _dot(hi.astype(BF16), wgu_bf[pl.ds(half, half), :])
             + bgu_ref[...])
        h_glu = jnp.minimum(h[:, :ff], SWIGLU_LIMIT)
        h_lin = jnp.clip(h[:, ff:], -SWIGLU_LIMIT, SWIGLU_LIMIT)
        act = (h_glu * _sigmoid(SWIGLU_ALPHA * h_glu) * (h_lin + 1.0)).astype(BF16)
        cw = min(512, half)
        for c in range(half // cw):
            lo_sl = slice(c * cw, (c + 1) * cw)
            hi_sl = slice(half + c * cw, half + (c + 1) * cw)
            y_lo = _dot(act, wdn_bf[:, lo_sl]) + bdn_ref[:, lo_sl]
            y_hi = _dot(act, wdn_bf[:, hi_sl]) + bdn_ref[:, hi_sl]
            ys_ref[:, lo_sl] = _pack_pair(y_lo, y_hi)

    @pl.when(j >= nv_ref[0])
    def _():
        ys_ref[...] = jnp.zeros_like(ys_ref)


def _experts(cfg, tile_e, n_valid, xs, perm, w_gu, b_gu_perm, w_dn, b_dn, layer):
    d, ff, ne = cfg.d_model, cfg.expert_ff, cfg.n_experts
    half = d // 2
    n_tiles, slot_rows = _moe_slots(cfg)
    tile_blk = lambda j, te, nv: (jnp.minimum(j, nv[0] - 1), 0)
    by_expert = lambda j, te, nv: (te[j], 0, 0)
    by_layer_expert = lambda j, te, nv: (layer, te[j], 0, 0)
    return pl.pallas_call(
        functools.partial(_expert_kernel, ff=ff),
        out_shape=jax.ShapeDtypeStruct((slot_rows, half), jnp.uint32),
        grid_spec=pltpu.PrefetchScalarGridSpec(
            num_scalar_prefetch=2,
            grid=(n_tiles,),
            in_specs=[
                pl.BlockSpec((MOE_TILE, half), tile_blk),
                pl.BlockSpec((2 * ff, 2 * ff), lambda j, te, nv: (0, 0)),
                pl.BlockSpec((None, None, d, 2 * ff), by_layer_expert),
                pl.BlockSpec((None, 1, 2 * ff), by_expert),
                pl.BlockSpec((None, None, ff, d), by_layer_expert),
                pl.BlockSpec((None, 1, d), by_expert),
            ],
            out_specs=pl.BlockSpec((MOE_TILE, half), lambda j, te, nv: (j, 0)),
            scratch_shapes=[pltpu.VMEM((d, 2 * ff), BF16), pltpu.VMEM((ff, d), BF16)]),
        compiler_params=_params(1),
        name="moe_experts",
    )(tile_e, n_valid, xs, perm, w_gu, b_gu_perm.reshape(ne, 1, 2 * ff), w_dn, b_dn.reshape(ne, 1, d))


def _combine_kernel(pos_ref, route_ref, res_ref, mod_ref, ys_hbm, o_ref, buf_ref, pos_smem, sem, idx_sem,
                    *, gate, top_k):
    bt, d = res_ref.shape
    half = d // 2
    n_assign = bt * top_k

    idx_copy = pltpu.make_async_copy(pos_ref, pos_smem, idx_sem)
    idx_copy.start()
    idx_copy.wait()

    def issue(t, carry):
        row, col = _assignment_index(t, top_k)
        for k in range(top_k):
            src = pos_smem[row, col + k]
            pltpu.make_async_copy(ys_hbm.at[pl.ds(src, 1)], buf_ref.at[k, pl.ds(t, 1)], sem).start()
        return carry

    lax.fori_loop(0, bt, issue, 0, unroll=2)
    pltpu.make_async_copy(buf_ref, buf_ref, sem).wait()

    weights = [route_ref[:, top_k + k:top_k + k + 1] for k in range(top_k)]
    cw = min(512, half)
    for c in range(half // cw):
        lo_sl = slice(c * cw, (c + 1) * cw)
        hi_sl = slice(half + c * cw, half + (c + 1) * cw)
        acc_lo = acc_hi = None
        for k in range(top_k):
            lo, hi = _unpack_pair(buf_ref[k, :, lo_sl])
            acc_lo = weights[k] * lo if acc_lo is None else acc_lo + weights[k] * lo
            acc_hi = weights[k] * hi if acc_hi is None else acc_hi + weights[k] * hi
        o_ref[:, lo_sl] = res_ref[:, lo_sl] + mod_ref[0, gate:gate + 1, lo_sl] * acc_lo
        o_ref[:, hi_sl] = res_ref[:, hi_sl] + mod_ref[0, gate:gate + 1, hi_sl] * acc_hi


def _combine(cfg, pos2d, route, res, mod, ys, gate):
    m, d = res.shape
    half = d // 2
    ep = route.shape[1]
    bt = MOE_BLOCK
    rows_per_step = bt * cfg.top_k // LANES
    return pl.pallas_call(
        functools.partial(_combine_kernel, gate=gate, top_k=cfg.top_k),
        out_shape=jax.ShapeDtypeStruct((m, d), F32),
        grid=(m // bt,),
        in_specs=[
            pl.BlockSpec((rows_per_step, LANES), lambda i: (i, 0)),
            pl.BlockSpec((bt, ep), lambda i: (i, 0)),
            pl.BlockSpec((bt, d), lambda i: (i, 0)),
            pl.BlockSpec((1, N_MOD, d), lambda i: (_seg_of_block(cfg, i, bt), 0, 0)),
            pl.BlockSpec(memory_space=pl.ANY),
        ],
        out_specs=pl.BlockSpec((bt, d), lambda i: (i, 0)),
        scratch_shapes=[pltpu.VMEM((cfg.top_k, bt, half), jnp.uint32),
                        pltpu.SMEM((rows_per_step, LANES), jnp.int32),
                        pltpu.SemaphoreType.DMA(()), pltpu.SemaphoreType.DMA(())],
        compiler_params=_params(1),
        name="moe_combine",
    )(pos2d, route, res, mod, ys)


def _moe_plan(cfg, route, counts8):
    k, ne = cfg.top_k, cfg.n_experts
    n_tiles, _ = _moe_slots(cfg)
    expert = route[:, :k].astype(jnp.int32)
    rank = route[:, 2 * k:3 * k].astype(jnp.int32)
    counts = counts8[0, :ne].astype(jnp.int32)
    padded = (counts + MOE_TILE - 1) // MOE_TILE * MOE_TILE
    region_end = jnp.cumsum(padded)
    region_start = region_end - padded
    pos = region_start[expert] + rank
    fill = (region_start + counts) // SUBLANES * SUBLANES
    nchunk = (region_end - fill) // SUBLANES
    n_valid = region_end[-1] // MOE_TILE
    tile_start = jnp.arange(n_tiles, dtype=jnp.int32) * MOE_TILE
    tile_e = jnp.sum(region_end[None, :] <= tile_start[:, None], axis=1)
    tile_e = jnp.minimum(tile_e, ne - 1).astype(jnp.int32)
    tile_e = tile_e[jnp.minimum(jnp.arange(n_tiles), n_valid - 1)]
    return (pos.reshape(-1, LANES), fill.astype(jnp.int32), nchunk.astype(jnp.int32), tile_e,
            n_valid.reshape(1).astype(jnp.int32))


def _deinterleave_perm(ff):
    col = jnp.arange(2 * ff)
    src = jnp.where(col < ff, 2 * col, 2 * (col - ff) + 1)
    return (jnp.arange(2 * ff)[:, None] == src[None, :]).astype(BF16)


def _forward(cfg, x, c, ctx, c_ctx, ada_a, ada_b, ada_bias, norm1_g, norm2_g, w_in, conv_a_w, lam_qk,
             subln_g, lru_conv_w, lru_conv_b, lru_w_a, lru_b_a, lru_w_x, lru_b_x, lru_lambda,
             w_branch, w_out, router_w, router_b, w_gu, b_gu, w_dn, b_dn, final_g):
    d, wd, bsz, s, lc = cfg.d_model, cfg.branch_w, cfg.batch, cfg.seq, cfg.ctx_len
    m_lat, m_all, ne = cfg.m_lat, cfg.m_all, cfg.n_experts
    assert bsz * lc <= s and s % lc == 0 and bsz + 1 <= SUBLANES
    bm = min(512, bsz * lc)
    bn = min(1024, d)
    bt = min(256, lc)
    ep = -(-ne // LANES) * LANES

    xs = jnp.concatenate([x.reshape(m_lat, d), ctx.reshape(bsz * lc, d)], axis=0)
    cond8 = jnp.zeros((SUBLANES, d), F32).at[:bsz].set(c).at[bsz].set(c_ctx)
    ctab, stab = _rope_tables(cfg, bt)
    ctx_blk = lambda b: m_lat // lc + b
    perm = _deinterleave_perm(cfg.expert_ff)

    for l in range(cfg.depth):
        lam_init = LAMBDA_INIT_BASE - LAMBDA_INIT_AMP * math.exp(-LAMBDA_INIT_RATE * l)
        mod = _ada(cfg, cond8, ada_a[l], ada_b[l], ada_bias[l])

        xn = _norm_mod(cfg, xs, norm1_g[l], mod, 0, 1)
        p = _matmul(xn, w_in, l, bm, bn)
        oa = jnp.concatenate([
            _conv_mixer(cfg, p, conv_a_w[l], s, 0, bsz, m_lat),
            _conv_mixer(cfg, p, conv_a_w[l], lc, m_lat // lc, bsz, bsz * lc)], axis=0)
        qt, kr, vt = _att_prep(cfg, p, ctab, stab, bt)
        nh = wd // ATT_VD
        ctx_seg = (p, ctx_blk, 4 * nh, vt, lc)
        lat_seg = (kr, lambda b: b, 0, vt, s)
        ob_lat = _attention(cfg, lam_qk[l], subln_g[l], qt, [ctx_seg, lat_seg], 0, s // bt, m_lat, lam_init, bq=bt)
        ob_ctx = _attention(cfg, lam_qk[l], subln_g[l], qt, [ctx_seg], m_lat // bt, lc // bt, bsz * lc, lam_init, bq=bt)
        ob = jnp.concatenate([ob_lat, ob_ctx], axis=0)
        oc = jnp.concatenate(_lru(cfg, p, lru_conv_w[l], lru_conv_b[l], lru_w_a[l], lru_b_a[l],
                                  lru_w_x[l], lru_b_x[l], lru_lambda[l]), axis=0)
        y = _merge(cfg, oa, ob, oc, p, w_branch, l, bm, bn)
        xs = _matmul_residual(cfg, y, w_out, l, xs, mod, 2, bm, min(512, bn))

        wr_pad = jnp.zeros((d, ep), F32).at[:, :ne].set(router_w[l])
        br_pad = jnp.full((1, ep), -jnp.inf, F32).at[0, :ne].set(router_b[l])
        xq, route, counts8 = _norm_router(cfg, xs, norm2_g[l], mod, 3, 4, wr_pad, br_pad)
        pos2d, fill, nchunk, tile_e, n_valid = _moe_plan(cfg, route, counts8)
        slots = _dispatch(cfg, fill, nchunk, n_valid, pos2d, xq)
        b_gu_perm = jnp.concatenate([b_gu[l][:, 0::2], b_gu[l][:, 1::2]], axis=-1)
        ys = _experts(cfg, tile_e, n_valid, slots, perm, w_gu, b_gu_perm, w_dn, b_dn[l], l)
        xs = _combine(cfg, pos2d, route, xs, mod, ys, 5)

    return _norm_plain(xs, final_g, m_lat).reshape(bsz, s, d)


def kernel(x, c, ctx, c_ctx, ada_a, ada_b, ada_bias, norm1_g, norm2_g, w_in, conv_a_w, lam_qk, subln_g, lru_conv_w, lru_conv_b, lru_w_a, lru_b_a, lru_w_x, lru_b_x, lru_lambda, w_branch, w_out, router_w, router_b, w_gu, b_gu, w_dn, b_dn, final_g):
    return _forward(_Cfg(), x, c, ctx, c_ctx, ada_a, ada_b, ada_bias, norm1_g, norm2_g, w_in, conv_a_w, lam_qk,
                    subln_g, lru_conv_w, lru_conv_b, lru_w_a, lru_b_a, lru_w_x, lru_b_x, lru_lambda,
                    w_branch, w_out, router_w, router_b, w_gu, b_gu, w_dn, b_dn, final_g)
```

```python
import functools
import math
from typing import NamedTuple

import jax
import jax.numpy as jnp
from jax import lax
from jax.experimental import pallas as pl
from jax.experimental.pallas import tpu as pltpu

F32 = jnp.float32
BF16 = jnp.bfloat16

LANES = 128
SUBLANES = 8
EPS = 1e-6
ATT_HD = 64
ATT_VD = 2 * ATT_HD
ATT_UNROLL = 8
ATT_Q_SCALE = ATT_HD ** -0.5 * math.log2(math.e)
ROPE_THETA = 10000.0
ROPE_AXIS_DIM = ATT_HD // 2
ROPE_PAIRS = ROPE_AXIS_DIM // 2
LAMBDA_INIT_BASE = 0.8
LAMBDA_INIT_AMP = 0.6
LAMBDA_INIT_RATE = 0.3
LRU_C = 8.0
SWIGLU_ALPHA = 1.702
SWIGLU_LIMIT = 7.0
N_MOD = 6
VMEM_LIMIT = 56 * 1024 * 1024


class _Cfg(NamedTuple):
    d_model: int = 4096
    batch: int = 2
    seq: int = 4096
    depth: int = 4
    grid_w: int = 64
    ctx_len: int = 256
    branch_w: int = 1024
    n_experts: int = 32
    top_k: int = 4
    expert_ff: int = 256

    @property
    def heads(self):
        return self.branch_w // ATT_VD

    @property
    def m_lat(self):
        return self.batch * self.seq

    @property
    def m_all(self):
        return self.batch * (self.seq + self.ctx_len)

    @property
    def in_cols(self):
        return 8 * self.branch_w + 3 * self.d_model


def _params(n_grid_dims):
    return pltpu.CompilerParams(
        dimension_semantics=("arbitrary",) * n_grid_dims, vmem_limit_bytes=VMEM_LIMIT)


def _dot(a, b):
    return jnp.dot(a, b, preferred_element_type=F32)


def _split_bf16(x):
    hi = x.astype(BF16)
    lo = (x - hi.astype(F32)).astype(BF16)
    return hi, lo


def _dot3(x, w):
    xh, xl = _split_bf16(x)
    wh, wl = _split_bf16(w)
    return _dot(xh, wh) + (_dot(xl, wh) + _dot(xh, wl))


def _sigmoid(x):
    return 1.0 / (1.0 + jnp.exp(-x))


def _ada_kernel(cond_ref, a_ref, b_ref, bias_ref, o_ref):
    cond = cond_ref[...]
    t = _dot3(cond * _sigmoid(cond), a_ref[...])
    o_ref[...] = _dot3(t, b_ref[...]) + bias_ref[...]


def _ada(cfg, cond8, a, b, bias):
    d = cfg.d_model
    rank = a.shape[1]
    out = pl.pallas_call(
        _ada_kernel,
        out_shape=jax.ShapeDtypeStruct((SUBLANES, N_MOD * d), F32),
        grid=(N_MOD,),
        in_specs=[
            pl.BlockSpec((SUBLANES, d), lambda j: (0, 0)),
            pl.BlockSpec((d, rank), lambda j: (0, 0)),
            pl.BlockSpec((rank, d), lambda j: (0, j)),
            pl.BlockSpec((1, d), lambda j: (0, j)),
        ],
        out_specs=pl.BlockSpec((SUBLANES, d), lambda j: (0, j)),
        compiler_params=_params(1),
        name="ada_modulation",
    )(cond8, a, b, bias.reshape(1, N_MOD * d))
    return out.reshape(SUBLANES, N_MOD, d)[: cfg.batch + 1]


def _rms(x, g):
    return x * lax.rsqrt(jnp.mean(x * x, axis=-1, keepdims=True) + EPS) * g


def _norm_mod_kernel(x_ref, g_ref, mod_ref, o_ref, *, shift, scale):
    y = _rms(x_ref[...], g_ref[...])
    y = y * (1.0 + mod_ref[0, scale:scale + 1, :]) + mod_ref[0, shift:shift + 1, :]
    o_ref[...] = y.astype(o_ref.dtype)


def _norm_plain_kernel(x_ref, g_ref, o_ref):
    o_ref[...] = _rms(x_ref[...], g_ref[...]).astype(o_ref.dtype)


def _pack_pair(lo, hi):
    lo_bits = lax.bitcast_convert_type(lo.astype(BF16).astype(F32), jnp.uint32)
    hi_bits = lax.bitcast_convert_type(hi.astype(BF16).astype(F32), jnp.uint32)
    return (lo_bits >> 16) | (hi_bits & jnp.uint32(0xFFFF0000))


def _unpack_pair(word):
    lo = lax.bitcast_convert_type(word << 16, F32)
    hi = lax.bitcast_convert_type(word & jnp.uint32(0xFFFF0000), F32)
    return lo, hi


def _norm_router_kernel(x_ref, g_ref, mod_ref, wr_ref, br_ref, xq_ref, route_ref, counts_ref, carry_ref,
                        *, shift, scale, top_k):
    @pl.when(pl.program_id(0) == 0)
    def _():
        carry_ref[...] = jnp.zeros_like(carry_ref)

    y = _rms(x_ref[...], g_ref[...])
    y = y * (1.0 + mod_ref[0, scale:scale + 1, :]) + mod_ref[0, shift:shift + 1, :]
    half = y.shape[1] // 2
    xq_ref[...] = _pack_pair(y[:, :half], y[:, half:])
    work = _dot3(y, wr_ref[...]) + br_ref[...]
    bm, n_lane = work.shape
    lane = lax.broadcasted_iota(jnp.int32, work.shape, 1).astype(F32)
    vals, hots, ids = [], [], []
    for _ in range(top_k):
        m = jnp.max(work, axis=-1, keepdims=True)
        idx = jnp.min(jnp.where(work == m, lane, float(n_lane)), axis=-1, keepdims=True)
        hot = lane == idx
        vals.append(m)
        hots.append(hot)
        ids.append(idx)
        work = jnp.where(hot, -jnp.inf, work)
    exps = [jnp.exp(v - vals[0]) for v in vals]
    denom = exps[0]
    for e in exps[1:]:
        denom = denom + e
    picked = jnp.zeros(work.shape, F32)
    for hot in hots:
        picked = jnp.where(hot, 1.0, picked)
    tri = (lax.broadcasted_iota(jnp.int32, (bm, bm), 0) > lax.broadcasted_iota(jnp.int32, (bm, bm), 1))
    before = _dot(jnp.where(tri, 1.0, 0.0).astype(BF16), picked.astype(BF16)) + carry_ref[0:1, :]
    carry_ref[...] = carry_ref[...] + jnp.sum(picked, axis=0, keepdims=True)
    counts_ref[...] = carry_ref[...]
    route = jnp.zeros(work.shape, F32)
    for k in range(top_k):
        rank = jnp.sum(jnp.where(hots[k], before, 0.0), axis=-1, keepdims=True)
        route = jnp.where(lane == float(k), ids[k], route)
        route = jnp.where(lane == float(top_k + k), exps[k] / denom, route)
        route = jnp.where(lane == float(2 * top_k + k), rank, route)
    route_ref[...] = route


def _seg_of_block(cfg, i, bm):
    return (i * bm) // cfg.seq


def _norm_mod(cfg, x, g, mod, shift, scale, bm=256):
    m, d = x.shape
    return pl.pallas_call(
        functools.partial(_norm_mod_kernel, shift=shift, scale=scale),
        out_shape=jax.ShapeDtypeStruct((m, d), BF16),
        grid=(m // bm,),
        in_specs=[
            pl.BlockSpec((bm, d), lambda i: (i, 0)),
            pl.BlockSpec((1, d), lambda i: (0, 0)),
            pl.BlockSpec((1, N_MOD, d), lambda i: (_seg_of_block(cfg, i, bm), 0, 0)),
        ],
        out_specs=pl.BlockSpec((bm, d), lambda i: (i, 0)),
        compiler_params=_params(1),
        name="norm_mod",
    )(x, g.reshape(1, d), mod)


def _norm_plain(x, g, m_rows, bm=256):
    d = x.shape[1]
    return pl.pallas_call(
        _norm_plain_kernel,
        out_shape=jax.ShapeDtypeStruct((m_rows, d), F32),
        grid=(m_rows // bm,),
        in_specs=[
            pl.BlockSpec((bm, d), lambda i: (i, 0)),
            pl.BlockSpec((1, d), lambda i: (0, 0)),
        ],
        out_specs=pl.BlockSpec((bm, d), lambda i: (i, 0)),
        compiler_params=_params(1),
        name="final_norm",
    )(x, g.reshape(1, d))


def _norm_router(cfg, x, g, mod, shift, scale, wr_pad, br_pad, bm=256):
    m, d = x.shape
    ep = wr_pad.shape[1]
    return pl.pallas_call(
        functools.partial(_norm_router_kernel, shift=shift, scale=scale, top_k=cfg.top_k),
        out_shape=(jax.ShapeDtypeStruct((m, d // 2), jnp.uint32), jax.ShapeDtypeStruct((m, ep), F32),
                   jax.ShapeDtypeStruct((SUBLANES, ep), F32)),
        grid=(m // bm,),
        in_specs=[
            pl.BlockSpec((bm, d), lambda i: (i, 0)),
            pl.BlockSpec((1, d), lambda i: (0, 0)),
            pl.BlockSpec((1, N_MOD, d), lambda i: (_seg_of_block(cfg, i, bm), 0, 0)),
            pl.BlockSpec((d, ep), lambda i: (0, 0)),
            pl.BlockSpec((1, ep), lambda i: (0, 0)),
        ],
        out_specs=(pl.BlockSpec((bm, d // 2), lambda i: (i, 0)), pl.BlockSpec((bm, ep), lambda i: (i, 0)),
                   pl.BlockSpec((SUBLANES, ep), lambda i: (0, 0))),
        scratch_shapes=[pltpu.VMEM((SUBLANES, ep), F32)],
        compiler_params=_params(1),
        name="norm_router",
    )(x, g.reshape(1, d), mod, wr_pad, br_pad)


CAST_ROWS = 256


def _cast_weight_block(w_ref, wbf_ref):
    @pl.when(pl.program_id(1) == 0)
    def _():
        rows = min(CAST_ROWS, w_ref.shape[0])

        def body(c, carry):
            r0 = pl.multiple_of(c * rows, rows)
            wbf_ref[pl.ds(r0, rows), :] = w_ref[pl.ds(r0, rows), :].astype(BF16)
            return carry

        lax.fori_loop(0, w_ref.shape[0] // rows, body, 0)


def _mm_kernel(x_ref, w_ref, o_ref, wbf_ref):
    _cast_weight_block(w_ref, wbf_ref)
    o_ref[...] = _dot(x_ref[...], wbf_ref[...]).astype(o_ref.dtype)


def _mm_res_kernel(x_ref, w_ref, res_ref, mod_ref, o_ref, wbf_ref, *, gate):
    _cast_weight_block(w_ref, wbf_ref)
    o_ref[...] = res_ref[...] + mod_ref[0, gate:gate + 1, :] * _dot(x_ref[...], wbf_ref[...])


def _matmul(x, w, layer, bm, bn, out_dtype=BF16):
    m, k = x.shape
    n = w.shape[2]
    return pl.pallas_call(
        _mm_kernel,
        out_shape=jax.ShapeDtypeStruct((m, n), out_dtype),
        grid=(n // bn, m // bm),
        in_specs=[
            pl.BlockSpec((bm, k), lambda j, i: (i, 0)),
            pl.BlockSpec((None, k, bn), lambda j, i: (layer, 0, j)),
        ],
        out_specs=pl.BlockSpec((bm, bn), lambda j, i: (i, j)),
        scratch_shapes=[pltpu.VMEM((k, bn), BF16)],
        compiler_params=_params(2),
        name="matmul",
    )(x, w)


def _matmul_residual(cfg, x, w, layer, res, mod, gate, bm, bn):
    m, k = x.shape
    n = w.shape[2]
    return pl.pallas_call(
        functools.partial(_mm_res_kernel, gate=gate),
        out_shape=jax.ShapeDtypeStruct((m, n), F32),
        grid=(n // bn, m // bm),
        in_specs=[
            pl.BlockSpec((bm, k), lambda j, i: (i, 0)),
            pl.BlockSpec((None, k, bn), lambda j, i: (layer, 0, j), pipeline_mode=pl.Buffered(1)),
            pl.BlockSpec((bm, bn), lambda j, i: (i, j)),
            pl.BlockSpec((1, N_MOD, bn), lambda j, i: (_seg_of_block(cfg, i, bm), 0, j)),
        ],
        out_specs=pl.BlockSpec((bm, bn), lambda j, i: (i, j)),
        scratch_shapes=[pltpu.VMEM((k, bn), BF16)],
        compiler_params=_params(2),
        name="matmul_residual",
    )(x, w, res, mod)


def _conv_a_kernel(b_ref, c_ref, h_ref, w_ref, o_ref, *, tt):
    t_len = b_ref.shape[0]
    w = w_ref[...]
    halo = 16

    def prod(r0, n):
        return c_ref[pl.ds(r0, n), :].astype(F32) * h_ref[pl.ds(r0, n), :].astype(F32)

    def chunk(ci, carry):
        r0 = pl.multiple_of(ci * tt, tt)
        u = prod(r0, tt)
        prev = prod(pl.multiple_of(jnp.maximum(r0 - halo, 0), halo), halo)[halo - 1:halo, :]
        prev = jnp.where(r0 == 0, 0.0, prev)
        nxt = prod(pl.multiple_of(jnp.minimum(r0 + tt, t_len - halo), halo), halo)[0:1, :]
        nxt = jnp.where(r0 + tt == t_len, 0.0, nxt)
        row = lax.broadcasted_iota(jnp.int32, u.shape, 0)
        up = jnp.where(row == 0, prev, pltpu.roll(u, 1, 0))
        dn = jnp.where(row == tt - 1, nxt, pltpu.roll(u, tt - 1, 0))
        conv = w[0:1, :] * up + w[1:2, :] * u + w[2:3, :] * dn
        o_ref[pl.ds(r0, tt), :] = (b_ref[pl.ds(r0, tt), :].astype(F32) * conv).astype(o_ref.dtype)
        return carry

    lax.fori_loop(0, t_len // tt, chunk, 0)


def _conv_mixer(cfg, p, w, t_len, row_blk0, n_seq, out_rows, bc=256):
    wd = cfg.branch_w
    ncb = wd // bc
    tt = min(256, t_len)
    spec = lambda off: pl.BlockSpec((t_len, bc), lambda s, j: (row_blk0 + s, off * ncb + j))
    return pl.pallas_call(
        functools.partial(_conv_a_kernel, tt=tt),
        out_shape=jax.ShapeDtypeStruct((out_rows, wd), BF16),
        grid=(n_seq, ncb),
        in_specs=[spec(0), spec(1), spec(2), pl.BlockSpec((3, bc), lambda s, j: (0, j))],
        out_specs=pl.BlockSpec((t_len, bc), lambda s, j: (s, j)),
        compiler_params=_params(2),
        name="conv_mixer",
    )(p, p, p, w)


def _rope_tables(cfg, bt):
    t = jnp.arange(cfg.seq)
    pos = jnp.stack([t // cfg.grid_w, t % cfg.grid_w], axis=-1).astype(F32)
    inv = ROPE_THETA ** (-jnp.arange(ROPE_PAIRS, dtype=F32) * 2.0 / ROPE_AXIS_DIM)
    ang = pos[:, :, None] * inv
    cos, sin = jnp.cos(ang), jnp.sin(ang)
    cmap = jnp.concatenate([cos[:, 0], cos[:, 0], cos[:, 1], cos[:, 1]], axis=-1)
    smap = jnp.concatenate([-sin[:, 0], sin[:, 0], -sin[:, 1], sin[:, 1]], axis=-1)
    ctab = jnp.concatenate([jnp.tile(cmap, (1, 2)), jnp.ones((bt, ATT_VD), F32)], axis=0)
    stab = jnp.concatenate([jnp.tile(smap, (1, 2)), jnp.zeros((bt, ATT_VD), F32)], axis=0)
    return ctab, stab


def _att_prep_kernel(q_ref, k_ref, v_ref, c_ref, s_ref, qt_ref, kr_ref, vt_ref, *, heads):
    cos = c_ref[...]
    sin = s_ref[...]
    lane = lax.broadcasted_iota(jnp.int32, cos.shape, 1)
    low = (lane % ROPE_AXIS_DIM) < ROPE_PAIRS

    def rope(u):
        swapped = jnp.where(low, pltpu.roll(u, ATT_VD - ROPE_PAIRS, 1), pltpu.roll(u, ROPE_PAIRS, 1))
        return u * cos + swapped * sin

    for h in range(heads):
        sl = slice(h * ATT_VD, (h + 1) * ATT_VD)
        qt_ref[sl, :] = (rope(q_ref[:, sl].astype(F32)) * ATT_Q_SCALE).T.astype(qt_ref.dtype)
        kr_ref[:, sl] = rope(k_ref[:, sl].astype(F32)).astype(kr_ref.dtype)
        vt_ref[sl, :] = v_ref[:, sl].astype(F32).T.astype(vt_ref.dtype)


def _att_prep(cfg, p, ctab, stab, bt=256):
    m, wd = cfg.m_all, cfg.branch_w
    n_lat_blk = cfg.m_lat // bt
    n_seq_blk = cfg.seq // bt
    tab_map = lambda i: (jnp.where(i < n_lat_blk, i % n_seq_blk, n_seq_blk), 0)
    return pl.pallas_call(
        functools.partial(_att_prep_kernel, heads=cfg.heads),
        out_shape=(jax.ShapeDtypeStruct((wd, m), BF16), jax.ShapeDtypeStruct((m, wd), BF16),
                   jax.ShapeDtypeStruct((wd, m), BF16)),
        grid=(m // bt,),
        in_specs=[
            pl.BlockSpec((bt, wd), lambda i: (i, 3)),
            pl.BlockSpec((bt, wd), lambda i: (i, 4)),
            pl.BlockSpec((bt, wd), lambda i: (i, 5)),
            pl.BlockSpec((bt, ATT_VD), tab_map),
            pl.BlockSpec((bt, ATT_VD), tab_map),
        ],
        out_specs=(pl.BlockSpec((wd, bt), lambda i: (0, i)), pl.BlockSpec((bt, wd), lambda i: (i, 0)),
                   pl.BlockSpec((wd, bt), lambda i: (0, i))),
        compiler_params=_params(1),
        name="att_prep",
    )(p, p, p, ctab, stab)


def _attn_kernel(lam_ref, g_ref, qt_ref, *refs, seg_lens, tk, lam_init):
    n_seg = len(seg_lens)
    kv = refs[:2 * n_seg]
    o_ref = refs[2 * n_seg]
    s1_ref, s2_ref, acc1_ref, acc2_ref = refs[2 * n_seg + 1:]
    bq = qt_ref.shape[1]

    qt = qt_ref[...].astype(F32)
    sub = lax.broadcasted_iota(jnp.int32, qt.shape, 0)
    q1 = jnp.where(sub < ATT_HD, qt, 0.0).astype(qt_ref.dtype)
    q2 = jnp.where(sub >= ATT_HD, qt, 0.0).astype(qt_ref.dtype)

    def chunks(seg):
        tkk = min(tk, seg_lens[seg])
        return tkk, seg_lens[seg] // tkk

    m1 = jnp.full((1, bq), -jnp.inf, F32)
    m2 = jnp.full((1, bq), -jnp.inf, F32)
    off = 0
    for seg in range(n_seg):
        k_ref = kv[2 * seg]
        tkk, n_chunk = chunks(seg)

        def body_a(c, carry, k_ref=k_ref, tkk=tkk, off=off):
            m1, m2 = carry
            r0 = pl.multiple_of(c * tkk, tkk)
            kc = k_ref[pl.ds(r0, tkk), :]
            s1 = _dot(kc, q1)
            s2 = _dot(kc, q2)
            s1_ref[pl.ds(off + r0, tkk), :] = s1
            s2_ref[pl.ds(off + r0, tkk), :] = s2
            return (jnp.maximum(m1, jnp.max(s1, axis=0, keepdims=True)),
                    jnp.maximum(m2, jnp.max(s2, axis=0, keepdims=True)))

        m1, m2 = lax.fori_loop(0, n_chunk, body_a, (m1, m2), unroll=min(ATT_UNROLL, n_chunk))
        off += seg_lens[seg]

    acc1_ref[...] = jnp.zeros_like(acc1_ref)
    acc2_ref[...] = jnp.zeros_like(acc2_ref)
    l1 = jnp.zeros((1, bq), F32)
    l2 = jnp.zeros((1, bq), F32)
    off = 0
    for seg in range(n_seg):
        vt_ref = kv[2 * seg + 1]
        tkk, n_chunk = chunks(seg)

        def body_b(c, carry, vt_ref=vt_ref, tkk=tkk, off=off):
            l1, l2 = carry
            r0 = pl.multiple_of(c * tkk, tkk)
            p1 = jnp.exp2(s1_ref[pl.ds(off + r0, tkk), :] - m1)
            p2 = jnp.exp2(s2_ref[pl.ds(off + r0, tkk), :] - m2)
            vt = vt_ref[:, pl.ds(r0, tkk)]
            acc1_ref[...] += _dot(vt, p1.astype(vt.dtype))
            acc2_ref[...] += _dot(vt, p2.astype(vt.dtype))
            return (l1 + jnp.sum(p1, axis=0, keepdims=True), l2 + jnp.sum(p2, axis=0, keepdims=True))

        l1, l2 = lax.fori_loop(0, n_chunk, body_b, (l1, l2), unroll=min(ATT_UNROLL, n_chunk))
        off += seg_lens[seg]

    lq = lam_ref[...]
    lam = (jnp.exp(jnp.sum(lq[0:1] * lq[1:2], axis=-1, keepdims=True))
           - jnp.exp(jnp.sum(lq[2:3] * lq[3:4], axis=-1, keepdims=True)) + lam_init)
    ot = acc1_ref[...] * (1.0 / l1) - acc2_ref[...] * (lam / l2)
    ms = jnp.mean(ot * ot, axis=0, keepdims=True)
    y = ot * lax.rsqrt(ms + EPS) * g_ref[...] * (1.0 - lam_init)
    o_ref[...] = y.T.astype(o_ref.dtype)


def _attention(cfg, lam_qk, subln_g, qt, segs, q_col_blk0, n_q_per_batch, out_rows, lam_init, bq=256, tk=512):
    wd = cfg.branch_w
    in_specs = [
        pl.BlockSpec((4, ATT_HD), lambda b, h, i: (0, 0)),
        pl.BlockSpec((ATT_VD, 1), lambda b, h, i: (0, 0)),
        pl.BlockSpec((ATT_VD, bq), lambda b, h, i: (h, q_col_blk0 + b * n_q_per_batch + i)),
    ]
    args = [lam_qk, subln_g.reshape(ATT_VD, 1), qt]
    seg_lens = []
    for k_arr, k_blk, k_col0, vt_arr, length in segs:
        in_specs.append(pl.BlockSpec((length, ATT_VD), lambda b, h, i, f=k_blk, c0=k_col0: (f(b), c0 + h)))
        in_specs.append(pl.BlockSpec((ATT_VD, length), lambda b, h, i, f=k_blk: (h, f(b))))
        args += [k_arr, vt_arr]
        seg_lens.append(length)
    l_tot = sum(seg_lens)
    return pl.pallas_call(
        functools.partial(_attn_kernel, seg_lens=tuple(seg_lens), tk=tk, lam_init=lam_init),
        out_shape=jax.ShapeDtypeStruct((out_rows, wd), BF16),
        grid=(cfg.batch, cfg.heads, n_q_per_batch),
        in_specs=in_specs,
        out_specs=pl.BlockSpec((bq, ATT_VD), lambda b, h, i: (b * n_q_per_batch + i, h)),
        scratch_shapes=[pltpu.VMEM((l_tot, bq), F32), pltpu.VMEM((l_tot, bq), F32),
                        pltpu.VMEM((ATT_VD, bq), F32), pltpu.VMEM((ATT_VD, bq), F32)],
        compiler_params=_params(3),
        name="diff_attention",
    )(*args)


SCAN_PAD = SUBLANES


def _scan_levels(n):
    plan = []
    while n > 1:
        fan = SUBLANES if n % SUBLANES == 0 else n
        assert fan <= SUBLANES, "sequence length must factor into groups of at most 8"
        plan.append((n, fan, n // fan))
        n //= fan
    return plan


def _scan_scratch_rows(n):
    rows = 0
    for _, _, parts in _scan_levels(n):
        rows += 2 * SCAN_PAD + -(-parts // SUBLANES) * SUBLANES
    return max(rows, SUBLANES)


def _scan_inplace(a_ref, b_ref, base, n, rev, sa_ref, sb_ref, off=0):
    if n == 1:
        return
    fan = SUBLANES if n % SUBLANES == 0 else n
    parts = n // fan
    order = list(range(fan))
    if rev:
        order.reverse()

    def rows(r):
        return pl.ds(base + r, parts, stride=fan) if parts > 1 else pl.ds(base + r, 1)

    acc_a = acc_b = None
    for r in order:
        a_r = a_ref[rows(r), :]
        b_r = b_ref[rows(r), :]
        if acc_a is None:
            acc_a, acc_b = a_r, b_r
        else:
            acc_b = a_r * acc_b + b_r
            acc_a = a_r * acc_a
            a_ref[rows(r), :] = acc_a
            b_ref[rows(r), :] = acc_b
    if parts == 1:
        return
    data = off + SCAN_PAD
    ones = jnp.ones((SCAN_PAD, LANES), F32)
    zeros = jnp.zeros((SCAN_PAD, LANES), F32)
    sa_ref[pl.ds(off, SCAN_PAD), :] = ones
    sb_ref[pl.ds(off, SCAN_PAD), :] = zeros
    sa_ref[pl.ds(data + parts, SCAN_PAD), :] = ones
    sb_ref[pl.ds(data + parts, SCAN_PAD), :] = zeros
    sa_ref[pl.ds(data, parts), :] = acc_a
    sb_ref[pl.ds(data, parts), :] = acc_b
    nxt = off + 2 * SCAN_PAD + -(-parts // SUBLANES) * SUBLANES
    _scan_inplace(sa_ref, sb_ref, data, parts, rev, sa_ref, sb_ref, nxt)
    shift = data + 1 if rev else data - 1
    carry_a = sa_ref[pl.ds(shift, parts), :]
    carry_b = sb_ref[pl.ds(shift, parts), :]
    for r in order:
        a_r = a_ref[rows(r), :]
        b_ref[rows(r), :] = b_ref[rows(r), :] + a_r * carry_b
        a_ref[rows(r), :] = a_r * carry_a


def _softplus(x):
    return jnp.maximum(x, 0.0) + jnp.log1p(jnp.exp(-jnp.abs(x)))


def _gelu_tanh(x):
    return 0.5 * x * (1.0 + jnp.tanh(math.sqrt(2.0 / math.pi) * (x + 0.044715 * (x * x * x))))


def _lru_kernel(lxl_ref, lyl_ref, lxc_ref, lyc_ref, cw_ref, cb_ref, wa_ref, ba_ref, wx_ref, bx_ref, lam_ref,
                ol_ref, oc_ref, pad_ref, a0_ref, b0_ref, a1_ref, b1_ref, sa_ref, sb_ref, *, tt):
    l_ctx = lxc_ref.shape[0]
    l_lat = lxl_ref.shape[0]
    cw = cw_ref[...]
    cb = cb_ref[...]
    ab_refs = ((a0_ref, b0_ref), (a1_ref, b1_ref))
    neg_sp = [-LRU_C * _softplus(-lam_ref[d:d + 1, :]) for d in range(2)]
    wa = [wa_ref[d].astype(BF16) for d in range(2)]
    wx = [wx_ref[d].astype(BF16) for d in range(2)]

    for lx_ref, base, length, is_ctx in ((lxc_ref, 0, l_ctx, True), (lxl_ref, l_ctx, l_lat, False)):
        ttk = min(tt, length)
        pad_ref[pl.ds(0, SUBLANES), :] = jnp.zeros((SUBLANES, LANES), F32)
        pad_ref[pl.ds(SUBLANES + length, SUBLANES), :] = jnp.zeros((SUBLANES, LANES), F32)

        def fill(ci, carry, lx_ref=lx_ref, ttk=ttk):
            r0 = pl.multiple_of(ci * ttk, ttk)
            pad_ref[pl.ds(SUBLANES + r0, ttk), :] = lx_ref[pl.ds(r0, ttk), :].astype(F32)
            return carry

        lax.fori_loop(0, length // ttk, fill, 0)

        def gates(ci, carry, base=base, length=length, is_ctx=is_ctx, ttk=ttk):
            r0 = pl.multiple_of(ci * ttk, ttk)
            xx = pad_ref[pl.ds(r0, ttk + 2 * SUBLANES), :]
            u = (cw[0:1, :] * xx[6:6 + ttk] + cw[1:2, :] * xx[7:7 + ttk]
                 + cw[2:3, :] * xx[8:8 + ttk] + cw[3:4, :] * xx[9:9 + ttk]) + cb
            ub = u.astype(BF16)
            row = r0 + lax.broadcasted_iota(jnp.int32, u.shape, 0)
            for d in range(2):
                r_gate = _sigmoid(_dot(ub, wa[d]) + ba_ref[d:d + 1, :])
                i_gate = _sigmoid(_dot(ub, wx[d]) + bx_ref[d:d + 1, :])
                log_a = neg_sp[d] * r_gate
                a = jnp.exp(log_a)
                mult = jnp.sqrt(-jnp.tanh(log_a) * (a * a + 1.0))
                if is_ctx:
                    mult = jnp.where(row == (length - 1 if d == 1 else 0), 1.0, mult)
                a_ref, b_ref = ab_refs[d]
                a_ref[pl.ds(base + r0, ttk), :] = a
                b_ref[pl.ds(base + r0, ttk), :] = mult * (i_gate * u)
            return carry

        lax.fori_loop(0, length // ttk, gates, 0)

    for d, rev in ((0, False), (1, True)):
        a_ref, b_ref = ab_refs[d]
        _scan_inplace(a_ref, b_ref, 0, l_ctx, rev, sa_ref, sb_ref)
        _scan_inplace(a_ref, b_ref, l_ctx, l_lat, rev, sa_ref, sb_ref)

    h0 = (b0_ref[pl.ds(l_ctx - 1, 1), :], b1_ref[pl.ds(0, 1), :])

    for ly_ref, o_ref, base, length, is_ctx in ((lyc_ref, oc_ref, 0, l_ctx, True), (lyl_ref, ol_ref, l_ctx, l_lat, False)):
        ttk = min(tt, length)

        def emit(ci, carry, ly_ref=ly_ref, o_ref=o_ref, base=base, is_ctx=is_ctx, ttk=ttk):
            r0 = pl.multiple_of(ci * ttk, ttk)
            h = b0_ref[pl.ds(base + r0, ttk), :] + b1_ref[pl.ds(base + r0, ttk), :]
            if not is_ctx:
                h = h + a0_ref[pl.ds(base + r0, ttk), :] * h0[0] + a1_ref[pl.ds(base + r0, ttk), :] * h0[1]
            o_ref[pl.ds(r0, ttk), :] = (h * _gelu_tanh(ly_ref[pl.ds(r0, ttk), :].astype(F32))).astype(o_ref.dtype)
            return carry

        lax.fori_loop(0, length // ttk, emit, 0)


def _lru(cfg, p, conv_w, conv_b, w_a, b_a, w_x, b_x, lam_p, tt=256):
    wd, s, lc = cfg.branch_w, cfg.seq, cfg.ctx_len
    nb = wd // LANES
    l_tot = s + lc
    ctx_blk0 = cfg.m_lat // lc
    lx_col, ly_col = 6 * nb, 7 * nb
    vec2 = pl.BlockSpec((2, LANES), lambda b, n: (0, n))
    wspec = pl.BlockSpec((2, None, LANES, LANES), lambda b, n: (0, n, 0, 0))
    scan_rows = max(_scan_scratch_rows(s), _scan_scratch_rows(lc))
    return pl.pallas_call(
        functools.partial(_lru_kernel, tt=tt),
        out_shape=(jax.ShapeDtypeStruct((cfg.m_lat, wd), BF16), jax.ShapeDtypeStruct((cfg.batch * lc, wd), BF16)),
        grid=(cfg.batch, nb),
        in_specs=[
            pl.BlockSpec((s, LANES), lambda b, n: (b, lx_col + n)),
            pl.BlockSpec((s, LANES), lambda b, n: (b, ly_col + n)),
            pl.BlockSpec((lc, LANES), lambda b, n: (ctx_blk0 + b, lx_col + n)),
            pl.BlockSpec((lc, LANES), lambda b, n: (ctx_blk0 + b, ly_col + n)),
            pl.BlockSpec((4, LANES), lambda b, n: (0, n)),
            pl.BlockSpec((1, LANES), lambda b, n: (0, n)),
            wspec, vec2, wspec, vec2, vec2,
        ],
        out_specs=(pl.BlockSpec((s, LANES), lambda b, n: (b, n)), pl.BlockSpec((lc, LANES), lambda b, n: (b, n))),
        scratch_shapes=[pltpu.VMEM((s + 2 * SUBLANES, LANES), F32)]
        + [pltpu.VMEM((l_tot, LANES), F32)] * 4
        + [pltpu.VMEM((scan_rows, LANES), F32)] * 2,
        compiler_params=_params(2),
        name="rglru",
    )(p, p, p, p, conv_w, conv_b.reshape(1, wd), w_a, b_a, w_x, b_x, lam_p)


def _merge_kernel(*refs, wd, n_lat_blk):
    branch_refs, (ga_ref, gb_ref, gc_ref, w_ref, o_ref, wbf_ref) = refs[:6], refs[6:]
    _cast_weight_block(w_ref, wbf_ref)
    is_lat = pl.program_id(1) < n_lat_blk
    acc = None
    for i, g_ref in enumerate((ga_ref, gb_ref, gc_ref)):
        x = jnp.where(is_lat, branch_refs[2 * i][...], branch_refs[2 * i + 1][...])
        g = _sigmoid(g_ref[...].astype(F32))
        term = g * _dot(x, wbf_ref[i * wd:(i + 1) * wd, :])
        acc = term if acc is None else acc + term
    o_ref[...] = acc.astype(o_ref.dtype)


def _merge(cfg, branches, p, wb, layer, bm, bn):
    m, wd, d = cfg.m_all, cfg.branch_w, cfg.d_model
    g_blk0 = 8 * wd // bn
    nd = d // bn
    n_lat_blk = cfg.m_lat // bm
    lat_spec = pl.BlockSpec((bm, wd), lambda j, i: (jnp.minimum(i, n_lat_blk - 1), 0))
    ctx_spec = pl.BlockSpec((bm, wd), lambda j, i: (jnp.maximum(i - n_lat_blk, 0), 0))
    gspec = lambda br: pl.BlockSpec((bm, bn), lambda j, i: (i, g_blk0 + br * nd + j))
    args = [a for pair in branches for a in pair]
    return pl.pallas_call(
        functools.partial(_merge_kernel, wd=wd, n_lat_blk=n_lat_blk),
        out_shape=jax.ShapeDtypeStruct((m, d), BF16),
        grid=(nd, m // bm),
        in_specs=[lat_spec, ctx_spec] * 3 + [gspec(0), gspec(1), gspec(2),
                  pl.BlockSpec((None, 3 * wd, bn), lambda j, i: (layer, 0, j))],
        out_specs=pl.BlockSpec((bm, bn), lambda j, i: (i, j)),
        scratch_shapes=[pltpu.VMEM((3 * wd, bn), BF16)],
        compiler_params=_params(2),
        name="merge_branches",
    )(*args, p, p, p, wb)


MOE_TILE = 256
MOE_BLOCK = 256


def _moe_slots(cfg):
    n_tiles = cfg.m_all * cfg.top_k // MOE_TILE + cfg.n_experts
    return n_tiles, n_tiles * MOE_TILE


def _assignment_index(t, top_k):
    per_row = LANES // top_k
    assert per_row * top_k == LANES and per_row & (per_row - 1) == 0
    shift = per_row.bit_length() - 1
    return lax.shift_right_logical(t, shift), (t & (per_row - 1)) * top_k


def _dispatch_kernel(fill_ref, nchunk_ref, nv_ref, pos_ref, xq_ref, xs_hbm, zero_ref, pos_smem, sem, idx_sem,
                     *, n_experts, top_k, n_tiles):
    i = pl.program_id(0)
    n_assign = pos_ref.shape[0] * pos_ref.shape[1]
    bt = n_assign // top_k

    @pl.when(i == 0)
    def _():
        zero_ref[...] = jnp.zeros_like(zero_ref)
        piece = zero_ref.at[pl.ds(0, SUBLANES)]
        total = 0
        for e in range(n_experts):
            def fill_piece(c, carry, e=e):
                start = pl.multiple_of(fill_ref[e] + c * SUBLANES, SUBLANES)
                pltpu.make_async_copy(piece, xs_hbm.at[pl.ds(start, SUBLANES)], sem).start()
                return carry

            lax.fori_loop(0, nchunk_ref[e], fill_piece, 0)
            total = total + nchunk_ref[e]

        def idle_tile(j, carry):
            start = pl.multiple_of(j * MOE_TILE, MOE_TILE)
            pltpu.make_async_copy(zero_ref, xs_hbm.at[pl.ds(start, MOE_TILE)], sem).start()
            return carry

        lax.fori_loop(nv_ref[0], n_tiles, idle_tile, 0)

        def drain_piece(c, carry):
            pltpu.make_async_copy(piece, xs_hbm.at[pl.ds(0, SUBLANES)], sem).wait()
            return carry

        lax.fori_loop(0, total, drain_piece, 0)

        def drain_tile(j, carry):
            pltpu.make_async_copy(zero_ref, xs_hbm.at[pl.ds(0, MOE_TILE)], sem).wait()
            return carry

        lax.fori_loop(nv_ref[0], n_tiles, drain_tile, 0)

    idx_copy = pltpu.make_async_copy(pos_ref, pos_smem, idx_sem)
    idx_copy.start()
    idx_copy.wait()

    def issue(t, carry):
        row, col = _assignment_index(t, top_k)
        for k in range(top_k):
            dst = pos_smem[row, col + k]
            pltpu.make_async_copy(xq_ref.at[pl.ds(t, 1)], xs_hbm.at[pl.ds(dst, 1)], sem).start()
        return carry

    lax.fori_loop(0, bt, issue, 0, unroll=2)
    pltpu.make_async_copy(xs_hbm.at[pl.ds(0, n_assign)], xs_hbm.at[pl.ds(0, n_assign)], sem).wait()


def _dispatch(cfg, fill, nchunk, n_valid, pos2d, xq):
    m, half = xq.shape
    n_tiles, slot_rows = _moe_slots(cfg)
    rows_per_step = MOE_BLOCK * cfg.top_k // LANES
    return pl.pallas_call(
        functools.partial(_dispatch_kernel, n_experts=cfg.n_experts, top_k=cfg.top_k, n_tiles=n_tiles),
        out_shape=jax.ShapeDtypeStruct((slot_rows, half), jnp.uint32),
        grid_spec=pltpu.PrefetchScalarGridSpec(
            num_scalar_prefetch=3,
            grid=(m // MOE_BLOCK,),
            in_specs=[pl.BlockSpec((rows_per_step, LANES), lambda i, *_: (i, 0)),
                      pl.BlockSpec((MOE_BLOCK, half), lambda i, *_: (i, 0))],
            out_specs=pl.BlockSpec(memory_space=pl.ANY),
            scratch_shapes=[pltpu.VMEM((MOE_TILE, half), jnp.uint32),
                            pltpu.SMEM((rows_per_step, LANES), jnp.int32),
                            pltpu.SemaphoreType.DMA(()), pltpu.SemaphoreType.DMA(())]),
        compiler_params=_params(1),
        name="moe_dispatch",
    )(fill, nchunk, n_valid, pos2d, xq)


def _expert_kernel(te_ref, nv_ref, xs_ref, perm_ref, wgu_ref, bgu_ref, wdn_ref, bdn_ref, ys_ref,
                   wgu_bf, wdn_bf, *, ff):
    j = pl.program_id(0)
    d = wgu_ref.shape[0]
    half = d // 2
    e = te_ref[j]
    e_prev = te_ref[jnp.maximum(j - 1, 0)]

    @pl.when((j == 0) | (e != e_prev))
    def _():
        rows = min(512, d)

        def cast_gu(c, carry):
            r0 = pl.multiple_of(c * rows, rows)
            w = wgu_ref[pl.ds(r0, rows), :].astype(BF16)
            wgu_bf[pl.ds(r0, rows), :] = _dot(w, perm_ref[...]).astype(BF16)
            return carry

        lax.fori_loop(0, d // rows, cast_gu, 0)
        rows_dn = 32

        def cast_dn(c, carry):
            r0 = pl.multiple_of(c * rows_dn, rows_dn)
            wdn_bf[pl.ds(r0, rows_dn), :] = wdn_ref[pl.ds(r0, rows_dn), :].astype(BF16)
            return carry

        lax.fori_loop(0, ff // rows_dn, cast_dn, 0)

    @pl.when(j < nv_ref[0])
    def _():
        lo, hi = _unpack_pair(xs_ref[...])
        h = (_dot(lo.astype(BF16), wgu_bf[pl.ds(0, half), :]) + _dot(hi.astype(BF16), wgu_bf[pl.ds(half, half), :])
             + bgu_ref[...])
        h_glu = jnp.minimum(h[:, :ff], SWIGLU_LIMIT)
        h_lin = jnp.clip(h[:, ff:], -SWIGLU_LIMIT, SWIGLU_LIMIT)
        act = (h_glu * _sigmoid(SWIGLU_ALPHA * h_glu) * (h_lin + 1.0)).astype(BF16)
        cw = min(512, half)
        for c in range(half // cw):
            lo_sl = slice(c * cw, (c + 1) * cw)
            hi_sl = slice(half + c * cw, half + (c + 1) * cw)
            y_lo = _dot(act, wdn_bf[:, lo_sl]) + bdn_ref[:, lo_sl]
            y_hi = _dot(act, wdn_bf[:, hi_sl]) + bdn_ref[:, hi_sl]
            ys_ref[:, lo_sl] = _pack_pair(y_lo, y_hi)

    @pl.when(j >= nv_ref[0])
    def _():
        ys_ref[...] = jnp.zeros_like(ys_ref)


def _experts(cfg, tile_e, n_valid, xs, perm, w_gu, b_gu_perm, w_dn, b_dn, layer):
    d, ff, ne = cfg.d_model, cfg.expert_ff, cfg.n_experts
    half = d // 2
    n_tiles, slot_rows = _moe_slots(cfg)
    tile_blk = lambda j, te, nv: (jnp.minimum(j, nv[0] - 1), 0)
    by_expert = lambda j, te, nv: (te[j], 0, 0)
    by_layer_expert = lambda j, te, nv: (layer, te[j], 0, 0)
    return pl.pallas_call(
        functools.partial(_expert_kernel, ff=ff),
        out_shape=jax.ShapeDtypeStruct((slot_rows, half), jnp.uint32),
        grid_spec=pltpu.PrefetchScalarGridSpec(
            num_scalar_prefetch=2,
            grid=(n_tiles,),
            in_specs=[
                pl.BlockSpec((MOE_TILE, half), tile_blk),
                pl.BlockSpec((2 * ff, 2 * ff), lambda j, te, nv: (0, 0)),
                pl.BlockSpec((None, None, d, 2 * ff), by_layer_expert),
                pl.BlockSpec((None, 1, 2 * ff), by_expert),
                pl.BlockSpec((None, None, ff, d), by_layer_expert),
                pl.BlockSpec((None, 1, d), by_expert),
            ],
            out_specs=pl.BlockSpec((MOE_TILE, half), lambda j, te, nv: (j, 0)),
            scratch_shapes=[pltpu.VMEM((d, 2 * ff), BF16), pltpu.VMEM((ff, d), BF16)]),
        compiler_params=_params(1),
        name="moe_experts",
    )(tile_e, n_valid, xs, perm, w_gu, b_gu_perm.reshape(ne, 1, 2 * ff), w_dn, b_dn.reshape(ne, 1, d))


def _combine_kernel(pos_ref, route_ref, res_ref, mod_ref, ys_hbm, o_ref, buf_ref, pos_smem, sem, idx_sem,
                    *, gate, top_k):
    bt, d = res_ref.shape
    half = d // 2

    idx_copy = pltpu.make_async_copy(pos_ref, pos_smem, idx_sem)
    idx_copy.start()
    idx_copy.wait()

    def issue(t, carry):
        row, col = _assignment_index(t, top_k)
        for k in range(top_k):
            src = pos_smem[row, col + k]
            pltpu.make_async_copy(ys_hbm.at[pl.ds(src, 1)], buf_ref.at[k, pl.ds(t, 1)], sem).start()
        return carry

    lax.fori_loop(0, bt, issue, 0, unroll=2)
    pltpu.make_async_copy(buf_ref, buf_ref, sem).wait()

    rb = 2 * SUBLANES
    cw = min(512, half)

    def rows(r, carry):
        r0 = pl.multiple_of(r * rb, rb)
        rs = pl.ds(r0, rb)
        weights = [route_ref[rs, top_k + k:top_k + k + 1] for k in range(top_k)]
        for c in range(half // cw):
            lo_sl = slice(c * cw, (c + 1) * cw)
            hi_sl = slice(half + c * cw, half + (c + 1) * cw)
            acc_lo = acc_hi = None
            for k in range(top_k):
                lo, hi = _unpack_pair(buf_ref[k, rs, lo_sl])
                acc_lo = weights[k] * lo if acc_lo is None else acc_lo + weights[k] * lo
                acc_hi = weights[k] * hi if acc_hi is None else acc_hi + weights[k] * hi
            o_ref[rs, lo_sl] = res_ref[rs, lo_sl] + mod_ref[0, gate:gate + 1, lo_sl] * acc_lo
            o_ref[rs, hi_sl] = res_ref[rs, hi_sl] + mod_ref[0, gate:gate + 1, hi_sl] * acc_hi
        return carry

    lax.fori_loop(0, bt // rb, rows, 0)


def _combine(cfg, pos2d, route, res, mod, ys, gate):
    m, d = res.shape
    half = d // 2
    ep = route.shape[1]
    bt = MOE_BLOCK
    rows_per_step = bt * cfg.top_k // LANES
    return pl.pallas_call(
        functools.partial(_combine_kernel, gate=gate, top_k=cfg.top_k),
        out_shape=jax.ShapeDtypeStruct((m, d), F32),
        grid=(m // bt,),
        in_specs=[
            pl.BlockSpec((rows_per_step, LANES), lambda i: (i, 0)),
            pl.BlockSpec((bt, ep), lambda i: (i, 0)),
            pl.BlockSpec((bt, d), lambda i: (i, 0)),
            pl.BlockSpec((1, N_MOD, d), lambda i: (_seg_of_block(cfg, i, bt), 0, 0)),
            pl.BlockSpec(memory_space=pl.ANY),
        ],
        out_specs=pl.BlockSpec((bt, d), lambda i: (i, 0)),
        scratch_shapes=[pltpu.VMEM((cfg.top_k, bt, half), jnp.uint32),
                        pltpu.SMEM((rows_per_step, LANES), jnp.int32),
                        pltpu.SemaphoreType.DMA(()), pltpu.SemaphoreType.DMA(())],
        compiler_params=_params(1),
        name="moe_combine",
    )(pos2d, route, res, mod, ys)


def _moe_plan(cfg, route, counts8):
    k, ne = cfg.top_k, cfg.n_experts
    n_tiles, _ = _moe_slots(cfg)
    expert = route[:, :k].astype(jnp.int32)
    rank = route[:, 2 * k:3 * k].astype(jnp.int32)
    counts = counts8[0, :ne].astype(jnp.int32)
    padded = (counts + MOE_TILE - 1) // MOE_TILE * MOE_TILE
    region_end = jnp.cumsum(padded)
    region_start = region_end - padded
    pos = region_start[expert] + rank
    fill = (region_start + counts) // SUBLANES * SUBLANES
    nchunk = (region_end - fill) // SUBLANES
    n_valid = region_end[-1] // MOE_TILE
    tile_start = jnp.arange(n_tiles, dtype=jnp.int32) * MOE_TILE
    tile_e = jnp.sum(region_end[None, :] <= tile_start[:, None], axis=1)
    tile_e = jnp.minimum(tile_e, ne - 1).astype(jnp.int32)
    tile_e = tile_e[jnp.minimum(jnp.arange(n_tiles), n_valid - 1)]
    return (pos.reshape(-1, LANES), fill.astype(jnp.int32), nchunk.astype(jnp.int32), tile_e,
            n_valid.reshape(1).astype(jnp.int32))


def _deinterleave_perm(ff):
    col = jnp.arange(2 * ff)
    src = jnp.where(col < ff, 2 * col, 2 * (col - ff) + 1)
    return (jnp.arange(2 * ff)[:, None] == src[None, :]).astype(BF16)


def _forward(cfg, x, c, ctx, c_ctx, ada_a, ada_b, ada_bias, norm1_g, norm2_g, w_in, conv_a_w, lam_qk,
             subln_g, lru_conv_w, lru_conv_b, lru_w_a, lru_b_a, lru_w_x, lru_b_x, lru_lambda,
             w_branch, w_out, router_w, router_b, w_gu, b_gu, w_dn, b_dn, final_g):
    d, wd, bsz, s, lc = cfg.d_model, cfg.branch_w, cfg.batch, cfg.seq, cfg.ctx_len
    m_lat, m_all, ne = cfg.m_lat, cfg.m_all, cfg.n_experts
    assert bsz * lc <= s and s % lc == 0 and bsz + 1 <= SUBLANES
    bm = min(512, bsz * lc)
    bn = min(1024, d)
    bt = min(256, lc)
    ep = -(-ne // LANES) * LANES

    xs = jnp.concatenate([x.reshape(m_lat, d), ctx.reshape(bsz * lc, d)], axis=0)
    cond8 = jnp.zeros((SUBLANES, d), F32).at[:bsz].set(c).at[bsz].set(c_ctx)
    ctab, stab = _rope_tables(cfg, bt)
    ctx_blk = lambda b: m_lat // lc + b
    perm = _deinterleave_perm(cfg.expert_ff)

    for l in range(cfg.depth):
        lam_init = LAMBDA_INIT_BASE - LAMBDA_INIT_AMP * math.exp(-LAMBDA_INIT_RATE * l)
        mod = _ada(cfg, cond8, ada_a[l], ada_b[l], ada_bias[l])

        xn = _norm_mod(cfg, xs, norm1_g[l], mod, 0, 1)
        p = _matmul(xn, w_in, l, bm, bn)
        oa = (_conv_mixer(cfg, p, conv_a_w[l], s, 0, bsz, m_lat),
              _conv_mixer(cfg, p, conv_a_w[l], lc, m_lat // lc, bsz, bsz * lc))
        qt, kr, vt = _att_prep(cfg, p, ctab, stab, bt)
        nh = wd // ATT_VD
        ctx_seg = (p, ctx_blk, 4 * nh, vt, lc)
        lat_seg = (kr, lambda b: b, 0, vt, s)
        ob_lat = _attention(cfg, lam_qk[l], subln_g[l], qt, [ctx_seg, lat_seg], 0, s // bt, m_lat, lam_init, bq=bt)
        ob_ctx = _attention(cfg, lam_qk[l], subln_g[l], qt, [ctx_seg], m_lat // bt, lc // bt, bsz * lc, lam_init, bq=bt)
        oc = _lru(cfg, p, lru_conv_w[l], lru_conv_b[l], lru_w_a[l], lru_b_a[l],
                  lru_w_x[l], lru_b_x[l], lru_lambda[l])
        y = _merge(cfg, (oa, (ob_lat, ob_ctx), oc), p, w_branch, l, bm, bn)
        xs = _matmul_residual(cfg, y, w_out, l, xs, mod, 2, bm, bn)

        wr_pad = jnp.zeros((d, ep), F32).at[:, :ne].set(router_w[l])
        br_pad = jnp.full((1, ep), -jnp.inf, F32).at[0, :ne].set(router_b[l])
        xq, route, counts8 = _norm_router(cfg, xs, norm2_g[l], mod, 3, 4, wr_pad, br_pad)
        pos2d, fill, nchunk, tile_e, n_valid = _moe_plan(cfg, route, counts8)
        slots = _dispatch(cfg, fill, nchunk, n_valid, pos2d, xq)
        b_gu_perm = jnp.concatenate([b_gu[l][:, 0::2], b_gu[l][:, 1::2]], axis=-1)
        ys = _experts(cfg, tile_e, n_valid, slots, perm, w_gu, b_gu_perm, w_dn, b_dn[l], l)
        xs = _combine(cfg, pos2d, route, xs, mod, ys, 5)

    return _norm_plain(xs, final_g, m_lat).reshape(bsz, s, d)


def kernel(x, c, ctx, c_ctx, ada_a, ada_b, ada_bias, norm1_g, norm2_g, w_in, conv_a_w, lam_qk, subln_g, lru_conv_w, lru_conv_b, lru_w_a, lru_b_a, lru_w_x, lru_b_x, lru_lambda, w_branch, w_out, router_w, router_b, w_gu, b_gu, w_dn, b_dn, final_g):
    return _forward(_Cfg(), x, c, ctx, c_ctx, ada_a, ada_b, ada_bias, norm1_g, norm2_g, w_in, conv_a_w, lam_qk,
                    subln_g, lru_conv_w, lru_conv_b, lru_w_a, lru_b_a, lru_w_x, lru_b_x, lru_lambda,
                    w_branch, w_out, router_w, router_b, w_gu, b_gu, w_dn, b_dn, final_g)
```

```python
import functools
import math
from typing import NamedTuple

import jax
import jax.numpy as jnp
from jax import lax
from jax.experimental import pallas as pl
from jax.experimental.pallas import tpu as pltpu

F32 = jnp.float32
BF16 = jnp.bfloat16

LANES = 128
SUBLANES = 8
EPS = 1e-6
ATT_HD = 64
ATT_VD = 2 * ATT_HD
ATT_UNROLL = 8
ATT_LATENT_BQ = 512
ATT_Q_SCALE = ATT_HD ** -0.5 * math.log2(math.e)
ROPE_THETA = 10000.0
ROPE_AXIS_DIM = ATT_HD // 2
ROPE_PAIRS = ROPE_AXIS_DIM // 2
LAMBDA_INIT_BASE = 0.8
LAMBDA_INIT_AMP = 0.6
LAMBDA_INIT_RATE = 0.3
LRU_C = 8.0
SWIGLU_ALPHA = 1.702
SWIGLU_LIMIT = 7.0
N_MOD = 6
VMEM_LIMIT = 56 * 1024 * 1024


class _Cfg(NamedTuple):
    d_model: int = 4096
    batch: int = 2
    seq: int = 4096
    depth: int = 4
    grid_w: int = 64
    ctx_len: int = 256
    branch_w: int = 1024
    n_experts: int = 32
    top_k: int = 4
    expert_ff: int = 256

    @property
    def heads(self):
        return self.branch_w // ATT_VD

    @property
    def m_lat(self):
        return self.batch * self.seq

    @property
    def m_all(self):
        return self.batch * (self.seq + self.ctx_len)

    @property
    def in_cols(self):
        return 8 * self.branch_w + 3 * self.d_model


def _params(n_grid_dims):
    return pltpu.CompilerParams(
        dimension_semantics=("arbitrary",) * n_grid_dims, vmem_limit_bytes=VMEM_LIMIT)


def _dot(a, b):
    return jnp.dot(a, b, preferred_element_type=F32)


def _split_bf16(x):
    hi = x.astype(BF16)
    lo = (x - hi.astype(F32)).astype(BF16)
    return hi, lo


def _dot3(x, w):
    xh, xl = _split_bf16(x)
    wh, wl = _split_bf16(w)
    return _dot(xh, wh) + (_dot(xl, wh) + _dot(xh, wl))


def _sigmoid(x):
    return 1.0 / (1.0 + jnp.exp(-x))


def _ada_kernel(cond_ref, a_ref, b_ref, bias_ref, o_ref):
    cond = cond_ref[...]
    t = _dot3(cond * _sigmoid(cond), a_ref[...])
    o_ref[...] = _dot3(t, b_ref[...]) + bias_ref[...]


def _ada(cfg, cond8, a, b, bias):
    d = cfg.d_model
    rank = a.shape[1]
    out = pl.pallas_call(
        _ada_kernel,
        out_shape=jax.ShapeDtypeStruct((SUBLANES, N_MOD * d), F32),
        grid=(N_MOD,),
        in_specs=[
            pl.BlockSpec((SUBLANES, d), lambda j: (0, 0)),
            pl.BlockSpec((d, rank), lambda j: (0, 0)),
            pl.BlockSpec((rank, d), lambda j: (0, j)),
            pl.BlockSpec((1, d), lambda j: (0, j)),
        ],
        out_specs=pl.BlockSpec((SUBLANES, d), lambda j: (0, j)),
        compiler_params=_params(1),
        name="ada_modulation",
    )(cond8, a, b, bias.reshape(1, N_MOD * d))
    return out.reshape(SUBLANES, N_MOD, d)[: cfg.batch + 1]


def _rms(x, g):
    return x * lax.rsqrt(jnp.mean(x * x, axis=-1, keepdims=True) + EPS) * g


def _norm_mod_kernel(x_ref, g_ref, mod_ref, o_ref, *, shift, scale):
    y = _rms(x_ref[...], g_ref[...])
    y = y * (1.0 + mod_ref[0, scale:scale + 1, :]) + mod_ref[0, shift:shift + 1, :]
    o_ref[...] = y.astype(o_ref.dtype)


def _norm_plain_kernel(x_ref, g_ref, o_ref):
    o_ref[...] = _rms(x_ref[...], g_ref[...]).astype(o_ref.dtype)


def _pack_pair(lo, hi):
    lo_bits = lax.bitcast_convert_type(lo.astype(BF16).astype(F32), jnp.uint32)
    hi_bits = lax.bitcast_convert_type(hi.astype(BF16).astype(F32), jnp.uint32)
    return (lo_bits >> 16) | (hi_bits & jnp.uint32(0xFFFF0000))


def _unpack_pair(word):
    lo = lax.bitcast_convert_type(word << 16, F32)
    hi = lax.bitcast_convert_type(word & jnp.uint32(0xFFFF0000), F32)
    return lo, hi


def _norm_router_kernel(x_ref, g_ref, mod_ref, wr_ref, br_ref, xq_ref, route_ref, counts_ref, carry_ref,
                        *, shift, scale, top_k):
    @pl.when(pl.program_id(0) == 0)
    def _():
        carry_ref[...] = jnp.zeros_like(carry_ref)

    y = _rms(x_ref[...], g_ref[...])
    y = y * (1.0 + mod_ref[0, scale:scale + 1, :]) + mod_ref[0, shift:shift + 1, :]
    half = y.shape[1] // 2
    xq_ref[...] = _pack_pair(y[:, :half], y[:, half:])
    work = _dot3(y, wr_ref[...]) + br_ref[...]
    bm, n_lane = work.shape
    lane = lax.broadcasted_iota(jnp.int32, work.shape, 1).astype(F32)
    vals, hots, ids = [], [], []
    for _ in range(top_k):
        m = jnp.max(work, axis=-1, keepdims=True)
        idx = jnp.min(jnp.where(work == m, lane, float(n_lane)), axis=-1, keepdims=True)
        hot = lane == idx
        vals.append(m)
        hots.append(hot)
        ids.append(idx)
        work = jnp.where(hot, -jnp.inf, work)
    exps = [jnp.exp(v - vals[0]) for v in vals]
    denom = exps[0]
    for e in exps[1:]:
        denom = denom + e
    picked = jnp.zeros(work.shape, F32)
    for hot in hots:
        picked = jnp.where(hot, 1.0, picked)
    tri = (lax.broadcasted_iota(jnp.int32, (bm, bm), 0) > lax.broadcasted_iota(jnp.int32, (bm, bm), 1))
    before = _dot(jnp.where(tri, 1.0, 0.0).astype(BF16), picked.astype(BF16)) + carry_ref[0:1, :]
    carry_ref[...] = carry_ref[...] + jnp.sum(picked, axis=0, keepdims=True)
    counts_ref[...] = carry_ref[...]
    route = jnp.zeros(work.shape, F32)
    for k in range(top_k):
        rank = jnp.sum(jnp.where(hots[k], before, 0.0), axis=-1, keepdims=True)
        route = jnp.where(lane == float(k), ids[k], route)
        route = jnp.where(lane == float(top_k + k), exps[k] / denom, route)
        route = jnp.where(lane == float(2 * top_k + k), rank, route)
    route_ref[...] = route


def _seg_of_block(cfg, i, bm):
    return (i * bm) // cfg.seq


def _norm_mod(cfg, x, g, mod, shift, scale, bm=256):
    m, d = x.shape
    return pl.pallas_call(
        functools.partial(_norm_mod_kernel, shift=shift, scale=scale),
        out_shape=jax.ShapeDtypeStruct((m, d), BF16),
        grid=(m // bm,),
        in_specs=[
            pl.BlockSpec((bm, d), lambda i: (i, 0)),
            pl.BlockSpec((1, d), lambda i: (0, 0)),
            pl.BlockSpec((1, N_MOD, d), lambda i: (_seg_of_block(cfg, i, bm), 0, 0)),
        ],
        out_specs=pl.BlockSpec((bm, d), lambda i: (i, 0)),
        compiler_params=_params(1),
        name="norm_mod",
    )(x, g.reshape(1, d), mod)


def _norm_plain(x, g, m_rows, bm=256):
    d = x.shape[1]
    return pl.pallas_call(
        _norm_plain_kernel,
        out_shape=jax.ShapeDtypeStruct((m_rows, d), F32),
        grid=(m_rows // bm,),
        in_specs=[
            pl.BlockSpec((bm, d), lambda i: (i, 0)),
            pl.BlockSpec((1, d), lambda i: (0, 0)),
        ],
        out_specs=pl.BlockSpec((bm, d), lambda i: (i, 0)),
        compiler_params=_params(1),
        name="final_norm",
    )(x, g.reshape(1, d))


def _norm_router(cfg, x, g, mod, shift, scale, wr_pad, br_pad, bm=256):
    m, d = x.shape
    ep = wr_pad.shape[1]
    return pl.pallas_call(
        functools.partial(_norm_router_kernel, shift=shift, scale=scale, top_k=cfg.top_k),
        out_shape=(jax.ShapeDtypeStruct((m, d // 2), jnp.uint32), jax.ShapeDtypeStruct((m, ep), F32),
                   jax.ShapeDtypeStruct((SUBLANES, ep), F32)),
        grid=(m // bm,),
        in_specs=[
            pl.BlockSpec((bm, d), lambda i: (i, 0)),
            pl.BlockSpec((1, d), lambda i: (0, 0)),
            pl.BlockSpec((1, N_MOD, d), lambda i: (_seg_of_block(cfg, i, bm), 0, 0)),
            pl.BlockSpec((d, ep), lambda i: (0, 0)),
            pl.BlockSpec((1, ep), lambda i: (0, 0)),
        ],
        out_specs=(pl.BlockSpec((bm, d // 2), lambda i: (i, 0)), pl.BlockSpec((bm, ep), lambda i: (i, 0)),
                   pl.BlockSpec((SUBLANES, ep), lambda i: (0, 0))),
        scratch_shapes=[pltpu.VMEM((SUBLANES, ep), F32)],
        compiler_params=_params(1),
        name="norm_router",
    )(x, g.reshape(1, d), mod, wr_pad, br_pad)


CAST_ROWS = 256


def _cast_weight_block(w_ref, wbf_ref):
    @pl.when(pl.program_id(1) == 0)
    def _():
        rows = min(CAST_ROWS, w_ref.shape[0])

        def body(c, carry):
            r0 = pl.multiple_of(c * rows, rows)
            wbf_ref[pl.ds(r0, rows), :] = w_ref[pl.ds(r0, rows), :].astype(BF16)
            return carry

        lax.fori_loop(0, w_ref.shape[0] // rows, body, 0)


def _mm_kernel(x_ref, w_ref, o_ref, wbf_ref):
    _cast_weight_block(w_ref, wbf_ref)
    o_ref[...] = _dot(x_ref[...], wbf_ref[...]).astype(o_ref.dtype)


def _mm_res_kernel(x_ref, w_ref, res_ref, mod_ref, o_ref, wbf_ref, *, gate):
    _cast_weight_block(w_ref, wbf_ref)
    o_ref[...] = res_ref[...] + mod_ref[0, gate:gate + 1, :] * _dot(x_ref[...], wbf_ref[...])


def _matmul(x, w, layer, bm, bn, out_dtype=BF16):
    m, k = x.shape
    n = w.shape[2]
    return pl.pallas_call(
        _mm_kernel,
        out_shape=jax.ShapeDtypeStruct((m, n), out_dtype),
        grid=(n // bn, m // bm),
        in_specs=[
            pl.BlockSpec((bm, k), lambda j, i: (i, 0)),
            pl.BlockSpec((None, k, bn), lambda j, i: (layer, 0, j)),
        ],
        out_specs=pl.BlockSpec((bm, bn), lambda j, i: (i, j)),
        scratch_shapes=[pltpu.VMEM((k, bn), BF16)],
        compiler_params=_params(2),
        name="matmul",
    )(x, w)


def _matmul_residual(cfg, x, w, layer, res, mod, gate, bm, bn):
    m, k = x.shape
    n = w.shape[2]
    return pl.pallas_call(
        functools.partial(_mm_res_kernel, gate=gate),
        out_shape=jax.ShapeDtypeStruct((m, n), F32),
        grid=(n // bn, m // bm),
        in_specs=[
            pl.BlockSpec((bm, k), lambda j, i: (i, 0)),
            pl.BlockSpec((None, k, bn), lambda j, i: (layer, 0, j), pipeline_mode=pl.Buffered(1)),
            pl.BlockSpec((bm, bn), lambda j, i: (i, j)),
            pl.BlockSpec((1, N_MOD, bn), lambda j, i: (_seg_of_block(cfg, i, bm), 0, j)),
        ],
        out_specs=pl.BlockSpec((bm, bn), lambda j, i: (i, j)),
        scratch_shapes=[pltpu.VMEM((k, bn), BF16)],
        compiler_params=_params(2),
        name="matmul_residual",
    )(x, w, res, mod)


def _conv_a_kernel(b_ref, c_ref, h_ref, w_ref, o_ref, *, tt):
    t_len = b_ref.shape[0]
    w = w_ref[...]
    halo = 16

    def prod(r0, n):
        return c_ref[pl.ds(r0, n), :].astype(F32) * h_ref[pl.ds(r0, n), :].astype(F32)

    def chunk(ci, carry):
        r0 = pl.multiple_of(ci * tt, tt)
        u = prod(r0, tt)
        prev = prod(pl.multiple_of(jnp.maximum(r0 - halo, 0), halo), halo)[halo - 1:halo, :]
        prev = jnp.where(r0 == 0, 0.0, prev)
        nxt = prod(pl.multiple_of(jnp.minimum(r0 + tt, t_len - halo), halo), halo)[0:1, :]
        nxt = jnp.where(r0 + tt == t_len, 0.0, nxt)
        row = lax.broadcasted_iota(jnp.int32, u.shape, 0)
        up = jnp.where(row == 0, prev, pltpu.roll(u, 1, 0))
        dn = jnp.where(row == tt - 1, nxt, pltpu.roll(u, tt - 1, 0))
        conv = w[0:1, :] * up + w[1:2, :] * u + w[2:3, :] * dn
        o_ref[pl.ds(r0, tt), :] = (b_ref[pl.ds(r0, tt), :].astype(F32) * conv).astype(o_ref.dtype)
        return carry

    lax.fori_loop(0, t_len // tt, chunk, 0)


def _conv_mixer(cfg, p, w, t_len, row_blk0, n_seq, out_rows, bc=256):
    wd = cfg.branch_w
    ncb = wd // bc
    tt = min(256, t_len)
    spec = lambda off: pl.BlockSpec((t_len, bc), lambda s, j: (row_blk0 + s, off * ncb + j))
    return pl.pallas_call(
        functools.partial(_conv_a_kernel, tt=tt),
        out_shape=jax.ShapeDtypeStruct((out_rows, wd), BF16),
        grid=(n_seq, ncb),
        in_specs=[spec(0), spec(1), spec(2), pl.BlockSpec((3, bc), lambda s, j: (0, j))],
        out_specs=pl.BlockSpec((t_len, bc), lambda s, j: (s, j)),
        compiler_params=_params(2),
        name="conv_mixer",
    )(p, p, p, w)


def _rope_tables(cfg, bt):
    t = jnp.arange(cfg.seq)
    pos = jnp.stack([t // cfg.grid_w, t % cfg.grid_w], axis=-1).astype(F32)
    inv = ROPE_THETA ** (-jnp.arange(ROPE_PAIRS, dtype=F32) * 2.0 / ROPE_AXIS_DIM)
    ang = pos[:, :, None] * inv
    cos, sin = jnp.cos(ang), jnp.sin(ang)
    cmap = jnp.concatenate([cos[:, 0], cos[:, 0], cos[:, 1], cos[:, 1]], axis=-1)
    smap = jnp.concatenate([-sin[:, 0], sin[:, 0], -sin[:, 1], sin[:, 1]], axis=-1)
    ctab = jnp.concatenate([jnp.tile(cmap, (1, 2)), jnp.ones((bt, ATT_VD), F32)], axis=0)
    stab = jnp.concatenate([jnp.tile(smap, (1, 2)), jnp.zeros((bt, ATT_VD), F32)], axis=0)
    return ctab, stab


def _att_prep_kernel(q_ref, k_ref, v_ref, c_ref, s_ref, qt_ref, kr_ref, vt_ref, *, heads):
    cos = c_ref[...]
    sin = s_ref[...]
    lane = lax.broadcasted_iota(jnp.int32, cos.shape, 1)
    low = (lane % ROPE_AXIS_DIM) < ROPE_PAIRS

    def rope(u):
        swapped = jnp.where(low, pltpu.roll(u, ATT_VD - ROPE_PAIRS, 1), pltpu.roll(u, ROPE_PAIRS, 1))
        return u * cos + swapped * sin

    for h in range(heads):
        sl = slice(h * ATT_VD, (h + 1) * ATT_VD)
        qt_ref[sl, :] = (rope(q_ref[:, sl].astype(F32)) * ATT_Q_SCALE).T.astype(qt_ref.dtype)
        kr_ref[:, sl] = rope(k_ref[:, sl].astype(F32)).astype(kr_ref.dtype)
        vt_ref[sl, :] = v_ref[:, sl].astype(F32).T.astype(vt_ref.dtype)


def _att_prep(cfg, p, ctab, stab, bt=256):
    m, wd = cfg.m_all, cfg.branch_w
    n_lat_blk = cfg.m_lat // bt
    n_seq_blk = cfg.seq // bt
    tab_map = lambda i: (jnp.where(i < n_lat_blk, i % n_seq_blk, n_seq_blk), 0)
    return pl.pallas_call(
        functools.partial(_att_prep_kernel, heads=cfg.heads),
        out_shape=(jax.ShapeDtypeStruct((wd, m), BF16), jax.ShapeDtypeStruct((m, wd), BF16),
                   jax.ShapeDtypeStruct((wd, m), BF16)),
        grid=(m // bt,),
        in_specs=[
            pl.BlockSpec((bt, wd), lambda i: (i, 3)),
            pl.BlockSpec((bt, wd), lambda i: (i, 4)),
            pl.BlockSpec((bt, wd), lambda i: (i, 5)),
            pl.BlockSpec((bt, ATT_VD), tab_map),
            pl.BlockSpec((bt, ATT_VD), tab_map),
        ],
        out_specs=(pl.BlockSpec((wd, bt), lambda i: (0, i)), pl.BlockSpec((bt, wd), lambda i: (i, 0)),
                   pl.BlockSpec((wd, bt), lambda i: (0, i))),
        compiler_params=_params(1),
        name="att_prep",
    )(p, p, p, ctab, stab)


def _attn_kernel(lam_ref, g_ref, qt_ref, *refs, seg_lens, tk, lam_init):
    n_seg = len(seg_lens)
    kv = refs[:2 * n_seg]
    o_ref = refs[2 * n_seg]
    s1_ref, s2_ref, acc1_ref, acc2_ref = refs[2 * n_seg + 1:]
    bq = qt_ref.shape[1]

    qt = qt_ref[...].astype(F32)
    sub = lax.broadcasted_iota(jnp.int32, qt.shape, 0)
    q1 = jnp.where(sub < ATT_HD, qt, 0.0).astype(qt_ref.dtype)
    q2 = jnp.where(sub >= ATT_HD, qt, 0.0).astype(qt_ref.dtype)

    def chunks(seg):
        tkk = min(tk, seg_lens[seg])
        return tkk, seg_lens[seg] // tkk

    m1 = jnp.full((1, bq), -jnp.inf, F32)
    m2 = jnp.full((1, bq), -jnp.inf, F32)
    off = 0
    for seg in range(n_seg):
        k_ref = kv[2 * seg]
        tkk, n_chunk = chunks(seg)

        def body_a(c, carry, k_ref=k_ref, tkk=tkk, off=off):
            m1, m2 = carry
            r0 = pl.multiple_of(c * tkk, tkk)
            kc = k_ref[pl.ds(r0, tkk), :]
            s1 = _dot(kc, q1)
            s2 = _dot(kc, q2)
            s1_ref[pl.ds(off + r0, tkk), :] = s1
            s2_ref[pl.ds(off + r0, tkk), :] = s2
            return (jnp.maximum(m1, jnp.max(s1, axis=0, keepdims=True)),
                    jnp.maximum(m2, jnp.max(s2, axis=0, keepdims=True)))

        m1, m2 = lax.fori_loop(0, n_chunk, body_a, (m1, m2), unroll=min(ATT_UNROLL, n_chunk))
        off += seg_lens[seg]

    acc1_ref[...] = jnp.zeros_like(acc1_ref)
    acc2_ref[...] = jnp.zeros_like(acc2_ref)
    l1 = jnp.zeros((1, bq), F32)
    l2 = jnp.zeros((1, bq), F32)
    off = 0
    for seg in range(n_seg):
        vt_ref = kv[2 * seg + 1]
        tkk, n_chunk = chunks(seg)

        def body_b(c, carry, vt_ref=vt_ref, tkk=tkk, off=off):
            l1, l2 = carry
            r0 = pl.multiple_of(c * tkk, tkk)
            p1 = jnp.exp2(s1_ref[pl.ds(off + r0, tkk), :] - m1)
            p2 = jnp.exp2(s2_ref[pl.ds(off + r0, tkk), :] - m2)
            vt = vt_ref[:, pl.ds(r0, tkk)]
            acc1_ref[...] += _dot(vt, p1.astype(vt.dtype))
            acc2_ref[...] += _dot(vt, p2.astype(vt.dtype))
            return (l1 + jnp.sum(p1, axis=0, keepdims=True), l2 + jnp.sum(p2, axis=0, keepdims=True))

        l1, l2 = lax.fori_loop(0, n_chunk, body_b, (l1, l2), unroll=min(ATT_UNROLL, n_chunk))
        off += seg_lens[seg]

    lq = lam_ref[...]
    lam = (jnp.exp(jnp.sum(lq[0:1] * lq[1:2], axis=-1, keepdims=True))
           - jnp.exp(jnp.sum(lq[2:3] * lq[3:4], axis=-1, keepdims=True)) + lam_init)
    ot = acc1_ref[...] * (1.0 / l1) - acc2_ref[...] * (lam / l2)
    ms = jnp.mean(ot * ot, axis=0, keepdims=True)
    y = ot * lax.rsqrt(ms + EPS) * g_ref[...] * (1.0 - lam_init)
    o_ref[...] = y.T.astype(o_ref.dtype)


def _attention(cfg, lam_qk, subln_g, qt, segs, q_col_blk0, n_q_per_batch, out_rows, lam_init, bq=256, tk=512):
    wd = cfg.branch_w
    in_specs = [
        pl.BlockSpec((4, ATT_HD), lambda b, h, i: (0, 0)),
        pl.BlockSpec((ATT_VD, 1), lambda b, h, i: (0, 0)),
        pl.BlockSpec((ATT_VD, bq), lambda b, h, i: (h, q_col_blk0 + b * n_q_per_batch + i)),
    ]
    args = [lam_qk, subln_g.reshape(ATT_VD, 1), qt]
    seg_lens = []
    for k_arr, k_blk, k_col0, vt_arr, length in segs:
        in_specs.append(pl.BlockSpec((length, ATT_VD), lambda b, h, i, f=k_blk, c0=k_col0: (f(b), c0 + h)))
        in_specs.append(pl.BlockSpec((ATT_VD, length), lambda b, h, i, f=k_blk: (h, f(b))))
        args += [k_arr, vt_arr]
        seg_lens.append(length)
    l_tot = sum(seg_lens)
    return pl.pallas_call(
        functools.partial(_attn_kernel, seg_lens=tuple(seg_lens), tk=tk, lam_init=lam_init),
        out_shape=jax.ShapeDtypeStruct((out_rows, wd), BF16),
        grid=(cfg.batch, cfg.heads, n_q_per_batch),
        in_specs=in_specs,
        out_specs=pl.BlockSpec((bq, ATT_VD), lambda b, h, i: (b * n_q_per_batch + i, h)),
        scratch_shapes=[pltpu.VMEM((l_tot, bq), F32), pltpu.VMEM((l_tot, bq), F32),
                        pltpu.VMEM((ATT_VD, bq), F32), pltpu.VMEM((ATT_VD, bq), F32)],
        compiler_params=_params(3),
        name="diff_attention",
    )(*args)


SCAN_PAD = SUBLANES


def _scan_levels(n):
    plan = []
    while n > 1:
        fan = SUBLANES if n % SUBLANES == 0 else n
        assert fan <= SUBLANES, "sequence length must factor into groups of at most 8"
        plan.append((n, fan, n // fan))
        n //= fan
    return plan


def _scan_scratch_rows(n):
    rows = 0
    for _, _, parts in _scan_levels(n):
        rows += 2 * SCAN_PAD + -(-parts // SUBLANES) * SUBLANES
    return max(rows, SUBLANES)


def _scan_inplace(a_ref, b_ref, base, n, rev, sa_ref, sb_ref, off=0):
    if n == 1:
        return
    fan = SUBLANES if n % SUBLANES == 0 else n
    parts = n // fan
    order = list(range(fan))
    if rev:
        order.reverse()

    def rows(r):
        return pl.ds(base + r, parts, stride=fan) if parts > 1 else pl.ds(base + r, 1)

    acc_a = acc_b = None
    for r in order:
        a_r = a_ref[rows(r), :]
        b_r = b_ref[rows(r), :]
        if acc_a is None:
            acc_a, acc_b = a_r, b_r
        else:
            acc_b = a_r * acc_b + b_r
            acc_a = a_r * acc_a
            a_ref[rows(r), :] = acc_a
            b_ref[rows(r), :] = acc_b
    if parts == 1:
        return
    data = off + SCAN_PAD
    ones = jnp.ones((SCAN_PAD, LANES), F32)
    zeros = jnp.zeros((SCAN_PAD, LANES), F32)
    sa_ref[pl.ds(off, SCAN_PAD), :] = ones
    sb_ref[pl.ds(off, SCAN_PAD), :] = zeros
    sa_ref[pl.ds(data + parts, SCAN_PAD), :] = ones
    sb_ref[pl.ds(data + parts, SCAN_PAD), :] = zeros
    sa_ref[pl.ds(data, parts), :] = acc_a
    sb_ref[pl.ds(data, parts), :] = acc_b
    nxt = off + 2 * SCAN_PAD + -(-parts // SUBLANES) * SUBLANES
    _scan_inplace(sa_ref, sb_ref, data, parts, rev, sa_ref, sb_ref, nxt)
    shift = data + 1 if rev else data - 1
    carry_a = sa_ref[pl.ds(shift, parts), :]
    carry_b = sb_ref[pl.ds(shift, parts), :]
    for r in order:
        a_r = a_ref[rows(r), :]
        b_ref[rows(r), :] = b_ref[rows(r), :] + a_r * carry_b
        a_ref[rows(r), :] = a_r * carry_a


def _softplus(x):
    return jnp.maximum(x, 0.0) + jnp.log1p(jnp.exp(-jnp.abs(x)))


def _gelu_tanh(x):
    return 0.5 * x * (1.0 + jnp.tanh(math.sqrt(2.0 / math.pi) * (x + 0.044715 * (x * x * x))))


def _lru_kernel(lxl_ref, lyl_ref, lxc_ref, lyc_ref, cw_ref, cb_ref, wa_ref, ba_ref, wx_ref, bx_ref, lam_ref,
                ol_ref, oc_ref, pad_ref, a0_ref, b0_ref, a1_ref, b1_ref, sa_ref, sb_ref, *, tt):
    l_ctx = lxc_ref.shape[0]
    l_lat = lxl_ref.shape[0]
    cw = cw_ref[...]
    cb = cb_ref[...]
    ab_refs = ((a0_ref, b0_ref), (a1_ref, b1_ref))
    neg_sp = [-LRU_C * _softplus(-lam_ref[d:d + 1, :]) for d in range(2)]
    wa = [wa_ref[d].astype(BF16) for d in range(2)]
    wx = [wx_ref[d].astype(BF16) for d in range(2)]

    for lx_ref, base, length, is_ctx in ((lxc_ref, 0, l_ctx, True), (lxl_ref, l_ctx, l_lat, False)):
        ttk = min(tt, length)
        pad_ref[pl.ds(0, SUBLANES), :] = jnp.zeros((SUBLANES, LANES), F32)
        pad_ref[pl.ds(SUBLANES + length, SUBLANES), :] = jnp.zeros((SUBLANES, LANES), F32)

        def fill(ci, carry, lx_ref=lx_ref, ttk=ttk):
            r0 = pl.multiple_of(ci * ttk, ttk)
            pad_ref[pl.ds(SUBLANES + r0, ttk), :] = lx_ref[pl.ds(r0, ttk), :].astype(F32)
            return carry

        lax.fori_loop(0, length // ttk, fill, 0)

        def gates(ci, carry, base=base, length=length, is_ctx=is_ctx, ttk=ttk):
            r0 = pl.multiple_of(ci * ttk, ttk)
            xx = pad_ref[pl.ds(r0, ttk + 2 * SUBLANES), :]
            u = (cw[0:1, :] * xx[6:6 + ttk] + cw[1:2, :] * xx[7:7 + ttk]
                 + cw[2:3, :] * xx[8:8 + ttk] + cw[3:4, :] * xx[9:9 + ttk]) + cb
            ub = u.astype(BF16)
            row = r0 + lax.broadcasted_iota(jnp.int32, u.shape, 0)
            for d in range(2):
                r_gate = _sigmoid(_dot(ub, wa[d]) + ba_ref[d:d + 1, :])
                i_gate = _sigmoid(_dot(ub, wx[d]) + bx_ref[d:d + 1, :])
                log_a = neg_sp[d] * r_gate
                a = jnp.exp(log_a)
                mult = jnp.sqrt(-jnp.tanh(log_a) * (a * a + 1.0))
                if is_ctx:
                    mult = jnp.where(row == (length - 1 if d == 1 else 0), 1.0, mult)
                a_ref, b_ref = ab_refs[d]
                a_ref[pl.ds(base + r0, ttk), :] = a
                b_ref[pl.ds(base + r0, ttk), :] = mult * (i_gate * u)
            return carry

        lax.fori_loop(0, length // ttk, gates, 0)

    for d, rev in ((0, False), (1, True)):
        a_ref, b_ref = ab_refs[d]
        _scan_inplace(a_ref, b_ref, 0, l_ctx, rev, sa_ref, sb_ref)
        _scan_inplace(a_ref, b_ref, l_ctx, l_lat, rev, sa_ref, sb_ref)

    h0 = (b0_ref[pl.ds(l_ctx - 1, 1), :], b1_ref[pl.ds(0, 1), :])

    for ly_ref, o_ref, base, length, is_ctx in ((lyc_ref, oc_ref, 0, l_ctx, True), (lyl_ref, ol_ref, l_ctx, l_lat, False)):
        ttk = min(tt, length)

        def emit(ci, carry, ly_ref=ly_ref, o_ref=o_ref, base=base, is_ctx=is_ctx, ttk=ttk):
            r0 = pl.multiple_of(ci * ttk, ttk)
            h = b0_ref[pl.ds(base + r0, ttk), :] + b1_ref[pl.ds(base + r0, ttk), :]
            if not is_ctx:
                h = h + a0_ref[pl.ds(base + r0, ttk), :] * h0[0] + a1_ref[pl.ds(base + r0, ttk), :] * h0[1]
            o_ref[pl.ds(r0, ttk), :] = (h * _gelu_tanh(ly_ref[pl.ds(r0, ttk), :].astype(F32))).astype(o_ref.dtype)
            return carry

        lax.fori_loop(0, length // ttk, emit, 0)


def _lru(cfg, p, conv_w, conv_b, w_a, b_a, w_x, b_x, lam_p, tt=256):
    wd, s, lc = cfg.branch_w, cfg.seq, cfg.ctx_len
    nb = wd // LANES
    l_tot = s + lc
    ctx_blk0 = cfg.m_lat // lc
    lx_col, ly_col = 6 * nb, 7 * nb
    vec2 = pl.BlockSpec((2, LANES), lambda b, n: (0, n))
    wspec = pl.BlockSpec((2, None, LANES, LANES), lambda b, n: (0, n, 0, 0))
    scan_rows = max(_scan_scratch_rows(s), _scan_scratch_rows(lc))
    return pl.pallas_call(
        functools.partial(_lru_kernel, tt=tt),
        out_shape=(jax.ShapeDtypeStruct((cfg.m_lat, wd), BF16), jax.ShapeDtypeStruct((cfg.batch * lc, wd), BF16)),
        grid=(cfg.batch, nb),
        in_specs=[
            pl.BlockSpec((s, LANES), lambda b, n: (b, lx_col + n)),
            pl.BlockSpec((s, LANES), lambda b, n: (b, ly_col + n)),
            pl.BlockSpec((lc, LANES), lambda b, n: (ctx_blk0 + b, lx_col + n)),
            pl.BlockSpec((lc, LANES), lambda b, n: (ctx_blk0 + b, ly_col + n)),
            pl.BlockSpec((4, LANES), lambda b, n: (0, n)),
            pl.BlockSpec((1, LANES), lambda b, n: (0, n)),
            wspec, vec2, wspec, vec2, vec2,
        ],
        out_specs=(pl.BlockSpec((s, LANES), lambda b, n: (b, n)), pl.BlockSpec((lc, LANES), lambda b, n: (b, n))),
        scratch_shapes=[pltpu.VMEM((s + 2 * SUBLANES, LANES), F32)]
        + [pltpu.VMEM((l_tot, LANES), F32)] * 4
        + [pltpu.VMEM((scan_rows, LANES), F32)] * 2,
        compiler_params=_params(2),
        name="rglru",
    )(p, p, p, p, conv_w, conv_b.reshape(1, wd), w_a, b_a, w_x, b_x, lam_p)


def _merge_kernel(*refs, wd, n_lat_blk):
    branch_refs, (ga_ref, gb_ref, gc_ref, w_ref, o_ref, wbf_ref) = refs[:6], refs[6:]
    _cast_weight_block(w_ref, wbf_ref)
    is_lat = pl.program_id(1) < n_lat_blk
    acc = None
    for i, g_ref in enumerate((ga_ref, gb_ref, gc_ref)):
        x = jnp.where(is_lat, branch_refs[2 * i][...], branch_refs[2 * i + 1][...])
        g = _sigmoid(g_ref[...].astype(F32))
        term = g * _dot(x, wbf_ref[i * wd:(i + 1) * wd, :])
        acc = term if acc is None else acc + term
    o_ref[...] = acc.astype(o_ref.dtype)


def _merge(cfg, branches, p, wb, layer, bm, bn):
    m, wd, d = cfg.m_all, cfg.branch_w, cfg.d_model
    g_blk0 = 8 * wd // bn
    nd = d // bn
    n_lat_blk = cfg.m_lat // bm
    lat_spec = pl.BlockSpec((bm, wd), lambda j, i: (jnp.minimum(i, n_lat_blk - 1), 0))
    ctx_spec = pl.BlockSpec((bm, wd), lambda j, i: (jnp.maximum(i - n_lat_blk, 0), 0))
    gspec = lambda br: pl.BlockSpec((bm, bn), lambda j, i: (i, g_blk0 + br * nd + j))
    args = [a for pair in branches for a in pair]
    return pl.pallas_call(
        functools.partial(_merge_kernel, wd=wd, n_lat_blk=n_lat_blk),
        out_shape=jax.ShapeDtypeStruct((m, d), BF16),
        grid=(nd, m // bm),
        in_specs=[lat_spec, ctx_spec] * 3 + [gspec(0), gspec(1), gspec(2),
                  pl.BlockSpec((None, 3 * wd, bn), lambda j, i: (layer, 0, j))],
        out_specs=pl.BlockSpec((bm, bn), lambda j, i: (i, j)),
        scratch_shapes=[pltpu.VMEM((3 * wd, bn), BF16)],
        compiler_params=_params(2),
        name="merge_branches",
    )(*args, p, p, p, wb)


MOE_TILE = 256
MOE_BLOCK = 256


def _moe_slots(cfg):
    n_tiles = cfg.m_all * cfg.top_k // MOE_TILE + cfg.n_experts
    return n_tiles, n_tiles * MOE_TILE


def _assignment_index(t, top_k):
    per_row = LANES // top_k
    assert per_row * top_k == LANES and per_row & (per_row - 1) == 0
    shift = per_row.bit_length() - 1
    return lax.shift_right_logical(t, shift), (t & (per_row - 1)) * top_k


def _dispatch_kernel(fill_ref, nchunk_ref, nv_ref, pos_ref, xq_ref, xs_hbm, zero_ref, pos_smem, sem, idx_sem,
                     *, n_experts, top_k, n_tiles):
    i = pl.program_id(0)
    n_assign = pos_ref.shape[0] * pos_ref.shape[1]
    bt = n_assign // top_k

    @pl.when(i == 0)
    def _():
        zero_ref[...] = jnp.zeros_like(zero_ref)
        piece = zero_ref.at[pl.ds(0, SUBLANES)]
        total = 0
        for e in range(n_experts):
            def fill_piece(c, carry, e=e):
                start = pl.multiple_of(fill_ref[e] + c * SUBLANES, SUBLANES)
                pltpu.make_async_copy(piece, xs_hbm.at[pl.ds(start, SUBLANES)], sem).start()
                return carry

            lax.fori_loop(0, nchunk_ref[e], fill_piece, 0)
            total = total + nchunk_ref[e]

        def idle_tile(j, carry):
            start = pl.multiple_of(j * MOE_TILE, MOE_TILE)
            pltpu.make_async_copy(zero_ref, xs_hbm.at[pl.ds(start, MOE_TILE)], sem).start()
            return carry

        lax.fori_loop(nv_ref[0], n_tiles, idle_tile, 0)

        def drain_piece(c, carry):
            pltpu.make_async_copy(piece, xs_hbm.at[pl.ds(0, SUBLANES)], sem).wait()
            return carry

        lax.fori_loop(0, total, drain_piece, 0)

        def drain_tile(j, carry):
            pltpu.make_async_copy(zero_ref, xs_hbm.at[pl.ds(0, MOE_TILE)], sem).wait()
            return carry

        lax.fori_loop(nv_ref[0], n_tiles, drain_tile, 0)

    idx_copy = pltpu.make_async_copy(pos_ref, pos_smem, idx_sem)
    idx_copy.start()
    idx_copy.wait()

    def issue(t, carry):
        row, col = _assignment_index(t, top_k)
        for k in range(top_k):
            dst = pos_smem[row, col + k]
            pltpu.make_async_copy(xq_ref.at[pl.ds(t, 1)], xs_hbm.at[pl.ds(dst, 1)], sem).start()
        return carry

    lax.fori_loop(0, bt, issue, 0, unroll=2)
    pltpu.make_async_copy(xs_hbm.at[pl.ds(0, n_assign)], xs_hbm.at[pl.ds(0, n_assign)], sem).wait()


def _dispatch(cfg, fill, nchunk, n_valid, pos2d, xq):
    m, half = xq.shape
    n_tiles, slot_rows = _moe_slots(cfg)
    rows_per_step = MOE_BLOCK * cfg.top_k // LANES
    return pl.pallas_call(
        functools.partial(_dispatch_kernel, n_experts=cfg.n_experts, top_k=cfg.top_k, n_tiles=n_tiles),
        out_shape=jax.ShapeDtypeStruct((slot_rows, half), jnp.uint32),
        grid_spec=pltpu.PrefetchScalarGridSpec(
            num_scalar_prefetch=3,
            grid=(m // MOE_BLOCK,),
            in_specs=[pl.BlockSpec((rows_per_step, LANES), lambda i, *_: (i, 0)),
                      pl.BlockSpec((MOE_BLOCK, half), lambda i, *_: (i, 0))],
            out_specs=pl.BlockSpec(memory_space=pl.ANY),
            scratch_shapes=[pltpu.VMEM((MOE_TILE, half), jnp.uint32),
                            pltpu.SMEM((rows_per_step, LANES), jnp.int32),
                            pltpu.SemaphoreType.DMA(()), pltpu.SemaphoreType.DMA(())]),
        compiler_params=_params(1),
        name="moe_dispatch",
    )(fill, nchunk, n_valid, pos2d, xq)


def _expert_kernel(te_ref, nv_ref, xs_ref, perm_ref, wgu_ref, bgu_ref, wdn_ref, bdn_ref, ys_ref,
                   wgu_bf, wdn_bf, *, ff):
    j = pl.program_id(0)
    d = wgu_ref.shape[0]
    half = d // 2
    e = te_ref[j]
    e_prev = te_ref[jnp.maximum(j - 1, 0)]

    @pl.when((j == 0) | (e != e_prev))
    def _():
        rows = min(512, d)

        def cast_gu(c, carry):
            r0 = pl.multiple_of(c * rows, rows)
            w = wgu_ref[pl.ds(r0, rows), :].astype(BF16)
            wgu_bf[pl.ds(r0, rows), :] = _dot(w, perm_ref[...]).astype(BF16)
            return carry

        lax.fori_loop(0, d // rows, cast_gu, 0)
        rows_dn = 32

        def cast_dn(c, carry):
            r0 = pl.multiple_of(c * rows_dn, rows_dn)
            wdn_bf[pl.ds(r0, rows_dn), :] = wdn_ref[pl.ds(r0, rows_dn), :].astype(BF16)
            return carry

        lax.fori_loop(0, ff // rows_dn, cast_dn, 0)

    @pl.when(j < nv_ref[0])
    def _():
        lo, hi = _unpack_pair(xs_ref[...])
        h = (_dot(lo.astype(BF16), wgu_bf[pl.ds(0, half), :]) + _dot(hi.astype(BF16), wgu_bf[pl.ds(half, half), :])
             + bgu_ref[...])
        h_glu = jnp.minimum(h[:, :ff], SWIGLU_LIMIT)
        h_lin = jnp.clip(h[:, ff:], -SWIGLU_LIMIT, SWIGLU_LIMIT)
        act = (h_glu * _sigmoid(SWIGLU_ALPHA * h_glu) * (h_lin + 1.0)).astype(BF16)
        cw = min(512, half)
        for c in range(half // cw):
            lo_sl = slice(c * cw, (c + 1) * cw)
            hi_sl = slice(half + c * cw, half + (c + 1) * cw)
            y_lo = _dot(act, wdn_bf[:, lo_sl]) + bdn_ref[:, lo_sl]
            y_hi = _dot(act, wdn_bf[:, hi_sl]) + bdn_ref[:, hi_sl]
            ys_ref[:, lo_sl] = _pack_pair(y_lo, y_hi)

    @pl.when(j >= nv_ref[0])
    def _():
        ys_ref[...] = jnp.zeros_like(ys_ref)


def _experts(cfg, tile_e, n_valid, xs, perm, w_gu, b_gu_perm, w_dn, b_dn, layer):
    d, ff, ne = cfg.d_model, cfg.expert_ff, cfg.n_experts
    half = d // 2
    n_tiles, slot_rows = _moe_slots(cfg)
    tile_blk = lambda j, te, nv: (jnp.minimum(j, nv[0] - 1), 0)
    by_expert = lambda j, te, nv: (te[j], 0, 0)
    by_layer_expert = lambda j, te, nv: (layer, te[j], 0, 0)
    return pl.pallas_call(
        functools.partial(_expert_kernel, ff=ff),
        out_shape=jax.ShapeDtypeStruct((slot_rows, half), jnp.uint32),
        grid_spec=pltpu.PrefetchScalarGridSpec(
            num_scalar_prefetch=2,
            grid=(n_tiles,),
            in_specs=[
                pl.BlockSpec((MOE_TILE, half), tile_blk),
                pl.BlockSpec((2 * ff, 2 * ff), lambda j, te, nv: (0, 0)),
                pl.BlockSpec((None, None, d, 2 * ff), by_layer_expert),
                pl.BlockSpec((None, 1, 2 * ff), by_expert),
                pl.BlockSpec((None, None, ff, d), by_layer_expert),
                pl.BlockSpec((None, 1, d), by_expert),
            ],
            out_specs=pl.BlockSpec((MOE_TILE, half), lambda j, te, nv: (j, 0)),
            scratch_shapes=[pltpu.VMEM((d, 2 * ff), BF16), pltpu.VMEM((ff, d), BF16)]),
        compiler_params=_params(1),
        name="moe_experts",
    )(tile_e, n_valid, xs, perm, w_gu, b_gu_perm.reshape(ne, 1, 2 * ff), w_dn, b_dn.reshape(ne, 1, d))


def _combine_kernel(pos_ref, pos_next_ref, route_ref, res_ref, mod_ref, ys_hbm, o_ref, buf_ref, pos_smem,
                    sems, idx_sem, *, gate, top_k):
    i = pl.program_id(0)
    n = pl.num_programs(0)
    bt, d = res_ref.shape
    half = d // 2
    slot = lax.rem(i, 2)
    other = 1 - slot

    def load_positions(src_ref):
        idx_copy = pltpu.make_async_copy(src_ref, pos_smem, idx_sem)
        idx_copy.start()
        idx_copy.wait()

    def gather_token(t, to_slot):
        row, col = _assignment_index(t, top_k)
        for k in range(top_k):
            src = pos_smem[row, col + k]
            pltpu.make_async_copy(ys_hbm.at[pl.ds(src, 1)], buf_ref.at[to_slot, k, pl.ds(t, 1)],
                                  sems.at[to_slot]).start()

    def wait_slot(s):
        pltpu.make_async_copy(buf_ref.at[s], buf_ref.at[s], sems.at[s]).wait()

    @pl.when(i == 0)
    def _():
        load_positions(pos_ref)

        def first(t, carry):
            gather_token(t, 0)
            return carry

        lax.fori_loop(0, bt, first, 0, unroll=2)

    wait_slot(slot)
    load_positions(pos_next_ref)

    rb = 2 * SUBLANES
    cw = min(512, half)

    def rows(r, carry):
        r0 = pl.multiple_of(r * rb, rb)
        rs = pl.ds(r0, rb)
        for u in range(rb):
            gather_token(r0 + u, other)
        weights = [route_ref[rs, top_k + k:top_k + k + 1] for k in range(top_k)]
        for c in range(half // cw):
            lo_sl = slice(c * cw, (c + 1) * cw)
            hi_sl = slice(half + c * cw, half + (c + 1) * cw)
            acc_lo = acc_hi = None
            for k in range(top_k):
                lo, hi = _unpack_pair(buf_ref[slot, k, rs, lo_sl])
                acc_lo = weights[k] * lo if acc_lo is None else acc_lo + weights[k] * lo
                acc_hi = weights[k] * hi if acc_hi is None else acc_hi + weights[k] * hi
            o_ref[rs, lo_sl] = res_ref[rs, lo_sl] + mod_ref[0, gate:gate + 1, lo_sl] * acc_lo
            o_ref[rs, hi_sl] = res_ref[rs, hi_sl] + mod_ref[0, gate:gate + 1, hi_sl] * acc_hi
        return carry

    lax.fori_loop(0, bt // rb, rows, 0)

    @pl.when(i == n - 1)
    def _():
        wait_slot(other)


def _combine(cfg, pos2d, route, res, mod, ys, gate):
    m, d = res.shape
    half = d // 2
    ep = route.shape[1]
    bt = MOE_BLOCK
    rows_per_step = bt * cfg.top_k // LANES
    return pl.pallas_call(
        functools.partial(_combine_kernel, gate=gate, top_k=cfg.top_k),
        out_shape=jax.ShapeDtypeStruct((m, d), F32),
        grid=(m // bt,),
        in_specs=[
            pl.BlockSpec((rows_per_step, LANES), lambda i: (i, 0)),
            pl.BlockSpec((rows_per_step, LANES), lambda i: (jnp.minimum(i + 1, m // bt - 1), 0)),
            pl.BlockSpec((bt, ep), lambda i: (i, 0)),
            pl.BlockSpec((bt, d), lambda i: (i, 0)),
            pl.BlockSpec((1, N_MOD, d), lambda i: (_seg_of_block(cfg, i, bt), 0, 0)),
            pl.BlockSpec(memory_space=pl.ANY),
        ],
        out_specs=pl.BlockSpec((bt, d), lambda i: (i, 0)),
        scratch_shapes=[pltpu.VMEM((2, cfg.top_k, bt, half), jnp.uint32),
                        pltpu.SMEM((rows_per_step, LANES), jnp.int32),
                        pltpu.SemaphoreType.DMA((2,)), pltpu.SemaphoreType.DMA(())],
        compiler_params=_params(1),
        name="moe_combine",
    )(pos2d, pos2d, route, res, mod, ys)


def _moe_plan(cfg, route, counts8):
    k, ne = cfg.top_k, cfg.n_experts
    n_tiles, _ = _moe_slots(cfg)
    expert = route[:, :k].astype(jnp.int32)
    rank = route[:, 2 * k:3 * k].astype(jnp.int32)
    counts = counts8[0, :ne].astype(jnp.int32)
    padded = (counts + MOE_TILE - 1) // MOE_TILE * MOE_TILE
    region_end = jnp.cumsum(padded)
    region_start = region_end - padded
    pos = region_start[expert] + rank
    fill = (region_start + counts) // SUBLANES * SUBLANES
    nchunk = (region_end - fill) // SUBLANES
    n_valid = region_end[-1] // MOE_TILE
    tile_start = jnp.arange(n_tiles, dtype=jnp.int32) * MOE_TILE
    tile_e = jnp.sum(region_end[None, :] <= tile_start[:, None], axis=1)
    tile_e = jnp.minimum(tile_e, ne - 1).astype(jnp.int32)
    tile_e = tile_e[jnp.minimum(jnp.arange(n_tiles), n_valid - 1)]
    return (pos.reshape(-1, LANES), fill.astype(jnp.int32), nchunk.astype(jnp.int32), tile_e,
            n_valid.reshape(1).astype(jnp.int32))


def _deinterleave_perm(ff):
    col = jnp.arange(2 * ff)
    src = jnp.where(col < ff, 2 * col, 2 * (col - ff) + 1)
    return (jnp.arange(2 * ff)[:, None] == src[None, :]).astype(BF16)


def _forward(cfg, x, c, ctx, c_ctx, ada_a, ada_b, ada_bias, norm1_g, norm2_g, w_in, conv_a_w, lam_qk,
             subln_g, lru_conv_w, lru_conv_b, lru_w_a, lru_b_a, lru_w_x, lru_b_x, lru_lambda,
             w_branch, w_out, router_w, router_b, w_gu, b_gu, w_dn, b_dn, final_g):
    d, wd, bsz, s, lc = cfg.d_model, cfg.branch_w, cfg.batch, cfg.seq, cfg.ctx_len
    m_lat, m_all, ne = cfg.m_lat, cfg.m_all, cfg.n_experts
    assert bsz * lc <= s and s % lc == 0 and bsz + 1 <= SUBLANES
    bm = min(512, bsz * lc)
    bn = min(1024, d)
    bt = min(256, lc)
    ep = -(-ne // LANES) * LANES

    xs = jnp.concatenate([x.reshape(m_lat, d), ctx.reshape(bsz * lc, d)], axis=0)
    cond8 = jnp.zeros((SUBLANES, d), F32).at[:bsz].set(c).at[bsz].set(c_ctx)
    ctab, stab = _rope_tables(cfg, bt)
    ctx_blk = lambda b: m_lat // lc + b
    perm = _deinterleave_perm(cfg.expert_ff)

    for l in range(cfg.depth):
        lam_init = LAMBDA_INIT_BASE - LAMBDA_INIT_AMP * math.exp(-LAMBDA_INIT_RATE * l)
        mod = _ada(cfg, cond8, ada_a[l], ada_b[l], ada_bias[l])

        xn = _norm_mod(cfg, xs, norm1_g[l], mod, 0, 1)
        p = _matmul(xn, w_in, l, bm, bn)
        oa = (_conv_mixer(cfg, p, conv_a_w[l], s, 0, bsz, m_lat),
              _conv_mixer(cfg, p, conv_a_w[l], lc, m_lat // lc, bsz, bsz * lc))
        qt, kr, vt = _att_prep(cfg, p, ctab, stab, bt)
        nh = wd // ATT_VD
        ctx_seg = (p, ctx_blk, 4 * nh, vt, lc)
        lat_seg = (kr, lambda b: b, 0, vt, s)
        bq_lat = min(ATT_LATENT_BQ, s)
        ob_lat = _attention(cfg, lam_qk[l], subln_g[l], qt, [ctx_seg, lat_seg], 0, s // bq_lat, m_lat, lam_init,
                            bq=bq_lat)
        ob_ctx = _attention(cfg, lam_qk[l], subln_g[l], qt, [ctx_seg], m_lat // bt, lc // bt, bsz * lc, lam_init, bq=bt)
        oc = _lru(cfg, p, lru_conv_w[l], lru_conv_b[l], lru_w_a[l], lru_b_a[l],
                  lru_w_x[l], lru_b_x[l], lru_lambda[l])
        y = _merge(cfg, (oa, (ob_lat, ob_ctx), oc), p, w_branch, l, bm, bn)
        xs = _matmul_residual(cfg, y, w_out, l, xs, mod, 2, bm, bn)

        wr_pad = jnp.zeros((d, ep), F32).at[:, :ne].set(router_w[l])
        br_pad = jnp.full((1, ep), -jnp.inf, F32).at[0, :ne].set(router_b[l])
        xq, route, counts8 = _norm_router(cfg, xs, norm2_g[l], mod, 3, 4, wr_pad, br_pad)
        pos2d, fill, nchunk, tile_e, n_valid = _moe_plan(cfg, route, counts8)
        slots = _dispatch(cfg, fill, nchunk, n_valid, pos2d, xq)
        b_gu_perm = jnp.concatenate([b_gu[l][:, 0::2], b_gu[l][:, 1::2]], axis=-1)
        ys = _experts(cfg, tile_e, n_valid, slots, perm, w_gu, b_gu_perm, w_dn, b_dn[l], l)
        xs = _combine(cfg, pos2d, route, xs, mod, ys, 5)

    return _norm_plain(xs, final_g, m_lat).reshape(bsz, s, d)


def kernel(x, c, ctx, c_ctx, ada_a, ada_b, ada_bias, norm1_g, norm2_g, w_in, conv_a_w, lam_qk, subln_g, lru_conv_w, lru_conv_b, lru_w_a, lru_b_a, lru_w_x, lru_b_x, lru_lambda, w_branch, w_out, router_w, router_b, w_gu, b_gu, w_dn, b_dn, final_g):
    return _forward(_Cfg(), x, c, ctx, c_ctx, ada_a, ada_b, ada_bias, norm1_g, norm2_g, w_in, conv_a_w, lam_qk,
                    subln_g, lru_conv_w, lru_conv_b, lru_w_a, lru_b_a, lru_w_x, lru_b_x, lru_lambda,
                    w_branch, w_out, router_w, router_b, w_gu, b_gu, w_dn, b_dn, final_g)
```

```python
import functools
import math
from typing import NamedTuple

import jax
import jax.numpy as jnp
from jax import lax
from jax.experimental import pallas as pl
from jax.experimental.pallas import tpu as pltpu

F32 = jnp.float32
BF16 = jnp.bfloat16

LANES = 128
SUBLANES = 8
EPS = 1e-6
ATT_HD = 64
ATT_VD = 2 * ATT_HD
ATT_UNROLL = 8
ATT_LATENT_BQ = 512
ATT_Q_SCALE = ATT_HD ** -0.5 * math.log2(math.e)
ROPE_THETA = 10000.0
ROPE_AXIS_DIM = ATT_HD // 2
ROPE_PAIRS = ROPE_AXIS_DIM // 2
LAMBDA_INIT_BASE = 0.8
LAMBDA_INIT_AMP = 0.6
LAMBDA_INIT_RATE = 0.3
LRU_C = 8.0
SWIGLU_ALPHA = 1.702
SWIGLU_LIMIT = 7.0
N_MOD = 6
VMEM_LIMIT = 56 * 1024 * 1024


class _Cfg(NamedTuple):
    d_model: int = 4096
    batch: int = 2
    seq: int = 4096
    depth: int = 4
    grid_w: int = 64
    ctx_len: int = 256
    branch_w: int = 1024
    n_experts: int = 32
    top_k: int = 4
    expert_ff: int = 256

    @property
    def heads(self):
        return self.branch_w // ATT_VD

    @property
    def m_lat(self):
        return self.batch * self.seq

    @property
    def m_all(self):
        return self.batch * (self.seq + self.ctx_len)

    @property
    def in_cols(self):
        return 8 * self.branch_w + 3 * self.d_model


def _params(n_grid_dims):
    return pltpu.CompilerParams(
        dimension_semantics=("arbitrary",) * n_grid_dims, vmem_limit_bytes=VMEM_LIMIT)


def _dot(a, b):
    return jnp.dot(a, b, preferred_element_type=F32)


def _split_bf16(x):
    hi = x.astype(BF16)
    lo = (x - hi.astype(F32)).astype(BF16)
    return hi, lo


def _dot3(x, w):
    xh, xl = _split_bf16(x)
    wh, wl = _split_bf16(w)
    return _dot(xh, wh) + (_dot(xl, wh) + _dot(xh, wl))


def _sigmoid(x):
    return 1.0 / (1.0 + jnp.exp(-x))


def _ada_kernel(cond_ref, a_ref, b_ref, bias_ref, o_ref):
    cond = cond_ref[...]
    t = _dot3(cond * _sigmoid(cond), a_ref[...])
    o_ref[...] = _dot3(t, b_ref[...]) + bias_ref[...]


def _ada(cfg, cond8, a, b, bias):
    d = cfg.d_model
    rank = a.shape[1]
    out = pl.pallas_call(
        _ada_kernel,
        out_shape=jax.ShapeDtypeStruct((SUBLANES, N_MOD * d), F32),
        grid=(N_MOD,),
        in_specs=[
            pl.BlockSpec((SUBLANES, d), lambda j: (0, 0)),
            pl.BlockSpec((d, rank), lambda j: (0, 0)),
            pl.BlockSpec((rank, d), lambda j: (0, j)),
            pl.BlockSpec((1, d), lambda j: (0, j)),
        ],
        out_specs=pl.BlockSpec((SUBLANES, d), lambda j: (0, j)),
        compiler_params=_params(1),
        name="ada_modulation",
    )(cond8, a, b, bias.reshape(1, N_MOD * d))
    return out.reshape(SUBLANES, N_MOD, d)[: cfg.batch + 1]


def _rms(x, g):
    return x * lax.rsqrt(jnp.mean(x * x, axis=-1, keepdims=True) + EPS) * g


def _norm_mod_kernel(x_ref, g_ref, mod_ref, o_ref, *, shift, scale):
    y = _rms(x_ref[...], g_ref[...])
    y = y * (1.0 + mod_ref[0, scale:scale + 1, :]) + mod_ref[0, shift:shift + 1, :]
    o_ref[...] = y.astype(o_ref.dtype)


def _norm_plain_kernel(x_ref, g_ref, o_ref):
    o_ref[...] = _rms(x_ref[...], g_ref[...]).astype(o_ref.dtype)


def _pack_pair(lo, hi):
    lo_bits = lax.bitcast_convert_type(lo.astype(BF16).astype(F32), jnp.uint32)
    hi_bits = lax.bitcast_convert_type(hi.astype(BF16).astype(F32), jnp.uint32)
    return (lo_bits >> 16) | (hi_bits & jnp.uint32(0xFFFF0000))


def _unpack_pair(word):
    lo = lax.bitcast_convert_type(word << 16, F32)
    hi = lax.bitcast_convert_type(word & jnp.uint32(0xFFFF0000), F32)
    return lo, hi


def _norm_router_kernel(x_ref, g_ref, mod_ref, wr_ref, br_ref, xq_ref, route_ref, counts_ref, carry_ref,
                        *, shift, scale, top_k):
    @pl.when(pl.program_id(0) == 0)
    def _():
        carry_ref[...] = jnp.zeros_like(carry_ref)

    y = _rms(x_ref[...], g_ref[...])
    y = y * (1.0 + mod_ref[0, scale:scale + 1, :]) + mod_ref[0, shift:shift + 1, :]
    half = y.shape[1] // 2
    xq_ref[...] = _pack_pair(y[:, :half], y[:, half:])
    work = _dot3(y, wr_ref[...]) + br_ref[...]
    bm, n_lane = work.shape
    lane = lax.broadcasted_iota(jnp.int32, work.shape, 1).astype(F32)
    vals, hots, ids = [], [], []
    for _ in range(top_k):
        m = jnp.max(work, axis=-1, keepdims=True)
        idx = jnp.min(jnp.where(work == m, lane, float(n_lane)), axis=-1, keepdims=True)
        hot = lane == idx
        vals.append(m)
        hots.append(hot)
        ids.append(idx)
        work = jnp.where(hot, -jnp.inf, work)
    exps = [jnp.exp(v - vals[0]) for v in vals]
    denom = exps[0]
    for e in exps[1:]:
        denom = denom + e
    picked = jnp.zeros(work.shape, F32)
    for hot in hots:
        picked = jnp.where(hot, 1.0, picked)
    tri = (lax.broadcasted_iota(jnp.int32, (bm, bm), 0) > lax.broadcasted_iota(jnp.int32, (bm, bm), 1))
    before = _dot(jnp.where(tri, 1.0, 0.0).astype(BF16), picked.astype(BF16)) + carry_ref[0:1, :]
    carry_ref[...] = carry_ref[...] + jnp.sum(picked, axis=0, keepdims=True)
    counts_ref[...] = carry_ref[...]
    route = jnp.zeros(work.shape, F32)
    for k in range(top_k):
        rank = jnp.sum(jnp.where(hots[k], before, 0.0), axis=-1, keepdims=True)
        route = jnp.where(lane == float(k), ids[k], route)
        route = jnp.where(lane == float(top_k + k), exps[k] / denom, route)
        route = jnp.where(lane == float(2 * top_k + k), rank, route)
    route_ref[...] = route


def _seg_of_block(cfg, i, bm):
    return (i * bm) // cfg.seq


def _norm_mod(cfg, x, g, mod, shift, scale, bm=256):
    m, d = x.shape
    return pl.pallas_call(
        functools.partial(_norm_mod_kernel, shift=shift, scale=scale),
        out_shape=jax.ShapeDtypeStruct((m, d), BF16),
        grid=(m // bm,),
        in_specs=[
            pl.BlockSpec((bm, d), lambda i: (i, 0)),
            pl.BlockSpec((1, d), lambda i: (0, 0)),
            pl.BlockSpec((1, N_MOD, d), lambda i: (_seg_of_block(cfg, i, bm), 0, 0)),
        ],
        out_specs=pl.BlockSpec((bm, d), lambda i: (i, 0)),
        compiler_params=_params(1),
        name="norm_mod",
    )(x, g.reshape(1, d), mod)


def _norm_plain(x, g, m_rows, bm=256):
    d = x.shape[1]
    return pl.pallas_call(
        _norm_plain_kernel,
        out_shape=jax.ShapeDtypeStruct((m_rows, d), F32),
        grid=(m_rows // bm,),
        in_specs=[
            pl.BlockSpec((bm, d), lambda i: (i, 0)),
            pl.BlockSpec((1, d), lambda i: (0, 0)),
        ],
        out_specs=pl.BlockSpec((bm, d), lambda i: (i, 0)),
        compiler_params=_params(1),
        name="final_norm",
    )(x, g.reshape(1, d))


def _norm_router(cfg, x, g, mod, shift, scale, wr_pad, br_pad, bm=256):
    m, d = x.shape
    ep = wr_pad.shape[1]
    return pl.pallas_call(
        functools.partial(_norm_router_kernel, shift=shift, scale=scale, top_k=cfg.top_k),
        out_shape=(jax.ShapeDtypeStruct((m, d // 2), jnp.uint32), jax.ShapeDtypeStruct((m, ep), F32),
                   jax.ShapeDtypeStruct((SUBLANES, ep), F32)),
        grid=(m // bm,),
        in_specs=[
            pl.BlockSpec((bm, d), lambda i: (i, 0)),
            pl.BlockSpec((1, d), lambda i: (0, 0)),
            pl.BlockSpec((1, N_MOD, d), lambda i: (_seg_of_block(cfg, i, bm), 0, 0)),
            pl.BlockSpec((d, ep), lambda i: (0, 0)),
            pl.BlockSpec((1, ep), lambda i: (0, 0)),
        ],
        out_specs=(pl.BlockSpec((bm, d // 2), lambda i: (i, 0)), pl.BlockSpec((bm, ep), lambda i: (i, 0)),
                   pl.BlockSpec((SUBLANES, ep), lambda i: (0, 0))),
        scratch_shapes=[pltpu.VMEM((SUBLANES, ep), F32)],
        compiler_params=_params(1),
        name="norm_router",
    )(x, g.reshape(1, d), mod, wr_pad, br_pad)


CAST_ROWS = 256


def _cast_weight_block(w_ref, wbf_ref):
    @pl.when(pl.program_id(1) == 0)
    def _():
        rows = min(CAST_ROWS, w_ref.shape[0])

        def body(c, carry):
            r0 = pl.multiple_of(c * rows, rows)
            wbf_ref[pl.ds(r0, rows), :] = w_ref[pl.ds(r0, rows), :].astype(BF16)
            return carry

        lax.fori_loop(0, w_ref.shape[0] // rows, body, 0)


def _mm_kernel(x_ref, w_ref, o_ref, wbf_ref):
    _cast_weight_block(w_ref, wbf_ref)
    o_ref[...] = _dot(x_ref[...], wbf_ref[...]).astype(o_ref.dtype)


def _mm_res_kernel(x_ref, w_ref, res_ref, mod_ref, o_ref, wbf_ref, *, gate):
    _cast_weight_block(w_ref, wbf_ref)
    o_ref[...] = res_ref[...] + mod_ref[0, gate:gate + 1, :] * _dot(x_ref[...], wbf_ref[...])


def _matmul(x, w, layer, bm, bn, out_dtype=BF16):
    m, k = x.shape
    n = w.shape[2]
    return pl.pallas_call(
        _mm_kernel,
        out_shape=jax.ShapeDtypeStruct((m, n), out_dtype),
        grid=(n // bn, m // bm),
        in_specs=[
            pl.BlockSpec((bm, k), lambda j, i: (i, 0)),
            pl.BlockSpec((None, k, bn), lambda j, i: (layer, 0, j)),
        ],
        out_specs=pl.BlockSpec((bm, bn), lambda j, i: (i, j)),
        scratch_shapes=[pltpu.VMEM((k, bn), BF16)],
        compiler_params=_params(2),
        name="matmul",
    )(x, w)


def _matmul_residual(cfg, x, w, layer, res, mod, gate, bm, bn):
    m, k = x.shape
    n = w.shape[2]
    return pl.pallas_call(
        functools.partial(_mm_res_kernel, gate=gate),
        out_shape=jax.ShapeDtypeStruct((m, n), F32),
        grid=(n // bn, m // bm),
        in_specs=[
            pl.BlockSpec((bm, k), lambda j, i: (i, 0)),
            pl.BlockSpec((None, k, bn), lambda j, i: (layer, 0, j), pipeline_mode=pl.Buffered(1)),
            pl.BlockSpec((bm, bn), lambda j, i: (i, j)),
            pl.BlockSpec((1, N_MOD, bn), lambda j, i: (_seg_of_block(cfg, i, bm), 0, j)),
        ],
        out_specs=pl.BlockSpec((bm, bn), lambda j, i: (i, j)),
        scratch_shapes=[pltpu.VMEM((k, bn), BF16)],
        compiler_params=_params(2),
        name="matmul_residual",
    )(x, w, res, mod)


def _conv_a_kernel(b_ref, c_ref, h_ref, w_ref, o_ref, *, tt):
    t_len = b_ref.shape[0]
    w = w_ref[...]
    halo = 16

    def prod(r0, n):
        return c_ref[pl.ds(r0, n), :].astype(F32) * h_ref[pl.ds(r0, n), :].astype(F32)

    def chunk(ci, carry):
        r0 = pl.multiple_of(ci * tt, tt)
        u = prod(r0, tt)
        prev = prod(pl.multiple_of(jnp.maximum(r0 - halo, 0), halo), halo)[halo - 1:halo, :]
        prev = jnp.where(r0 == 0, 0.0, prev)
        nxt = prod(pl.multiple_of(jnp.minimum(r0 + tt, t_len - halo), halo), halo)[0:1, :]
        nxt = jnp.where(r0 + tt == t_len, 0.0, nxt)
        row = lax.broadcasted_iota(jnp.int32, u.shape, 0)
        up = jnp.where(row == 0, prev, pltpu.roll(u, 1, 0))
        dn = jnp.where(row == tt - 1, nxt, pltpu.roll(u, tt - 1, 0))
        conv = w[0:1, :] * up + w[1:2, :] * u + w[2:3, :] * dn
        o_ref[pl.ds(r0, tt), :] = (b_ref[pl.ds(r0, tt), :].astype(F32) * conv).astype(o_ref.dtype)
        return carry

    lax.fori_loop(0, t_len // tt, chunk, 0)


def _conv_mixer(cfg, p, w, t_len, row_blk0, n_seq, out_rows, bc=256):
    wd = cfg.branch_w
    ncb = wd // bc
    tt = min(256, t_len)
    spec = lambda off: pl.BlockSpec((t_len, bc), lambda s, j: (row_blk0 + s, off * ncb + j))
    return pl.pallas_call(
        functools.partial(_conv_a_kernel, tt=tt),
        out_shape=jax.ShapeDtypeStruct((out_rows, wd), BF16),
        grid=(n_seq, ncb),
        in_specs=[spec(0), spec(1), spec(2), pl.BlockSpec((3, bc), lambda s, j: (0, j))],
        out_specs=pl.BlockSpec((t_len, bc), lambda s, j: (s, j)),
        compiler_params=_params(2),
        name="conv_mixer",
    )(p, p, p, w)


def _rope_tables(cfg, bt):
    t = jnp.arange(cfg.seq)
    pos = jnp.stack([t // cfg.grid_w, t % cfg.grid_w], axis=-1).astype(F32)
    inv = ROPE_THETA ** (-jnp.arange(ROPE_PAIRS, dtype=F32) * 2.0 / ROPE_AXIS_DIM)
    ang = pos[:, :, None] * inv
    cos, sin = jnp.cos(ang), jnp.sin(ang)
    cmap = jnp.concatenate([cos[:, 0], cos[:, 0], cos[:, 1], cos[:, 1]], axis=-1)
    smap = jnp.concatenate([-sin[:, 0], sin[:, 0], -sin[:, 1], sin[:, 1]], axis=-1)
    ctab = jnp.concatenate([jnp.tile(cmap, (1, 2)), jnp.ones((bt, ATT_VD), F32)], axis=0)
    stab = jnp.concatenate([jnp.tile(smap, (1, 2)), jnp.zeros((bt, ATT_VD), F32)], axis=0)
    return ctab, stab


def _att_prep_kernel(q_ref, k_ref, v_ref, c_ref, s_ref, qt_ref, kr_ref, vt_ref, *, heads):
    cos = c_ref[...]
    sin = s_ref[...]
    lane = lax.broadcasted_iota(jnp.int32, cos.shape, 1)
    low = (lane % ROPE_AXIS_DIM) < ROPE_PAIRS

    def rope(u):
        swapped = jnp.where(low, pltpu.roll(u, ATT_VD - ROPE_PAIRS, 1), pltpu.roll(u, ROPE_PAIRS, 1))
        return u * cos + swapped * sin

    for h in range(heads):
        sl = slice(h * ATT_VD, (h + 1) * ATT_VD)
        qt_ref[sl, :] = (rope(q_ref[:, sl].astype(F32)) * ATT_Q_SCALE).T.astype(qt_ref.dtype)
        kr_ref[:, sl] = rope(k_ref[:, sl].astype(F32)).astype(kr_ref.dtype)
        vt_ref[sl, :] = v_ref[:, sl].astype(F32).T.astype(vt_ref.dtype)


def _att_prep(cfg, p, ctab, stab, bt=256):
    m, wd = cfg.m_all, cfg.branch_w
    n_lat_blk = cfg.m_lat // bt
    n_seq_blk = cfg.seq // bt
    tab_map = lambda i: (jnp.where(i < n_lat_blk, i % n_seq_blk, n_seq_blk), 0)
    return pl.pallas_call(
        functools.partial(_att_prep_kernel, heads=cfg.heads),
        out_shape=(jax.ShapeDtypeStruct((wd, m), BF16), jax.ShapeDtypeStruct((m, wd), BF16),
                   jax.ShapeDtypeStruct((wd, m), BF16)),
        grid=(m // bt,),
        in_specs=[
            pl.BlockSpec((bt, wd), lambda i: (i, 3)),
            pl.BlockSpec((bt, wd), lambda i: (i, 4)),
            pl.BlockSpec((bt, wd), lambda i: (i, 5)),
            pl.BlockSpec((bt, ATT_VD), tab_map),
            pl.BlockSpec((bt, ATT_VD), tab_map),
        ],
        out_specs=(pl.BlockSpec((wd, bt), lambda i: (0, i)), pl.BlockSpec((bt, wd), lambda i: (i, 0)),
                   pl.BlockSpec((wd, bt), lambda i: (0, i))),
        compiler_params=_params(1),
        name="att_prep",
    )(p, p, p, ctab, stab)


def _attn_kernel(lam_ref, g_ref, qt_ref, *refs, seg_lens, tk, lam_init):
    n_seg = len(seg_lens)
    kv = refs[:2 * n_seg]
    o_ref = refs[2 * n_seg]
    s1_ref, s2_ref, acc1_ref, acc2_ref = refs[2 * n_seg + 1:]
    bq = qt_ref.shape[1]

    qt = qt_ref[...].astype(F32)
    sub = lax.broadcasted_iota(jnp.int32, qt.shape, 0)
    q1 = jnp.where(sub < ATT_HD, qt, 0.0).astype(qt_ref.dtype)
    q2 = jnp.where(sub >= ATT_HD, qt, 0.0).astype(qt_ref.dtype)

    def chunks(seg):
        tkk = min(tk, seg_lens[seg])
        return tkk, seg_lens[seg] // tkk

    def pass_a(q, s_ref):
        m = jnp.full((1, bq), -jnp.inf, F32)
        off = 0
        for seg in range(n_seg):
            k_ref = kv[2 * seg]
            tkk, n_chunk = chunks(seg)

            def body(c, m, k_ref=k_ref, tkk=tkk, off=off):
                r0 = pl.multiple_of(c * tkk, tkk)
                s = _dot(k_ref[pl.ds(r0, tkk), :], q)
                s_ref[pl.ds(off + r0, tkk), :] = s
                return jnp.maximum(m, jnp.max(s, axis=0, keepdims=True))

            m = lax.fori_loop(0, n_chunk, body, m, unroll=min(ATT_UNROLL, n_chunk))
            off += seg_lens[seg]
        return m

    def pass_b(s_ref, m, acc_ref):
        acc_ref[...] = jnp.zeros_like(acc_ref)
        l = jnp.zeros((1, bq), F32)
        off = 0
        for seg in range(n_seg):
            vt_ref = kv[2 * seg + 1]
            tkk, n_chunk = chunks(seg)

            def body(c, l, vt_ref=vt_ref, tkk=tkk, off=off):
                r0 = pl.multiple_of(c * tkk, tkk)
                p = jnp.exp2(s_ref[pl.ds(off + r0, tkk), :] - m)
                vt = vt_ref[:, pl.ds(r0, tkk)]
                acc_ref[...] += _dot(vt, p.astype(vt.dtype))
                return l + jnp.sum(p, axis=0, keepdims=True)

            l = lax.fori_loop(0, n_chunk, body, l, unroll=min(ATT_UNROLL, n_chunk))
            off += seg_lens[seg]
        return l

    m1 = pass_a(q1, s1_ref)
    acc1_ref[...] = jnp.zeros_like(acc1_ref)
    l1 = jnp.zeros((1, bq), F32)
    m2 = jnp.full((1, bq), -jnp.inf, F32)
    off = 0
    for seg in range(n_seg):
        k_ref, vt_ref = kv[2 * seg], kv[2 * seg + 1]
        tkk, n_chunk = chunks(seg)
        for c in range(n_chunk):
            r0 = c * tkk
            s = _dot(k_ref[pl.ds(r0, tkk), :], q2)
            s2_ref[pl.ds(off + r0, tkk), :] = s
            m2 = jnp.maximum(m2, jnp.max(s, axis=0, keepdims=True))
            p = jnp.exp2(s1_ref[pl.ds(off + r0, tkk), :] - m1)
            vt = vt_ref[:, pl.ds(r0, tkk)]
            acc1_ref[...] += _dot(vt, p.astype(vt.dtype))
            l1 = l1 + jnp.sum(p, axis=0, keepdims=True)
        off += seg_lens[seg]
    l2 = pass_b(s2_ref, m2, acc2_ref)

    lq = lam_ref[...]
    lam = (jnp.exp(jnp.sum(lq[0:1] * lq[1:2], axis=-1, keepdims=True))
           - jnp.exp(jnp.sum(lq[2:3] * lq[3:4], axis=-1, keepdims=True)) + lam_init)
    ot = acc1_ref[...] * (1.0 / l1) - acc2_ref[...] * (lam / l2)
    ms = jnp.mean(ot * ot, axis=0, keepdims=True)
    y = ot * lax.rsqrt(ms + EPS) * g_ref[...] * (1.0 - lam_init)
    o_ref[...] = y.T.astype(o_ref.dtype)


def _attention(cfg, lam_qk, subln_g, qt, segs, q_col_blk0, n_q_per_batch, out_rows, lam_init, bq=256, tk=512):
    wd = cfg.branch_w
    in_specs = [
        pl.BlockSpec((4, ATT_HD), lambda b, h, i: (0, 0)),
        pl.BlockSpec((ATT_VD, 1), lambda b, h, i: (0, 0)),
        pl.BlockSpec((ATT_VD, bq), lambda b, h, i: (h, q_col_blk0 + b * n_q_per_batch + i)),
    ]
    args = [lam_qk, subln_g.reshape(ATT_VD, 1), qt]
    seg_lens = []
    for k_arr, k_blk, k_col0, vt_arr, length in segs:
        in_specs.append(pl.BlockSpec((length, ATT_VD), lambda b, h, i, f=k_blk, c0=k_col0: (f(b), c0 + h)))
        in_specs.append(pl.BlockSpec((ATT_VD, length), lambda b, h, i, f=k_blk: (h, f(b))))
        args += [k_arr, vt_arr]
        seg_lens.append(length)
    l_tot = sum(seg_lens)
    return pl.pallas_call(
        functools.partial(_attn_kernel, seg_lens=tuple(seg_lens), tk=tk, lam_init=lam_init),
        out_shape=jax.ShapeDtypeStruct((out_rows, wd), BF16),
        grid=(cfg.batch, cfg.heads, n_q_per_batch),
        in_specs=in_specs,
        out_specs=pl.BlockSpec((bq, ATT_VD), lambda b, h, i: (b * n_q_per_batch + i, h)),
        scratch_shapes=[pltpu.VMEM((l_tot, bq), F32), pltpu.VMEM((l_tot, bq), F32),
                        pltpu.VMEM((ATT_VD, bq), F32), pltpu.VMEM((ATT_VD, bq), F32)],
        compiler_params=_params(3),
        name="diff_attention",
    )(*args)


SCAN_PAD = SUBLANES


def _scan_levels(n):
    plan = []
    while n > 1:
        fan = SUBLANES if n % SUBLANES == 0 else n
        assert fan <= SUBLANES, "sequence length must factor into groups of at most 8"
        plan.append((n, fan, n // fan))
        n //= fan
    return plan


def _scan_scratch_rows(n):
    rows = 0
    for _, _, parts in _scan_levels(n):
        rows += 2 * SCAN_PAD + -(-parts // SUBLANES) * SUBLANES
    return max(rows, SUBLANES)


def _scan_inplace(a_ref, b_ref, base, n, rev, sa_ref, sb_ref, off=0):
    if n == 1:
        return
    fan = SUBLANES if n % SUBLANES == 0 else n
    parts = n // fan
    order = list(range(fan))
    if rev:
        order.reverse()

    def rows(r):
        return pl.ds(base + r, parts, stride=fan) if parts > 1 else pl.ds(base + r, 1)

    acc_a = acc_b = None
    for r in order:
        a_r = a_ref[rows(r), :]
        b_r = b_ref[rows(r), :]
        if acc_a is None:
            acc_a, acc_b = a_r, b_r
        else:
            acc_b = a_r * acc_b + b_r
            acc_a = a_r * acc_a
            a_ref[rows(r), :] = acc_a
            b_ref[rows(r), :] = acc_b
    if parts == 1:
        return
    data = off + SCAN_PAD
    ones = jnp.ones((SCAN_PAD, LANES), F32)
    zeros = jnp.zeros((SCAN_PAD, LANES), F32)
    sa_ref[pl.ds(off, SCAN_PAD), :] = ones
    sb_ref[pl.ds(off, SCAN_PAD), :] = zeros
    sa_ref[pl.ds(data + parts, SCAN_PAD), :] = ones
    sb_ref[pl.ds(data + parts, SCAN_PAD), :] = zeros
    sa_ref[pl.ds(data, parts), :] = acc_a
    sb_ref[pl.ds(data, parts), :] = acc_b
    nxt = off + 2 * SCAN_PAD + -(-parts // SUBLANES) * SUBLANES
    _scan_inplace(sa_ref, sb_ref, data, parts, rev, sa_ref, sb_ref, nxt)
    shift = data + 1 if rev else data - 1
    carry_a = sa_ref[pl.ds(shift, parts), :]
    carry_b = sb_ref[pl.ds(shift, parts), :]
    for r in order:
        a_r = a_ref[rows(r), :]
        b_ref[rows(r), :] = b_ref[rows(r), :] + a_r * carry_b
        a_ref[rows(r), :] = a_r * carry_a


def _softplus(x):
    return jnp.maximum(x, 0.0) + jnp.log1p(jnp.exp(-jnp.abs(x)))


def _gelu_tanh(x):
    return 0.5 * x * (1.0 + jnp.tanh(math.sqrt(2.0 / math.pi) * (x + 0.044715 * (x * x * x))))


def _lru_kernel(lxl_ref, lyl_ref, lxc_ref, lyc_ref, cw_ref, cb_ref, wa_ref, ba_ref, wx_ref, bx_ref, lam_ref,
                ol_ref, oc_ref, pad_ref, a0_ref, b0_ref, a1_ref, b1_ref, sa_ref, sb_ref, *, tt):
    l_ctx = lxc_ref.shape[0]
    l_lat = lxl_ref.shape[0]
    cw = cw_ref[...]
    cb = cb_ref[...]
    ab_refs = ((a0_ref, b0_ref), (a1_ref, b1_ref))
    neg_sp = [-LRU_C * _softplus(-lam_ref[d:d + 1, :]) for d in range(2)]
    wa = [wa_ref[d].astype(BF16) for d in range(2)]
    wx = [wx_ref[d].astype(BF16) for d in range(2)]

    for lx_ref, base, length, is_ctx in ((lxc_ref, 0, l_ctx, True), (lxl_ref, l_ctx, l_lat, False)):
        ttk = min(tt, length)
        pad_ref[pl.ds(0, SUBLANES), :] = jnp.zeros((SUBLANES, LANES), F32)
        pad_ref[pl.ds(SUBLANES + length, SUBLANES), :] = jnp.zeros((SUBLANES, LANES), F32)

        def fill(ci, carry, lx_ref=lx_ref, ttk=ttk):
            r0 = pl.multiple_of(ci * ttk, ttk)
            pad_ref[pl.ds(SUBLANES + r0, ttk), :] = lx_ref[pl.ds(r0, ttk), :].astype(F32)
            return carry

        lax.fori_loop(0, length // ttk, fill, 0)

        def gates(ci, carry, base=base, length=length, is_ctx=is_ctx, ttk=ttk):
            r0 = pl.multiple_of(ci * ttk, ttk)
            xx = pad_ref[pl.ds(r0, ttk + 2 * SUBLANES), :]
            u = (cw[0:1, :] * xx[6:6 + ttk] + cw[1:2, :] * xx[7:7 + ttk]
                 + cw[2:3, :] * xx[8:8 + ttk] + cw[3:4, :] * xx[9:9 + ttk]) + cb
            ub = u.astype(BF16)
            row = r0 + lax.broadcasted_iota(jnp.int32, u.shape, 0)
            for d in range(2):
                r_gate = _sigmoid(_dot(ub, wa[d]) + ba_ref[d:d + 1, :])
                i_gate = _sigmoid(_dot(ub, wx[d]) + bx_ref[d:d + 1, :])
                log_a = neg_sp[d] * r_gate
                a = jnp.exp(log_a)
                mult = jnp.sqrt(-jnp.tanh(log_a) * (a * a + 1.0))
                if is_ctx:
                    mult = jnp.where(row == (length - 1 if d == 1 else 0), 1.0, mult)
                a_ref, b_ref = ab_refs[d]
                a_ref[pl.ds(base + r0, ttk), :] = a
                b_ref[pl.ds(base + r0, ttk), :] = mult * (i_gate * u)
            return carry

        lax.fori_loop(0, length // ttk, gates, 0)

    for d, rev in ((0, False), (1, True)):
        a_ref, b_ref = ab_refs[d]
        _scan_inplace(a_ref, b_ref, 0, l_ctx, rev, sa_ref, sb_ref)
        _scan_inplace(a_ref, b_ref, l_ctx, l_lat, rev, sa_ref, sb_ref)

    h0 = (b0_ref[pl.ds(l_ctx - 1, 1), :], b1_ref[pl.ds(0, 1), :])

    for ly_ref, o_ref, base, length, is_ctx in ((lyc_ref, oc_ref, 0, l_ctx, True), (lyl_ref, ol_ref, l_ctx, l_lat, False)):
        ttk = min(tt, length)

        def emit(ci, carry, ly_ref=ly_ref, o_ref=o_ref, base=base, is_ctx=is_ctx, ttk=ttk):
            r0 = pl.multiple_of(ci * ttk, ttk)
            h = b0_ref[pl.ds(base + r0, ttk), :] + b1_ref[pl.ds(base + r0, ttk), :]
            if not is_ctx:
                h = h + a0_ref[pl.ds(base + r0, ttk), :] * h0[0] + a1_ref[pl.ds(base + r0, ttk), :] * h0[1]
            o_ref[pl.ds(r0, ttk), :] = (h * _gelu_tanh(ly_ref[pl.ds(r0, ttk), :].astype(F32))).astype(o_ref.dtype)
            return carry

        lax.fori_loop(0, length // ttk, emit, 0)


def _lru(cfg, p, conv_w, conv_b, w_a, b_a, w_x, b_x, lam_p, tt=256):
    wd, s, lc = cfg.branch_w, cfg.seq, cfg.ctx_len
    nb = wd // LANES
    l_tot = s + lc
    ctx_blk0 = cfg.m_lat // lc
    lx_col, ly_col = 6 * nb, 7 * nb
    vec2 = pl.BlockSpec((2, LANES), lambda b, n: (0, n))
    wspec = pl.BlockSpec((2, None, LANES, LANES), lambda b, n: (0, n, 0, 0))
    scan_rows = max(_scan_scratch_rows(s), _scan_scratch_rows(lc))
    return pl.pallas_call(
        functools.partial(_lru_kernel, tt=tt),
        out_shape=(jax.ShapeDtypeStruct((cfg.m_lat, wd), BF16), jax.ShapeDtypeStruct((cfg.batch * lc, wd), BF16)),
        grid=(cfg.batch, nb),
        in_specs=[
            pl.BlockSpec((s, LANES), lambda b, n: (b, lx_col + n)),
            pl.BlockSpec((s, LANES), lambda b, n: (b, ly_col + n)),
            pl.BlockSpec((lc, LANES), lambda b, n: (ctx_blk0 + b, lx_col + n)),
            pl.BlockSpec((lc, LANES), lambda b, n: (ctx_blk0 + b, ly_col + n)),
            pl.BlockSpec((4, LANES), lambda b, n: (0, n)),
            pl.BlockSpec((1, LANES), lambda b, n: (0, n)),
            wspec, vec2, wspec, vec2, vec2,
        ],
        out_specs=(pl.BlockSpec((s, LANES), lambda b, n: (b, n)), pl.BlockSpec((lc, LANES), lambda b, n: (b, n))),
        scratch_shapes=[pltpu.VMEM((s + 2 * SUBLANES, LANES), F32)]
        + [pltpu.VMEM((l_tot, LANES), F32)] * 4
        + [pltpu.VMEM((scan_rows, LANES), F32)] * 2,
        compiler_params=_params(2),
        name="rglru",
    )(p, p, p, p, conv_w, conv_b.reshape(1, wd), w_a, b_a, w_x, b_x, lam_p)


def _merge_kernel(*refs, wd, n_lat_blk):
    branch_refs, (ga_ref, gb_ref, gc_ref, w_ref, o_ref, wbf_ref) = refs[:6], refs[6:]
    _cast_weight_block(w_ref, wbf_ref)
    is_lat = pl.program_id(1) < n_lat_blk
    acc = None
    for i, g_ref in enumerate((ga_ref, gb_ref, gc_ref)):
        x = jnp.where(is_lat, branch_refs[2 * i][...], branch_refs[2 * i + 1][...])
        g = _sigmoid(g_ref[...].astype(F32))
        term = g * _dot(x, wbf_ref[i * wd:(i + 1) * wd, :])
        acc = term if acc is None else acc + term
    o_ref[...] = acc.astype(o_ref.dtype)


def _merge(cfg, branches, p, wb, layer, bm, bn):
    m, wd, d = cfg.m_all, cfg.branch_w, cfg.d_model
    g_blk0 = 8 * wd // bn
    nd = d // bn
    n_lat_blk = cfg.m_lat // bm
    lat_spec = pl.BlockSpec((bm, wd), lambda j, i: (jnp.minimum(i, n_lat_blk - 1), 0))
    ctx_spec = pl.BlockSpec((bm, wd), lambda j, i: (jnp.maximum(i - n_lat_blk, 0), 0))
    gspec = lambda br: pl.BlockSpec((bm, bn), lambda j, i: (i, g_blk0 + br * nd + j))
    args = [a for pair in branches for a in pair]
    return pl.pallas_call(
        functools.partial(_merge_kernel, wd=wd, n_lat_blk=n_lat_blk),
        out_shape=jax.ShapeDtypeStruct((m, d), BF16),
        grid=(nd, m // bm),
        in_specs=[lat_spec, ctx_spec] * 3 + [gspec(0), gspec(1), gspec(2),
                  pl.BlockSpec((None, 3 * wd, bn), lambda j, i: (layer, 0, j))],
        out_specs=pl.BlockSpec((bm, bn), lambda j, i: (i, j)),
        scratch_shapes=[pltpu.VMEM((3 * wd, bn), BF16)],
        compiler_params=_params(2),
        name="merge_branches",
    )(*args, p, p, p, wb)


MOE_TILE = 256
MOE_BLOCK = 256


def _moe_slots(cfg):
    n_tiles = cfg.m_all * cfg.top_k // MOE_TILE + cfg.n_experts
    return n_tiles, n_tiles * MOE_TILE


def _assignment_index(t, top_k):
    per_row = LANES // top_k
    assert per_row * top_k == LANES and per_row & (per_row - 1) == 0
    shift = per_row.bit_length() - 1
    return lax.shift_right_logical(t, shift), (t & (per_row - 1)) * top_k


def _dispatch_kernel(fill_ref, nchunk_ref, nv_ref, pos_ref, xq_ref, xs_hbm, zero_ref, pos_smem, sem, idx_sem,
                     *, n_experts, top_k, n_tiles):
    i = pl.program_id(0)
    n_assign = pos_ref.shape[0] * pos_ref.shape[1]
    bt = n_assign // top_k

    @pl.when(i == 0)
    def _():
        zero_ref[...] = jnp.zeros_like(zero_ref)
        piece = zero_ref.at[pl.ds(0, SUBLANES)]
        total = 0
        for e in range(n_experts):
            def fill_piece(c, carry, e=e):
                start = pl.multiple_of(fill_ref[e] + c * SUBLANES, SUBLANES)
                pltpu.make_async_copy(piece, xs_hbm.at[pl.ds(start, SUBLANES)], sem).start()
                return carry

            lax.fori_loop(0, nchunk_ref[e], fill_piece, 0)
            total = total + nchunk_ref[e]

        def idle_tile(j, carry):
            start = pl.multiple_of(j * MOE_TILE, MOE_TILE)
            pltpu.make_async_copy(zero_ref, xs_hbm.at[pl.ds(start, MOE_TILE)], sem).start()
            return carry

        lax.fori_loop(nv_ref[0], n_tiles, idle_tile, 0)

        def drain_piece(c, carry):
            pltpu.make_async_copy(piece, xs_hbm.at[pl.ds(0, SUBLANES)], sem).wait()
            return carry

        lax.fori_loop(0, total, drain_piece, 0)

        def drain_tile(j, carry):
            pltpu.make_async_copy(zero_ref, xs_hbm.at[pl.ds(0, MOE_TILE)], sem).wait()
            return carry

        lax.fori_loop(nv_ref[0], n_tiles, drain_tile, 0)

    idx_copy = pltpu.make_async_copy(pos_ref, pos_smem, idx_sem)
    idx_copy.start()
    idx_copy.wait()

    def issue(t, carry):
        row, col = _assignment_index(t, top_k)
        for k in range(top_k):
            dst = pos_smem[row, col + k]
            pltpu.make_async_copy(xq_ref.at[pl.ds(t, 1)], xs_hbm.at[pl.ds(dst, 1)], sem).start()
        return carry

    lax.fori_loop(0, bt, issue, 0, unroll=2)
    pltpu.make_async_copy(xs_hbm.at[pl.ds(0, n_assign)], xs_hbm.at[pl.ds(0, n_assign)], sem).wait()


def _dispatch(cfg, fill, nchunk, n_valid, pos2d, xq):
    m, half = xq.shape
    n_tiles, slot_rows = _moe_slots(cfg)
    rows_per_step = MOE_BLOCK * cfg.top_k // LANES
    return pl.pallas_call(
        functools.partial(_dispatch_kernel, n_experts=cfg.n_experts, top_k=cfg.top_k, n_tiles=n_tiles),
        out_shape=jax.ShapeDtypeStruct((slot_rows, half), jnp.uint32),
        grid_spec=pltpu.PrefetchScalarGridSpec(
            num_scalar_prefetch=3,
            grid=(m // MOE_BLOCK,),
            in_specs=[pl.BlockSpec((rows_per_step, LANES), lambda i, *_: (i, 0)),
                      pl.BlockSpec((MOE_BLOCK, half), lambda i, *_: (i, 0))],
            out_specs=pl.BlockSpec(memory_space=pl.ANY),
            scratch_shapes=[pltpu.VMEM((MOE_TILE, half), jnp.uint32),
                            pltpu.SMEM((rows_per_step, LANES), jnp.int32),
                            pltpu.SemaphoreType.DMA(()), pltpu.SemaphoreType.DMA(())]),
        compiler_params=_params(1),
        name="moe_dispatch",
    )(fill, nchunk, n_valid, pos2d, xq)


def _expert_kernel(te_ref, nv_ref, xs_ref, perm_ref, wgu_ref, bgu_ref, wdn_ref, bdn_ref, ys_ref,
                   wgu_bf, wdn_bf, *, ff):
    j = pl.program_id(0)
    d = wgu_ref.shape[0]
    half = d // 2
    e = te_ref[j]
    e_prev = te_ref[jnp.maximum(j - 1, 0)]

    @pl.when((j == 0) | (e != e_prev))
    def _():
        rows = min(512, d)

        def cast_gu(c, carry):
            r0 = pl.multiple_of(c * rows, rows)
            w = wgu_ref[pl.ds(r0, rows), :].astype(BF16)
            wgu_bf[pl.ds(r0, rows), :] = _dot(w, perm_ref[...]).astype(BF16)
            return carry

        lax.fori_loop(0, d // rows, cast_gu, 0)
        rows_dn = 32

        def cast_dn(c, carry):
            r0 = pl.multiple_of(c * rows_dn, rows_dn)
            wdn_bf[pl.ds(r0, rows_dn), :] = wdn_ref[pl.ds(r0, rows_dn), :].astype(BF16)
            return carry

        lax.fori_loop(0, ff // rows_dn, cast_dn, 0)

    @pl.when(j < nv_ref[0])
    def _():
        lo, hi = _unpack_pair(xs_ref[...])
        h = (_dot(lo.astype(BF16), wgu_bf[pl.ds(0, half), :]) + _dot(hi.astype(BF16), wgu_bf[pl.ds(half, half), :])
             + bgu_ref[...])
        h_glu = jnp.minimum(h[:, :ff], SWIGLU_LIMIT)
        h_lin = jnp.clip(h[:, ff:], -SWIGLU_LIMIT, SWIGLU_LIMIT)
        act = (h_glu * _sigmoid(SWIGLU_ALPHA * h_glu) * (h_lin + 1.0)).astype(BF16)
        cw = min(512, half)
        for c in range(half // cw):
            lo_sl = slice(c * cw, (c + 1) * cw)
            hi_sl = slice(half + c * cw, half + (c + 1) * cw)
            y_lo = _dot(act, wdn_bf[:, lo_sl]) + bdn_ref[:, lo_sl]
            y_hi = _dot(act, wdn_bf[:, hi_sl]) + bdn_ref[:, hi_sl]
            ys_ref[:, lo_sl] = _pack_pair(y_lo, y_hi)

    @pl.when(j >= nv_ref[0])
    def _():
        ys_ref[...] = jnp.zeros_like(ys_ref)


def _experts(cfg, tile_e, n_valid, xs, perm, w_gu, b_gu_perm, w_dn, b_dn, layer):
    d, ff, ne = cfg.d_model, cfg.expert_ff, cfg.n_experts
    half = d // 2
    n_tiles, slot_rows = _moe_slots(cfg)
    tile_blk = lambda j, te, nv: (jnp.minimum(j, nv[0] - 1), 0)
    by_expert = lambda j, te, nv: (te[j], 0, 0)
    by_layer_expert = lambda j, te, nv: (layer, te[j], 0, 0)
    return pl.pallas_call(
        functools.partial(_expert_kernel, ff=ff),
        out_shape=jax.ShapeDtypeStruct((slot_rows, half), jnp.uint32),
        grid_spec=pltpu.PrefetchScalarGridSpec(
            num_scalar_prefetch=2,
            grid=(n_tiles,),
            in_specs=[
                pl.BlockSpec((MOE_TILE, half), tile_blk),
                pl.BlockSpec((2 * ff, 2 * ff), lambda j, te, nv: (0, 0)),
                pl.BlockSpec((None, None, d, 2 * ff), by_layer_expert),
                pl.BlockSpec((None, 1, 2 * ff), by_expert),
                pl.BlockSpec((None, None, ff, d), by_layer_expert),
                pl.BlockSpec((None, 1, d), by_expert),
            ],
            out_specs=pl.BlockSpec((MOE_TILE, half), lambda j, te, nv: (j, 0)),
            scratch_shapes=[pltpu.VMEM((d, 2 * ff), BF16), pltpu.VMEM((ff, d), BF16)]),
        compiler_params=_params(1),
        name="moe_experts",
    )(tile_e, n_valid, xs, perm, w_gu, b_gu_perm.reshape(ne, 1, 2 * ff), w_dn, b_dn.reshape(ne, 1, d))


def _combine_kernel(pos_ref, pos_next_ref, route_ref, res_ref, mod_ref, ys_hbm, o_ref, buf_ref, pos_smem,
                    sems, idx_sem, *, gate, top_k):
    i = pl.program_id(0)
    n = pl.num_programs(0)
    bt, d = res_ref.shape
    half = d // 2
    slot = lax.rem(i, 2)
    other = 1 - slot

    def load_positions(src_ref):
        idx_copy = pltpu.make_async_copy(src_ref, pos_smem, idx_sem)
        idx_copy.start()
        idx_copy.wait()

    def gather_token(t, to_slot):
        row, col = _assignment_index(t, top_k)
        for k in range(top_k):
            src = pos_smem[row, col + k]
            pltpu.make_async_copy(ys_hbm.at[pl.ds(src, 1)], buf_ref.at[to_slot, k, pl.ds(t, 1)],
                                  sems.at[to_slot]).start()

    def wait_slot(s):
        pltpu.make_async_copy(buf_ref.at[s], buf_ref.at[s], sems.at[s]).wait()

    @pl.when(i == 0)
    def _():
        load_positions(pos_ref)

        def first(t, carry):
            gather_token(t, 0)
            return carry

        lax.fori_loop(0, bt, first, 0, unroll=2)

    wait_slot(slot)
    load_positions(pos_next_ref)

    rb = 2 * SUBLANES
    cw = min(512, half)

    def rows(r, carry):
        r0 = pl.multiple_of(r * rb, rb)
        rs = pl.ds(r0, rb)
        for u in range(rb):
            gather_token(r0 + u, other)
        weights = [route_ref[rs, top_k + k:top_k + k + 1] for k in range(top_k)]
        for c in range(half // cw):
            lo_sl = slice(c * cw, (c + 1) * cw)
            hi_sl = slice(half + c * cw, half + (c + 1) * cw)
            acc_lo = acc_hi = None
            for k in range(top_k):
                lo, hi = _unpack_pair(buf_ref[slot, k, rs, lo_sl])
                acc_lo = weights[k] * lo if acc_lo is None else acc_lo + weights[k] * lo
                acc_hi = weights[k] * hi if acc_hi is None else acc_hi + weights[k] * hi
            o_ref[rs, lo_sl] = res_ref[rs, lo_sl] + mod_ref[0, gate:gate + 1, lo_sl] * acc_lo
            o_ref[rs, hi_sl] = res_ref[rs, hi_sl] + mod_ref[0, gate:gate + 1, hi_sl] * acc_hi
        return carry

    lax.fori_loop(0, bt // rb, rows, 0)

    @pl.when(i == n - 1)
    def _():
        wait_slot(other)


def _combine(cfg, pos2d, route, res, mod, ys, gate):
    m, d = res.shape
    half = d // 2
    ep = route.shape[1]
    bt = MOE_BLOCK
    rows_per_step = bt * cfg.top_k // LANES
    return pl.pallas_call(
        functools.partial(_combine_kernel, gate=gate, top_k=cfg.top_k),
        out_shape=jax.ShapeDtypeStruct((m, d), F32),
        grid=(m // bt,),
        in_specs=[
            pl.BlockSpec((rows_per_step, LANES), lambda i: (i, 0)),
            pl.BlockSpec((rows_per_step, LANES), lambda i: (jnp.minimum(i + 1, m // bt - 1), 0)),
            pl.BlockSpec((bt, ep), lambda i: (i, 0)),
            pl.BlockSpec((bt, d), lambda i: (i, 0)),
            pl.BlockSpec((1, N_MOD, d), lambda i: (_seg_of_block(cfg, i, bt), 0, 0)),
            pl.BlockSpec(memory_space=pl.ANY),
        ],
        out_specs=pl.BlockSpec((bt, d), lambda i: (i, 0)),
        scratch_shapes=[pltpu.VMEM((2, cfg.top_k, bt, half), jnp.uint32),
                        pltpu.SMEM((rows_per_step, LANES), jnp.int32),
                        pltpu.SemaphoreType.DMA((2,)), pltpu.SemaphoreType.DMA(())],
        compiler_params=_params(1),
        name="moe_combine",
    )(pos2d, pos2d, route, res, mod, ys)


def _moe_plan(cfg, route, counts8):
    k, ne = cfg.top_k, cfg.n_experts
    n_tiles, _ = _moe_slots(cfg)
    expert = route[:, :k].astype(jnp.int32)
    rank = route[:, 2 * k:3 * k].astype(jnp.int32)
    counts = counts8[0, :ne].astype(jnp.int32)
    padded = (counts + MOE_TILE - 1) // MOE_TILE * MOE_TILE
    region_end = jnp.cumsum(padded)
    region_start = region_end - padded
    pos = region_start[expert] + rank
    fill = (region_start + counts) // SUBLANES * SUBLANES
    nchunk = (region_end - fill) // SUBLANES
    n_valid = region_end[-1] // MOE_TILE
    tile_start = jnp.arange(n_tiles, dtype=jnp.int32) * MOE_TILE
    tile_e = jnp.sum(region_end[None, :] <= tile_start[:, None], axis=1)
    tile_e = jnp.minimum(tile_e, ne - 1).astype(jnp.int32)
    tile_e = tile_e[jnp.minimum(jnp.arange(n_tiles), n_valid - 1)]
    return (pos.reshape(-1, LANES), fill.astype(jnp.int32), nchunk.astype(jnp.int32), tile_e,
            n_valid.reshape(1).astype(jnp.int32))


def _deinterleave_perm(ff):
    col = jnp.arange(2 * ff)
    src = jnp.where(col < ff, 2 * col, 2 * (col - ff) + 1)
    return (jnp.arange(2 * ff)[:, None] == src[None, :]).astype(BF16)


def _forward(cfg, x, c, ctx, c_ctx, ada_a, ada_b, ada_bias, norm1_g, norm2_g, w_in, conv_a_w, lam_qk,
             subln_g, lru_conv_w, lru_conv_b, lru_w_a, lru_b_a, lru_w_x, lru_b_x, lru_lambda,
             w_branch, w_out, router_w, router_b, w_gu, b_gu, w_dn, b_dn, final_g):
    d, wd, bsz, s, lc = cfg.d_model, cfg.branch_w, cfg.batch, cfg.seq, cfg.ctx_len
    m_lat, m_all, ne = cfg.m_lat, cfg.m_all, cfg.n_experts
    assert bsz * lc <= s and s % lc == 0 and bsz + 1 <= SUBLANES
    bm = min(512, bsz * lc)
    bn = min(1024, d)
    bt = min(256, lc)
    ep = -(-ne // LANES) * LANES

    xs = jnp.concatenate([x.reshape(m_lat, d), ctx.reshape(bsz * lc, d)], axis=0)
    cond8 = jnp.zeros((SUBLANES, d), F32).at[:bsz].set(c).at[bsz].set(c_ctx)
    ctab, stab = _rope_tables(cfg, bt)
    ctx_blk = lambda b: m_lat // lc + b
    perm = _deinterleave_perm(cfg.expert_ff)

    for l in range(cfg.depth):
        lam_init = LAMBDA_INIT_BASE - LAMBDA_INIT_AMP * math.exp(-LAMBDA_INIT_RATE * l)
        mod = _ada(cfg, cond8, ada_a[l], ada_b[l], ada_bias[l])

        xn = _norm_mod(cfg, xs, norm1_g[l], mod, 0, 1)
        p = _matmul(xn, w_in, l, bm, bn)
        oa = (_conv_mixer(cfg, p, conv_a_w[l], s, 0, bsz, m_lat),
              _conv_mixer(cfg, p, conv_a_w[l], lc, m_lat // lc, bsz, bsz * lc))
        qt, kr, vt = _att_prep(cfg, p, ctab, stab, bt)
        nh = wd // ATT_VD
        ctx_seg = (p, ctx_blk, 4 * nh, vt, lc)
        lat_seg = (kr, lambda b: b, 0, vt, s)
        bq_lat = min(ATT_LATENT_BQ, s)
        ob_lat = _attention(cfg, lam_qk[l], subln_g[l], qt, [ctx_seg, lat_seg], 0, s // bq_lat, m_lat, lam_init,
                            bq=bq_lat)
        ob_ctx = _attention(cfg, lam_qk[l], subln_g[l], qt, [ctx_seg], m_lat // bt, lc // bt, bsz * lc, lam_init, bq=bt)
        oc = _lru(cfg, p, lru_conv_w[l], lru_conv_b[l], lru_w_a[l], lru_b_a[l],
                  lru_w_x[l], lru_b_x[l], lru_lambda[l])
        y = _merge(cfg, (oa, (ob_lat, ob_ctx), oc), p, w_branch, l, bm, bn)
        xs = _matmul_residual(cfg, y, w_out, l, xs, mod, 2, bm, bn)

        wr_pad = jnp.zeros((d, ep), F32).at[:, :ne].set(router_w[l])
        br_pad = jnp.full((1, ep), -jnp.inf, F32).at[0, :ne].set(router_b[l])
        xq, route, counts8 = _norm_router(cfg, xs, norm2_g[l], mod, 3, 4, wr_pad, br_pad)
        pos2d, fill, nchunk, tile_e, n_valid = _moe_plan(cfg, route, counts8)
        slots = _dispatch(cfg, fill, nchunk, n_valid, pos2d, xq)
        b_gu_perm = jnp.concatenate([b_gu[l][:, 0::2], b_gu[l][:, 1::2]], axis=-1)
        ys = _experts(cfg, tile_e, n_valid, slots, perm, w_gu, b_gu_perm, w_dn, b_dn[l], l)
        xs = _combine(cfg, pos2d, route, xs, mod, ys, 5)

    return _norm_plain(xs, final_g, m_lat).reshape(bsz, s, d)


def kernel(x, c, ctx, c_ctx, ada_a, ada_b, ada_bias, norm1_g, norm2_g, w_in, conv_a_w, lam_qk, subln_g, lru_conv_w, lru_conv_b, lru_w_a, lru_b_a, lru_w_x, lru_b_x, lru_lambda, w_branch, w_out, router_w, router_b, w_gu, b_gu, w_dn, b_dn, final_g):
    return _forward(_Cfg(), x, c, ctx, c_ctx, ada_a, ada_b, ada_bias, norm1_g, norm2_g, w_in, conv_a_w, lam_qk,
                    subln_g, lru_conv_w, lru_conv_b, lru_w_a, lru_b_a, lru_w_x, lru_b_x, lru_lambda,
                    w_branch, w_out, router_w, router_b, w_gu, b_gu, w_dn, b_dn, final_g)
```

```python
import functools
import math
from typing import NamedTuple

import jax
import jax.numpy as jnp
from jax import lax
from jax.experimental import pallas as pl
from jax.experimental.pallas import tpu as pltpu

F32 = jnp.float32
BF16 = jnp.bfloat16

LANES = 128
SUBLANES = 8
EPS = 1e-6
ATT_HD = 64
ATT_VD = 2 * ATT_HD
ATT_UNROLL = 8
ATT_LATENT_BQ = 512
ATT_Q_SCALE = ATT_HD ** -0.5 * math.log2(math.e)
ROPE_THETA = 10000.0
ROPE_AXIS_DIM = ATT_HD // 2
ROPE_PAIRS = ROPE_AXIS_DIM // 2
LAMBDA_INIT_BASE = 0.8
LAMBDA_INIT_AMP = 0.6
LAMBDA_INIT_RATE = 0.3
LRU_C = 8.0
SWIGLU_ALPHA = 1.702
SWIGLU_LIMIT = 7.0
N_MOD = 6
VMEM_LIMIT = 56 * 1024 * 1024


class _Cfg(NamedTuple):
    d_model: int = 4096
    batch: int = 2
    seq: int = 4096
    depth: int = 4
    grid_w: int = 64
    ctx_len: int = 256
    branch_w: int = 1024
    n_experts: int = 32
    top_k: int = 4
    expert_ff: int = 256

    @property
    def heads(self):
        return self.branch_w // ATT_VD

    @property
    def m_lat(self):
        return self.batch * self.seq

    @property
    def m_all(self):
        return self.batch * (self.seq + self.ctx_len)

    @property
    def in_cols(self):
        return 8 * self.branch_w + 3 * self.d_model


def _params(n_grid_dims):
    return pltpu.CompilerParams(
        dimension_semantics=("arbitrary",) * n_grid_dims, vmem_limit_bytes=VMEM_LIMIT)


def _dot(a, b):
    return jnp.dot(a, b, preferred_element_type=F32)


def _split_bf16(x):
    hi = x.astype(BF16)
    lo = (x - hi.astype(F32)).astype(BF16)
    return hi, lo


def _dot3(x, w):
    xh, xl = _split_bf16(x)
    wh, wl = _split_bf16(w)
    return _dot(xh, wh) + (_dot(xl, wh) + _dot(xh, wl))


def _sigmoid(x):
    return 1.0 / (1.0 + jnp.exp(-x))


def _ada_kernel(cond_ref, a_ref, b_ref, bias_ref, o_ref):
    cond = cond_ref[...]
    t = _dot3(cond * _sigmoid(cond), a_ref[...])
    o_ref[...] = _dot3(t, b_ref[...]) + bias_ref[...]


def _ada(cfg, cond8, a, b, bias):
    d = cfg.d_model
    rank = a.shape[1]
    out = pl.pallas_call(
        _ada_kernel,
        out_shape=jax.ShapeDtypeStruct((SUBLANES, N_MOD * d), F32),
        grid=(N_MOD,),
        in_specs=[
            pl.BlockSpec((SUBLANES, d), lambda j: (0, 0)),
            pl.BlockSpec((d, rank), lambda j: (0, 0)),
            pl.BlockSpec((rank, d), lambda j: (0, j)),
            pl.BlockSpec((1, d), lambda j: (0, j)),
        ],
        out_specs=pl.BlockSpec((SUBLANES, d), lambda j: (0, j)),
        compiler_params=_params(1),
        name="ada_modulation",
    )(cond8, a, b, bias.reshape(1, N_MOD * d))
    return out.reshape(SUBLANES, N_MOD, d)[: cfg.batch + 1]


def _rms(x, g):
    return x * lax.rsqrt(jnp.mean(x * x, axis=-1, keepdims=True) + EPS) * g


def _norm_mod_kernel(x_ref, g_ref, mod_ref, o_ref, *, shift, scale):
    y = _rms(x_ref[...], g_ref[...])
    y = y * (1.0 + mod_ref[0, scale:scale + 1, :]) + mod_ref[0, shift:shift + 1, :]
    o_ref[...] = y.astype(o_ref.dtype)


def _norm_plain_kernel(x_ref, g_ref, o_ref):
    o_ref[...] = _rms(x_ref[...], g_ref[...]).astype(o_ref.dtype)


def _pack_pair(lo, hi):
    lo_bits = lax.bitcast_convert_type(lo.astype(BF16).astype(F32), jnp.uint32)
    hi_bits = lax.bitcast_convert_type(hi.astype(BF16).astype(F32), jnp.uint32)
    return (lo_bits >> 16) | (hi_bits & jnp.uint32(0xFFFF0000))


def _unpack_pair(word):
    lo = lax.bitcast_convert_type(word << 16, F32)
    hi = lax.bitcast_convert_type(word & jnp.uint32(0xFFFF0000), F32)
    return lo, hi


def _norm_router_kernel(x_ref, g_ref, mod_ref, wr_ref, br_ref, xq_ref, route_ref, counts_ref, carry_ref,
                        *, shift, scale, top_k):
    @pl.when(pl.program_id(0) == 0)
    def _():
        carry_ref[...] = jnp.zeros_like(carry_ref)

    y = _rms(x_ref[...], g_ref[...])
    y = y * (1.0 + mod_ref[0, scale:scale + 1, :]) + mod_ref[0, shift:shift + 1, :]
    half = y.shape[1] // 2
    xq_ref[...] = _pack_pair(y[:, :half], y[:, half:])
    work = _dot3(y, wr_ref[...]) + br_ref[...]
    bm, n_lane = work.shape
    lane = lax.broadcasted_iota(jnp.int32, work.shape, 1).astype(F32)
    vals, hots, ids = [], [], []
    for _ in range(top_k):
        m = jnp.max(work, axis=-1, keepdims=True)
        idx = jnp.min(jnp.where(work == m, lane, float(n_lane)), axis=-1, keepdims=True)
        hot = lane == idx
        vals.append(m)
        hots.append(hot)
        ids.append(idx)
        work = jnp.where(hot, -jnp.inf, work)
    exps = [jnp.exp(v - vals[0]) for v in vals]
    denom = exps[0]
    for e in exps[1:]:
        denom = denom + e
    picked = jnp.zeros(work.shape, F32)
    for hot in hots:
        picked = jnp.where(hot, 1.0, picked)
    tri = (lax.broadcasted_iota(jnp.int32, (bm, bm), 0) > lax.broadcasted_iota(jnp.int32, (bm, bm), 1))
    before = _dot(jnp.where(tri, 1.0, 0.0).astype(BF16), picked.astype(BF16)) + carry_ref[0:1, :]
    carry_ref[...] = carry_ref[...] + jnp.sum(picked, axis=0, keepdims=True)
    counts_ref[...] = carry_ref[...]
    route = jnp.zeros(work.shape, F32)
    for k in range(top_k):
        rank = jnp.sum(jnp.where(hots[k], before, 0.0), axis=-1, keepdims=True)
        route = jnp.where(lane == float(k), ids[k], route)
        route = jnp.where(lane == float(top_k + k), exps[k] / denom, route)
        route = jnp.where(lane == float(2 * top_k + k), rank, route)
    route_ref[...] = route


def _seg_of_block(cfg, i, bm):
    return (i * bm) // cfg.seq


def _norm_mod(cfg, x, g, mod, shift, scale, bm=256):
    m, d = x.shape
    return pl.pallas_call(
        functools.partial(_norm_mod_kernel, shift=shift, scale=scale),
        out_shape=jax.ShapeDtypeStruct((m, d), BF16),
        grid=(m // bm,),
        in_specs=[
            pl.BlockSpec((bm, d), lambda i: (i, 0)),
            pl.BlockSpec((1, d), lambda i: (0, 0)),
            pl.BlockSpec((1, N_MOD, d), lambda i: (_seg_of_block(cfg, i, bm), 0, 0)),
        ],
        out_specs=pl.BlockSpec((bm, d), lambda i: (i, 0)),
        compiler_params=_params(1),
        name="norm_mod",
    )(x, g.reshape(1, d), mod)


def _norm_plain(x, g, m_rows, bm=256):
    d = x.shape[1]
    return pl.pallas_call(
        _norm_plain_kernel,
        out_shape=jax.ShapeDtypeStruct((m_rows, d), F32),
        grid=(m_rows // bm,),
        in_specs=[
            pl.BlockSpec((bm, d), lambda i: (i, 0)),
            pl.BlockSpec((1, d), lambda i: (0, 0)),
        ],
        out_specs=pl.BlockSpec((bm, d), lambda i: (i, 0)),
        compiler_params=_params(1),
        name="final_norm",
    )(x, g.reshape(1, d))


def _norm_router(cfg, x, g, mod, shift, scale, wr_pad, br_pad, bm=256):
    m, d = x.shape
    ep = wr_pad.shape[1]
    return pl.pallas_call(
        functools.partial(_norm_router_kernel, shift=shift, scale=scale, top_k=cfg.top_k),
        out_shape=(jax.ShapeDtypeStruct((m, d // 2), jnp.uint32), jax.ShapeDtypeStruct((m, ep), F32),
                   jax.ShapeDtypeStruct((SUBLANES, ep), F32)),
        grid=(m // bm,),
        in_specs=[
            pl.BlockSpec((bm, d), lambda i: (i, 0)),
            pl.BlockSpec((1, d), lambda i: (0, 0)),
            pl.BlockSpec((1, N_MOD, d), lambda i: (_seg_of_block(cfg, i, bm), 0, 0)),
            pl.BlockSpec((d, ep), lambda i: (0, 0)),
            pl.BlockSpec((1, ep), lambda i: (0, 0)),
        ],
        out_specs=(pl.BlockSpec((bm, d // 2), lambda i: (i, 0)), pl.BlockSpec((bm, ep), lambda i: (i, 0)),
                   pl.BlockSpec((SUBLANES, ep), lambda i: (0, 0))),
        scratch_shapes=[pltpu.VMEM((SUBLANES, ep), F32)],
        compiler_params=_params(1),
        name="norm_router",
    )(x, g.reshape(1, d), mod, wr_pad, br_pad)


CAST_ROWS = 256


def _cast_weight_block(w_ref, wbf_ref):
    @pl.when(pl.program_id(1) == 0)
    def _():
        rows = min(CAST_ROWS, w_ref.shape[0])

        def body(c, carry):
            r0 = pl.multiple_of(c * rows, rows)
            wbf_ref[pl.ds(r0, rows), :] = w_ref[pl.ds(r0, rows), :].astype(BF16)
            return carry

        lax.fori_loop(0, w_ref.shape[0] // rows, body, 0)


def _mm_kernel(x_ref, w_ref, o_ref, wbf_ref):
    _cast_weight_block(w_ref, wbf_ref)
    o_ref[...] = _dot(x_ref[...], wbf_ref[...]).astype(o_ref.dtype)


def _mm_res_kernel(x_ref, w_ref, res_ref, mod_ref, o_ref, wbf_ref, *, gate):
    _cast_weight_block(w_ref, wbf_ref)
    o_ref[...] = res_ref[...] + mod_ref[0, gate:gate + 1, :] * _dot(x_ref[...], wbf_ref[...])


def _matmul(x, w, layer, bm, bn, out_dtype=BF16):
    m, k = x.shape
    n = w.shape[2]
    return pl.pallas_call(
        _mm_kernel,
        out_shape=jax.ShapeDtypeStruct((m, n), out_dtype),
        grid=(n // bn, m // bm),
        in_specs=[
            pl.BlockSpec((bm, k), lambda j, i: (i, 0)),
            pl.BlockSpec((None, k, bn), lambda j, i: (layer, 0, j)),
        ],
        out_specs=pl.BlockSpec((bm, bn), lambda j, i: (i, j)),
        scratch_shapes=[pltpu.VMEM((k, bn), BF16)],
        compiler_params=_params(2),
        name="matmul",
    )(x, w)


def _matmul_residual(cfg, x, w, layer, res, mod, gate, bm, bn):
    m, k = x.shape
    n = w.shape[2]
    return pl.pallas_call(
        functools.partial(_mm_res_kernel, gate=gate),
        out_shape=jax.ShapeDtypeStruct((m, n), F32),
        grid=(n // bn, m // bm),
        in_specs=[
            pl.BlockSpec((bm, k), lambda j, i: (i, 0)),
            pl.BlockSpec((None, k, bn), lambda j, i: (layer, 0, j), pipeline_mode=pl.Buffered(1)),
            pl.BlockSpec((bm, bn), lambda j, i: (i, j)),
            pl.BlockSpec((1, N_MOD, bn), lambda j, i: (_seg_of_block(cfg, i, bm), 0, j)),
        ],
        out_specs=pl.BlockSpec((bm, bn), lambda j, i: (i, j)),
        scratch_shapes=[pltpu.VMEM((k, bn), BF16)],
        compiler_params=_params(2),
        name="matmul_residual",
    )(x, w, res, mod)


def _conv_a_kernel(b_ref, c_ref, h_ref, w_ref, o_ref, *, tt):
    t_len = b_ref.shape[0]
    w = w_ref[...]
    halo = 16

    def prod(r0, n):
        return c_ref[pl.ds(r0, n), :].astype(F32) * h_ref[pl.ds(r0, n), :].astype(F32)

    def chunk(ci, carry):
        r0 = pl.multiple_of(ci * tt, tt)
        u = prod(r0, tt)
        prev = prod(pl.multiple_of(jnp.maximum(r0 - halo, 0), halo), halo)[halo - 1:halo, :]
        prev = jnp.where(r0 == 0, 0.0, prev)
        nxt = prod(pl.multiple_of(jnp.minimum(r0 + tt, t_len - halo), halo), halo)[0:1, :]
        nxt = jnp.where(r0 + tt == t_len, 0.0, nxt)
        row = lax.broadcasted_iota(jnp.int32, u.shape, 0)
        up = jnp.where(row == 0, prev, pltpu.roll(u, 1, 0))
        dn = jnp.where(row == tt - 1, nxt, pltpu.roll(u, tt - 1, 0))
        conv = w[0:1, :] * up + w[1:2, :] * u + w[2:3, :] * dn
        o_ref[pl.ds(r0, tt), :] = (b_ref[pl.ds(r0, tt), :].astype(F32) * conv).astype(o_ref.dtype)
        return carry

    lax.fori_loop(0, t_len // tt, chunk, 0)


def _conv_mixer(cfg, p, w, t_len, row_blk0, n_seq, out_rows, bc=256):
    wd = cfg.branch_w
    ncb = wd // bc
    tt = min(256, t_len)
    spec = lambda off: pl.BlockSpec((t_len, bc), lambda s, j: (row_blk0 + s, off * ncb + j))
    return pl.pallas_call(
        functools.partial(_conv_a_kernel, tt=tt),
        out_shape=jax.ShapeDtypeStruct((out_rows, wd), BF16),
        grid=(n_seq, ncb),
        in_specs=[spec(0), spec(1), spec(2), pl.BlockSpec((3, bc), lambda s, j: (0, j))],
        out_specs=pl.BlockSpec((t_len, bc), lambda s, j: (s, j)),
        compiler_params=_params(2),
        name="conv_mixer",
    )(p, p, p, w)


def _rope_tables(cfg, bt):
    t = jnp.arange(cfg.seq)
    pos = jnp.stack([t // cfg.grid_w, t % cfg.grid_w], axis=-1).astype(F32)
    inv = ROPE_THETA ** (-jnp.arange(ROPE_PAIRS, dtype=F32) * 2.0 / ROPE_AXIS_DIM)
    ang = pos[:, :, None] * inv
    cos, sin = jnp.cos(ang), jnp.sin(ang)
    cmap = jnp.concatenate([cos[:, 0], cos[:, 0], cos[:, 1], cos[:, 1]], axis=-1)
    smap = jnp.concatenate([-sin[:, 0], sin[:, 0], -sin[:, 1], sin[:, 1]], axis=-1)
    ctab = jnp.concatenate([jnp.tile(cmap, (1, 2)), jnp.ones((bt, ATT_VD), F32)], axis=0)
    stab = jnp.concatenate([jnp.tile(smap, (1, 2)), jnp.zeros((bt, ATT_VD), F32)], axis=0)
    return ctab, stab


def _att_prep_kernel(q_ref, k_ref, v_ref, c_ref, s_ref, qt_ref, kr_ref, vt_ref, *, heads):
    cos = c_ref[...]
    sin = s_ref[...]
    lane = lax.broadcasted_iota(jnp.int32, cos.shape, 1)
    low = (lane % ROPE_AXIS_DIM) < ROPE_PAIRS

    def rope(u):
        swapped = jnp.where(low, pltpu.roll(u, ATT_VD - ROPE_PAIRS, 1), pltpu.roll(u, ROPE_PAIRS, 1))
        return u * cos + swapped * sin

    for h in range(heads):
        sl = slice(h * ATT_VD, (h + 1) * ATT_VD)
        qt_ref[sl, :] = (rope(q_ref[:, sl].astype(F32)) * ATT_Q_SCALE).T.astype(qt_ref.dtype)
        kr_ref[:, sl] = rope(k_ref[:, sl].astype(F32)).astype(kr_ref.dtype)
        vt_ref[sl, :] = v_ref[:, sl].astype(F32).T.astype(vt_ref.dtype)


def _att_prep(cfg, p, ctab, stab, bt=256):
    m, wd = cfg.m_all, cfg.branch_w
    n_lat_blk = cfg.m_lat // bt
    n_seq_blk = cfg.seq // bt
    tab_map = lambda i: (jnp.where(i < n_lat_blk, i % n_seq_blk, n_seq_blk), 0)
    return pl.pallas_call(
        functools.partial(_att_prep_kernel, heads=cfg.heads),
        out_shape=(jax.ShapeDtypeStruct((wd, m), BF16), jax.ShapeDtypeStruct((m, wd), BF16),
                   jax.ShapeDtypeStruct((wd, m), BF16)),
        grid=(m // bt,),
        in_specs=[
            pl.BlockSpec((bt, wd), lambda i: (i, 3)),
            pl.BlockSpec((bt, wd), lambda i: (i, 4)),
            pl.BlockSpec((bt, wd), lambda i: (i, 5)),
            pl.BlockSpec((bt, ATT_VD), tab_map),
            pl.BlockSpec((bt, ATT_VD), tab_map),
        ],
        out_specs=(pl.BlockSpec((wd, bt), lambda i: (0, i)), pl.BlockSpec((bt, wd), lambda i: (i, 0)),
                   pl.BlockSpec((wd, bt), lambda i: (0, i))),
        compiler_params=_params(1),
        name="att_prep",
    )(p, p, p, ctab, stab)


def _attn_kernel(lam_ref, g_ref, qt_ref, *refs, seg_lens, tk, lam_init):
    n_seg = len(seg_lens)
    kv = refs[:2 * n_seg]
    o_ref = refs[2 * n_seg]
    s1_ref, s2_ref, acc1_ref, acc2_ref = refs[2 * n_seg + 1:]
    bq = qt_ref.shape[1]

    qt = qt_ref[...].astype(F32)
    sub = lax.broadcasted_iota(jnp.int32, qt.shape, 0)
    q1 = jnp.where(sub < ATT_HD, qt, 0.0).astype(qt_ref.dtype)
    q2 = jnp.where(sub >= ATT_HD, qt, 0.0).astype(qt_ref.dtype)

    def chunks(seg):
        tkk = min(tk, seg_lens[seg])
        return tkk, seg_lens[seg] // tkk

    def pass_a(q, s_ref):
        m = jnp.full((1, bq), -jnp.inf, F32)
        off = 0
        for seg in range(n_seg):
            k_ref = kv[2 * seg]
            tkk, n_chunk = chunks(seg)

            def body(c, m, k_ref=k_ref, tkk=tkk, off=off):
                r0 = pl.multiple_of(c * tkk, tkk)
                s = _dot(k_ref[pl.ds(r0, tkk), :], q)
                s_ref[pl.ds(off + r0, tkk), :] = s
                return jnp.maximum(m, jnp.max(s, axis=0, keepdims=True))

            m = lax.fori_loop(0, n_chunk, body, m, unroll=min(ATT_UNROLL, n_chunk))
            off += seg_lens[seg]
        return m

    def pass_b(s_ref, m, acc_ref):
        acc_ref[...] = jnp.zeros_like(acc_ref)
        l = jnp.zeros((1, bq), F32)
        off = 0
        for seg in range(n_seg):
            vt_ref = kv[2 * seg + 1]
            tkk, n_chunk = chunks(seg)

            def body(c, l, vt_ref=vt_ref, tkk=tkk, off=off):
                r0 = pl.multiple_of(c * tkk, tkk)
                p = jnp.exp2(s_ref[pl.ds(off + r0, tkk), :] - m)
                vt = vt_ref[:, pl.ds(r0, tkk)]
                acc_ref[...] += _dot(vt, p.astype(vt.dtype))
                return l + jnp.sum(p, axis=0, keepdims=True)

            l = lax.fori_loop(0, n_chunk, body, l, unroll=min(ATT_UNROLL, n_chunk))
            off += seg_lens[seg]
        return l

    m1 = pass_a(q1, s1_ref)
    acc1_ref[...] = jnp.zeros_like(acc1_ref)
    l1 = jnp.zeros((1, bq), F32)
    m2 = jnp.full((1, bq), -jnp.inf, F32)
    off = 0
    for seg in range(n_seg):
        k_ref, vt_ref = kv[2 * seg], kv[2 * seg + 1]
        tkk, n_chunk = chunks(seg)
        for c in range(n_chunk):
            r0 = c * tkk
            s = _dot(k_ref[pl.ds(r0, tkk), :], q2)
            s2_ref[pl.ds(off + r0, tkk), :] = s
            m2 = jnp.maximum(m2, jnp.max(s, axis=0, keepdims=True))
            p = jnp.exp2(s1_ref[pl.ds(off + r0, tkk), :] - m1)
            vt = vt_ref[:, pl.ds(r0, tkk)]
            acc1_ref[...] += _dot(vt, p.astype(vt.dtype))
            l1 = l1 + jnp.sum(p, axis=0, keepdims=True)
        off += seg_lens[seg]
    l2 = pass_b(s2_ref, m2, acc2_ref)

    lq = lam_ref[...]
    lam = (jnp.exp(jnp.sum(lq[0:1] * lq[1:2], axis=-1, keepdims=True))
           - jnp.exp(jnp.sum(lq[2:3] * lq[3:4], axis=-1, keepdims=True)) + lam_init)
    ot = acc1_ref[...] * (1.0 / l1) - acc2_ref[...] * (lam / l2)
    ms = jnp.mean(ot * ot, axis=0, keepdims=True)
    y = ot * lax.rsqrt(ms + EPS) * g_ref[...] * (1.0 - lam_init)
    o_ref[...] = y.T.astype(o_ref.dtype)


def _attention(cfg, lam_qk, subln_g, qt, segs, q_col_blk0, n_q_per_batch, out_rows, lam_init, bq=256, tk=512):
    wd = cfg.branch_w
    in_specs = [
        pl.BlockSpec((4, ATT_HD), lambda b, h, i: (0, 0)),
        pl.BlockSpec((ATT_VD, 1), lambda b, h, i: (0, 0)),
        pl.BlockSpec((ATT_VD, bq), lambda b, h, i: (h, q_col_blk0 + b * n_q_per_batch + i)),
    ]
    args = [lam_qk, subln_g.reshape(ATT_VD, 1), qt]
    seg_lens = []
    for k_arr, k_blk, k_col0, vt_arr, length in segs:
        in_specs.append(pl.BlockSpec((length, ATT_VD), lambda b, h, i, f=k_blk, c0=k_col0: (f(b), c0 + h)))
        in_specs.append(pl.BlockSpec((ATT_VD, length), lambda b, h, i, f=k_blk: (h, f(b))))
        args += [k_arr, vt_arr]
        seg_lens.append(length)
    l_tot = sum(seg_lens)
    return pl.pallas_call(
        functools.partial(_attn_kernel, seg_lens=tuple(seg_lens), tk=tk, lam_init=lam_init),
        out_shape=jax.ShapeDtypeStruct((out_rows, wd), BF16),
        grid=(cfg.batch, cfg.heads, n_q_per_batch),
        in_specs=in_specs,
        out_specs=pl.BlockSpec((bq, ATT_VD), lambda b, h, i: (b * n_q_per_batch + i, h)),
        scratch_shapes=[pltpu.VMEM((l_tot, bq), F32), pltpu.VMEM((l_tot, bq), F32),
                        pltpu.VMEM((ATT_VD, bq), F32), pltpu.VMEM((ATT_VD, bq), F32)],
        compiler_params=_params(3),
        name="diff_attention",
    )(*args)


SCAN_PAD = SUBLANES


def _scan_levels(n):
    plan = []
    while n > 1:
        fan = SUBLANES if n % SUBLANES == 0 else n
        assert fan <= SUBLANES, "sequence length must factor into groups of at most 8"
        plan.append((n, fan, n // fan))
        n //= fan
    return plan


def _scan_scratch_rows(n):
    rows = 0
    for _, _, parts in _scan_levels(n):
        rows += 2 * SCAN_PAD + -(-parts // SUBLANES) * SUBLANES
    return max(rows, SUBLANES)


def _scan_inplace(a_ref, b_ref, base, n, rev, sa_ref, sb_ref, off=0):
    if n == 1:
        return
    fan = SUBLANES if n % SUBLANES == 0 else n
    parts = n // fan
    order = list(range(fan))
    if rev:
        order.reverse()

    def rows(r):
        return pl.ds(base + r, parts, stride=fan) if parts > 1 else pl.ds(base + r, 1)

    acc_a = acc_b = None
    for r in order:
        a_r = a_ref[rows(r), :]
        b_r = b_ref[rows(r), :]
        if acc_a is None:
            acc_a, acc_b = a_r, b_r
        else:
            acc_b = a_r * acc_b + b_r
            acc_a = a_r * acc_a
            a_ref[rows(r), :] = acc_a
            b_ref[rows(r), :] = acc_b
    if parts == 1:
        return
    data = off + SCAN_PAD
    ones = jnp.ones((SCAN_PAD, LANES), F32)
    zeros = jnp.zeros((SCAN_PAD, LANES), F32)
    sa_ref[pl.ds(off, SCAN_PAD), :] = ones
    sb_ref[pl.ds(off, SCAN_PAD), :] = zeros
    sa_ref[pl.ds(data + parts, SCAN_PAD), :] = ones
    sb_ref[pl.ds(data + parts, SCAN_PAD), :] = zeros
    sa_ref[pl.ds(data, parts), :] = acc_a
    sb_ref[pl.ds(data, parts), :] = acc_b
    nxt = off + 2 * SCAN_PAD + -(-parts // SUBLANES) * SUBLANES
    _scan_inplace(sa_ref, sb_ref, data, parts, rev, sa_ref, sb_ref, nxt)
    shift = data + 1 if rev else data - 1
    carry_a = sa_ref[pl.ds(shift, parts), :]
    carry_b = sb_ref[pl.ds(shift, parts), :]
    for r in order:
        a_r = a_ref[rows(r), :]
        b_ref[rows(r), :] = b_ref[rows(r), :] + a_r * carry_b
        a_ref[rows(r), :] = a_r * carry_a


def _softplus(x):
    return jnp.maximum(x, 0.0) + jnp.log1p(jnp.exp(-jnp.abs(x)))


def _gelu_tanh(x):
    return 0.5 * x * (1.0 + jnp.tanh(math.sqrt(2.0 / math.pi) * (x + 0.044715 * (x * x * x))))


def _lru_kernel(lxl_ref, lyl_ref, lxc_ref, lyc_ref, cw_ref, cb_ref, wa_ref, ba_ref, wx_ref, bx_ref, lam_ref,
                ol_ref, oc_ref, pad_ref, a0_ref, b0_ref, a1_ref, b1_ref, sa_ref, sb_ref, *, tt):
    l_ctx = lxc_ref.shape[0]
    l_lat = lxl_ref.shape[0]
    cw = cw_ref[...]
    cb = cb_ref[...]
    ab_refs = ((a0_ref, b0_ref), (a1_ref, b1_ref))
    neg_sp = [-LRU_C * _softplus(-lam_ref[d:d + 1, :]) for d in range(2)]
    wa = [wa_ref[d].astype(BF16) for d in range(2)]
    wx = [wx_ref[d].astype(BF16) for d in range(2)]

    for lx_ref, base, length, is_ctx in ((lxc_ref, 0, l_ctx, True), (lxl_ref, l_ctx, l_lat, False)):
        ttk = min(tt, length)
        pad_ref[pl.ds(0, SUBLANES), :] = jnp.zeros((SUBLANES, LANES), F32)
        pad_ref[pl.ds(SUBLANES + length, SUBLANES), :] = jnp.zeros((SUBLANES, LANES), F32)

        def fill(ci, carry, lx_ref=lx_ref, ttk=ttk):
            r0 = pl.multiple_of(ci * ttk, ttk)
            pad_ref[pl.ds(SUBLANES + r0, ttk), :] = lx_ref[pl.ds(r0, ttk), :].astype(F32)
            return carry

        lax.fori_loop(0, length // ttk, fill, 0)

        def gates(ci, carry, base=base, length=length, is_ctx=is_ctx, ttk=ttk):
            r0 = pl.multiple_of(ci * ttk, ttk)
            xx = pad_ref[pl.ds(r0, ttk + 2 * SUBLANES), :]
            u = (cw[0:1, :] * xx[6:6 + ttk] + cw[1:2, :] * xx[7:7 + ttk]
                 + cw[2:3, :] * xx[8:8 + ttk] + cw[3:4, :] * xx[9:9 + ttk]) + cb
            ub = u.astype(BF16)
            row = r0 + lax.broadcasted_iota(jnp.int32, u.shape, 0)
            for d in range(2):
                r_gate = _sigmoid(_dot(ub, wa[d]) + ba_ref[d:d + 1, :])
                i_gate = _sigmoid(_dot(ub, wx[d]) + bx_ref[d:d + 1, :])
                log_a = neg_sp[d] * r_gate
                a = jnp.exp(log_a)
                mult = jnp.sqrt(-jnp.tanh(log_a) * (a * a + 1.0))
                if is_ctx:
                    mult = jnp.where(row == (length - 1 if d == 1 else 0), 1.0, mult)
                a_ref, b_ref = ab_refs[d]
                a_ref[pl.ds(base + r0, ttk), :] = a
                b_ref[pl.ds(base + r0, ttk), :] = mult * (i_gate * u)
            return carry

        lax.fori_loop(0, length // ttk, gates, 0, unroll=min(2, length // ttk))

    for d, rev in ((0, False), (1, True)):
        a_ref, b_ref = ab_refs[d]
        _scan_inplace(a_ref, b_ref, 0, l_ctx, rev, sa_ref, sb_ref)
        _scan_inplace(a_ref, b_ref, l_ctx, l_lat, rev, sa_ref, sb_ref)

    h0 = (b0_ref[pl.ds(l_ctx - 1, 1), :], b1_ref[pl.ds(0, 1), :])

    for ly_ref, o_ref, base, length, is_ctx in ((lyc_ref, oc_ref, 0, l_ctx, True), (lyl_ref, ol_ref, l_ctx, l_lat, False)):
        ttk = min(tt, length)

        def emit(ci, carry, ly_ref=ly_ref, o_ref=o_ref, base=base, is_ctx=is_ctx, ttk=ttk):
            r0 = pl.multiple_of(ci * ttk, ttk)
            h = b0_ref[pl.ds(base + r0, ttk), :] + b1_ref[pl.ds(base + r0, ttk), :]
            if not is_ctx:
                h = h + a0_ref[pl.ds(base + r0, ttk), :] * h0[0] + a1_ref[pl.ds(base + r0, ttk), :] * h0[1]
            o_ref[pl.ds(r0, ttk), :] = (h * _gelu_tanh(ly_ref[pl.ds(r0, ttk), :].astype(F32))).astype(o_ref.dtype)
            return carry

        lax.fori_loop(0, length // ttk, emit, 0)


def _lru(cfg, p, conv_w, conv_b, w_a, b_a, w_x, b_x, lam_p, tt=256):
    wd, s, lc = cfg.branch_w, cfg.seq, cfg.ctx_len
    nb = wd // LANES
    l_tot = s + lc
    ctx_blk0 = cfg.m_lat // lc
    lx_col, ly_col = 6 * nb, 7 * nb
    vec2 = pl.BlockSpec((2, LANES), lambda b, n: (0, n))
    wspec = pl.BlockSpec((2, None, LANES, LANES), lambda b, n: (0, n, 0, 0))
    scan_rows = max(_scan_scratch_rows(s), _scan_scratch_rows(lc))
    return pl.pallas_call(
        functools.partial(_lru_kernel, tt=tt),
        out_shape=(jax.ShapeDtypeStruct((cfg.m_lat, wd), BF16), jax.ShapeDtypeStruct((cfg.batch * lc, wd), BF16)),
        grid=(cfg.batch, nb),
        in_specs=[
            pl.BlockSpec((s, LANES), lambda b, n: (b, lx_col + n)),
            pl.BlockSpec((s, LANES), lambda b, n: (b, ly_col + n)),
            pl.BlockSpec((lc, LANES), lambda b, n: (ctx_blk0 + b, lx_col + n)),
            pl.BlockSpec((lc, LANES), lambda b, n: (ctx_blk0 + b, ly_col + n)),
            pl.BlockSpec((4, LANES), lambda b, n: (0, n)),
            pl.BlockSpec((1, LANES), lambda b, n: (0, n)),
            wspec, vec2, wspec, vec2, vec2,
        ],
        out_specs=(pl.BlockSpec((s, LANES), lambda b, n: (b, n)), pl.BlockSpec((lc, LANES), lambda b, n: (b, n))),
        scratch_shapes=[pltpu.VMEM((s + 2 * SUBLANES, LANES), F32)]
        + [pltpu.VMEM((l_tot, LANES), F32)] * 4
        + [pltpu.VMEM((scan_rows, LANES), F32)] * 2,
        compiler_params=_params(2),
        name="rglru",
    )(p, p, p, p, conv_w, conv_b.reshape(1, wd), w_a, b_a, w_x, b_x, lam_p)


def _merge_kernel(*refs, wd, n_lat_blk):
    branch_refs, (ga_ref, gb_ref, gc_ref, w_ref, o_ref, wbf_ref) = refs[:6], refs[6:]
    _cast_weight_block(w_ref, wbf_ref)
    is_lat = pl.program_id(1) < n_lat_blk
    acc = None
    for i, g_ref in enumerate((ga_ref, gb_ref, gc_ref)):
        x = jnp.where(is_lat, branch_refs[2 * i][...], branch_refs[2 * i + 1][...])
        g = _sigmoid(g_ref[...].astype(F32))
        term = g * _dot(x, wbf_ref[i * wd:(i + 1) * wd, :])
        acc = term if acc is None else acc + term
    o_ref[...] = acc.astype(o_ref.dtype)


def _merge(cfg, branches, p, wb, layer, bm, bn):
    m, wd, d = cfg.m_all, cfg.branch_w, cfg.d_model
    g_blk0 = 8 * wd // bn
    nd = d // bn
    n_lat_blk = cfg.m_lat // bm
    lat_spec = pl.BlockSpec((bm, wd), lambda j, i: (jnp.minimum(i, n_lat_blk - 1), 0))
    ctx_spec = pl.BlockSpec((bm, wd), lambda j, i: (jnp.maximum(i - n_lat_blk, 0), 0))
    gspec = lambda br: pl.BlockSpec((bm, bn), lambda j, i: (i, g_blk0 + br * nd + j))
    args = [a for pair in branches for a in pair]
    return pl.pallas_call(
        functools.partial(_merge_kernel, wd=wd, n_lat_blk=n_lat_blk),
        out_shape=jax.ShapeDtypeStruct((m, d), BF16),
        grid=(nd, m // bm),
        in_specs=[lat_spec, ctx_spec] * 3 + [gspec(0), gspec(1), gspec(2),
                  pl.BlockSpec((None, 3 * wd, bn), lambda j, i: (layer, 0, j))],
        out_specs=pl.BlockSpec((bm, bn), lambda j, i: (i, j)),
        scratch_shapes=[pltpu.VMEM((3 * wd, bn), BF16)],
        compiler_params=_params(2),
        name="merge_branches",
    )(*args, p, p, p, wb)


MOE_TILE = 256
MOE_BLOCK = 256


def _moe_slots(cfg):
    n_tiles = cfg.m_all * cfg.top_k // MOE_TILE + cfg.n_experts
    return n_tiles, n_tiles * MOE_TILE


def _assignment_index(t, top_k):
    per_row = LANES // top_k
    assert per_row * top_k == LANES and per_row & (per_row - 1) == 0
    shift = per_row.bit_length() - 1
    return lax.shift_right_logical(t, shift), (t & (per_row - 1)) * top_k


def _dispatch_kernel(fill_ref, nchunk_ref, nv_ref, pos_ref, xq_ref, xs_hbm, zero_ref, pos_smem, sem, idx_sem,
                     *, n_experts, top_k, n_tiles):
    i = pl.program_id(0)
    n_assign = pos_ref.shape[0] * pos_ref.shape[1]
    bt = n_assign // top_k

    @pl.when(i == 0)
    def _():
        zero_ref[...] = jnp.zeros_like(zero_ref)
        piece = zero_ref.at[pl.ds(0, SUBLANES)]
        total = 0
        for e in range(n_experts):
            def fill_piece(c, carry, e=e):
                start = pl.multiple_of(fill_ref[e] + c * SUBLANES, SUBLANES)
                pltpu.make_async_copy(piece, xs_hbm.at[pl.ds(start, SUBLANES)], sem).start()
                return carry

            lax.fori_loop(0, nchunk_ref[e], fill_piece, 0)
            total = total + nchunk_ref[e]

        def idle_tile(j, carry):
            start = pl.multiple_of(j * MOE_TILE, MOE_TILE)
            pltpu.make_async_copy(zero_ref, xs_hbm.at[pl.ds(start, MOE_TILE)], sem).start()
            return carry

        lax.fori_loop(nv_ref[0], n_tiles, idle_tile, 0)

        def drain_piece(c, carry):
            pltpu.make_async_copy(piece, xs_hbm.at[pl.ds(0, SUBLANES)], sem).wait()
            return carry

        lax.fori_loop(0, total, drain_piece, 0)

        def drain_tile(j, carry):
            pltpu.make_async_copy(zero_ref, xs_hbm.at[pl.ds(0, MOE_TILE)], sem).wait()
            return carry

        lax.fori_loop(nv_ref[0], n_tiles, drain_tile, 0)

    idx_copy = pltpu.make_async_copy(pos_ref, pos_smem, idx_sem)
    idx_copy.start()
    idx_copy.wait()

    def issue(t, carry):
        row, col = _assignment_index(t, top_k)
        for k in range(top_k):
            dst = pos_smem[row, col + k]
            pltpu.make_async_copy(xq_ref.at[pl.ds(t, 1)], xs_hbm.at[pl.ds(dst, 1)], sem).start()
        return carry

    lax.fori_loop(0, bt, issue, 0, unroll=2)
    pltpu.make_async_copy(xs_hbm.at[pl.ds(0, n_assign)], xs_hbm.at[pl.ds(0, n_assign)], sem).wait()


def _dispatch(cfg, fill, nchunk, n_valid, pos2d, xq):
    m, half = xq.shape
    n_tiles, slot_rows = _moe_slots(cfg)
    rows_per_step = MOE_BLOCK * cfg.top_k // LANES
    return pl.pallas_call(
        functools.partial(_dispatch_kernel, n_experts=cfg.n_experts, top_k=cfg.top_k, n_tiles=n_tiles),
        out_shape=jax.ShapeDtypeStruct((slot_rows, half), jnp.uint32),
        grid_spec=pltpu.PrefetchScalarGridSpec(
            num_scalar_prefetch=3,
            grid=(m // MOE_BLOCK,),
            in_specs=[pl.BlockSpec((rows_per_step, LANES), lambda i, *_: (i, 0)),
                      pl.BlockSpec((MOE_BLOCK, half), lambda i, *_: (i, 0))],
            out_specs=pl.BlockSpec(memory_space=pl.ANY),
            scratch_shapes=[pltpu.VMEM((MOE_TILE, half), jnp.uint32),
                            pltpu.SMEM((rows_per_step, LANES), jnp.int32),
                            pltpu.SemaphoreType.DMA(()), pltpu.SemaphoreType.DMA(())]),
        compiler_params=_params(1),
        name="moe_dispatch",
    )(fill, nchunk, n_valid, pos2d, xq)


def _expert_kernel(te_ref, nv_ref, xs_ref, perm_ref, wgu_ref, bgu_ref, wdn_ref, bdn_ref, ys_ref,
                   wgu_bf, wdn_bf, *, ff):
    j = pl.program_id(0)
    d = wgu_ref.shape[0]
    half = d // 2
    e = te_ref[j]
    e_prev = te_ref[jnp.maximum(j - 1, 0)]

    @pl.when((j == 0) | (e != e_prev))
    def _():
        rows = min(512, d)

        def cast_gu(c, carry):
            r0 = pl.multiple_of(c * rows, rows)
            w = wgu_ref[pl.ds(r0, rows), :].astype(BF16)
            wgu_bf[pl.ds(r0, rows), :] = _dot(w, perm_ref[...]).astype(BF16)
            return carry

        lax.fori_loop(0, d // rows, cast_gu, 0, unroll=min(4, d // rows))
        rows_dn = 32

        def cast_dn(c, carry):
            r0 = pl.multiple_of(c * rows_dn, rows_dn)
            wdn_bf[pl.ds(r0, rows_dn), :] = wdn_ref[pl.ds(r0, rows_dn), :].astype(BF16)
            return carry

        lax.fori_loop(0, ff // rows_dn, cast_dn, 0)

    @pl.when(j < nv_ref[0])
    def _():
        lo, hi = _unpack_pair(xs_ref[...])
        h = (_dot(lo.astype(BF16), wgu_bf[pl.ds(0, half), :]) + _dot(hi.astype(BF16), wgu_bf[pl.ds(half, half), :])
             + bgu_ref[...])
        h_glu = jnp.minimum(h[:, :ff], SWIGLU_LIMIT)
        h_lin = jnp.clip(h[:, ff:], -SWIGLU_LIMIT, SWIGLU_LIMIT)
        act = (h_glu * _sigmoid(SWIGLU_ALPHA * h_glu) * (h_lin + 1.0)).astype(BF16)
        cw = min(512, half)
        for c in range(half // cw):
            lo_sl = slice(c * cw, (c + 1) * cw)
            hi_sl = slice(half + c * cw, half + (c + 1) * cw)
            y_lo = _dot(act, wdn_bf[:, lo_sl]) + bdn_ref[:, lo_sl]
            y_hi = _dot(act, wdn_bf[:, hi_sl]) + bdn_ref[:, hi_sl]
            ys_ref[:, lo_sl] = _pack_pair(y_lo, y_hi)

    @pl.when(j >= nv_ref[0])
    def _():
        ys_ref[...] = jnp.zeros_like(ys_ref)


def _experts(cfg, tile_e, n_valid, xs, perm, w_gu, b_gu_perm, w_dn, b_dn, layer):
    d, ff, ne = cfg.d_model, cfg.expert_ff, cfg.n_experts
    half = d // 2
    n_tiles, slot_rows = _moe_slots(cfg)
    tile_blk = lambda j, te, nv: (jnp.minimum(j, nv[0] - 1), 0)
    by_expert = lambda j, te, nv: (te[j], 0, 0)
    by_layer_expert = lambda j, te, nv: (layer, te[j], 0, 0)
    return pl.pallas_call(
        functools.partial(_expert_kernel, ff=ff),
        out_shape=jax.ShapeDtypeStruct((slot_rows, half), jnp.uint32),
        grid_spec=pltpu.PrefetchScalarGridSpec(
            num_scalar_prefetch=2,
            grid=(n_tiles,),
            in_specs=[
                pl.BlockSpec((MOE_TILE, half), tile_blk),
                pl.BlockSpec((2 * ff, 2 * ff), lambda j, te, nv: (0, 0)),
                pl.BlockSpec((None, None, d, 2 * ff), by_layer_expert),
                pl.BlockSpec((None, 1, 2 * ff), by_expert),
                pl.BlockSpec((None, None, ff, d), by_layer_expert),
                pl.BlockSpec((None, 1, d), by_expert),
            ],
            out_specs=pl.BlockSpec((MOE_TILE, half), lambda j, te, nv: (j, 0)),
            scratch_shapes=[pltpu.VMEM((d, 2 * ff), BF16), pltpu.VMEM((ff, d), BF16)]),
        compiler_params=_params(1),
        name="moe_experts",
    )(tile_e, n_valid, xs, perm, w_gu, b_gu_perm.reshape(ne, 1, 2 * ff), w_dn, b_dn.reshape(ne, 1, d))


def _combine_kernel(pos_ref, pos_next_ref, route_ref, res_ref, mod_ref, ys_hbm, o_ref, buf_ref, pos_smem,
                    sems, idx_sem, *, gate, top_k):
    i = pl.program_id(0)
    n = pl.num_programs(0)
    bt, d = res_ref.shape
    half = d // 2
    slot = lax.rem(i, 2)
    other = 1 - slot

    def load_positions(src_ref):
        idx_copy = pltpu.make_async_copy(src_ref, pos_smem, idx_sem)
        idx_copy.start()
        idx_copy.wait()

    def gather_token(t, to_slot):
        row, col = _assignment_index(t, top_k)
        for k in range(top_k):
            src = pos_smem[row, col + k]
            pltpu.make_async_copy(ys_hbm.at[pl.ds(src, 1)], buf_ref.at[to_slot, k, pl.ds(t, 1)],
                                  sems.at[to_slot]).start()

    def wait_slot(s):
        pltpu.make_async_copy(buf_ref.at[s], buf_ref.at[s], sems.at[s]).wait()

    @pl.when(i == 0)
    def _():
        load_positions(pos_ref)

        def first(t, carry):
            gather_token(t, 0)
            return carry

        lax.fori_loop(0, bt, first, 0, unroll=2)

    wait_slot(slot)
    load_positions(pos_next_ref)

    rb = 2 * SUBLANES
    cw = min(512, half)

    def rows(r, carry):
        r0 = pl.multiple_of(r * rb, rb)
        rs = pl.ds(r0, rb)
        for u in range(rb):
            gather_token(r0 + u, other)
        weights = [route_ref[rs, top_k + k:top_k + k + 1] for k in range(top_k)]
        for c in range(half // cw):
            lo_sl = slice(c * cw, (c + 1) * cw)
            hi_sl = slice(half + c * cw, half + (c + 1) * cw)
            acc_lo = acc_hi = None
            for k in range(top_k):
                lo, hi = _unpack_pair(buf_ref[slot, k, rs, lo_sl])
                acc_lo = weights[k] * lo if acc_lo is None else acc_lo + weights[k] * lo
                acc_hi = weights[k] * hi if acc_hi is None else acc_hi + weights[k] * hi
            o_ref[rs, lo_sl] = res_ref[rs, lo_sl] + mod_ref[0, gate:gate + 1, lo_sl] * acc_lo
            o_ref[rs, hi_sl] = res_ref[rs, hi_sl] + mod_ref[0, gate:gate + 1, hi_sl] * acc_hi
        return carry

    lax.fori_loop(0, bt // rb, rows, 0)

    @pl.when(i == n - 1)
    def _():
        wait_slot(other)


def _combine(cfg, pos2d, route, res, mod, ys, gate):
    m, d = res.shape
    half = d // 2
    ep = route.shape[1]
    bt = MOE_BLOCK
    rows_per_step = bt * cfg.top_k // LANES
    return pl.pallas_call(
        functools.partial(_combine_kernel, gate=gate, top_k=cfg.top_k),
        out_shape=jax.ShapeDtypeStruct((m, d), F32),
        grid=(m // bt,),
        in_specs=[
            pl.BlockSpec((rows_per_step, LANES), lambda i: (i, 0)),
            pl.BlockSpec((rows_per_step, LANES), lambda i: (jnp.minimum(i + 1, m // bt - 1), 0)),
            pl.BlockSpec((bt, ep), lambda i: (i, 0)),
            pl.BlockSpec((bt, d), lambda i: (i, 0)),
            pl.BlockSpec((1, N_MOD, d), lambda i: (_seg_of_block(cfg, i, bt), 0, 0)),
            pl.BlockSpec(memory_space=pl.ANY),
        ],
        out_specs=pl.BlockSpec((bt, d), lambda i: (i, 0)),
        scratch_shapes=[pltpu.VMEM((2, cfg.top_k, bt, half), jnp.uint32),
                        pltpu.SMEM((rows_per_step, LANES), jnp.int32),
                        pltpu.SemaphoreType.DMA((2,)), pltpu.SemaphoreType.DMA(())],
        compiler_params=_params(1),
        name="moe_combine",
    )(pos2d, pos2d, route, res, mod, ys)


def _moe_plan(cfg, route, counts8):
    k, ne = cfg.top_k, cfg.n_experts
    n_tiles, _ = _moe_slots(cfg)
    expert = route[:, :k].astype(jnp.int32)
    rank = route[:, 2 * k:3 * k].astype(jnp.int32)
    counts = counts8[0, :ne].astype(jnp.int32)
    padded = (counts + MOE_TILE - 1) // MOE_TILE * MOE_TILE
    region_end = jnp.cumsum(padded)
    region_start = region_end - padded
    pos = region_start[expert] + rank
    fill = (region_start + counts) // SUBLANES * SUBLANES
    nchunk = (region_end - fill) // SUBLANES
    n_valid = region_end[-1] // MOE_TILE
    tile_start = jnp.arange(n_tiles, dtype=jnp.int32) * MOE_TILE
    tile_e = jnp.sum(region_end[None, :] <= tile_start[:, None], axis=1)
    tile_e = jnp.minimum(tile_e, ne - 1).astype(jnp.int32)
    tile_e = tile_e[jnp.minimum(jnp.arange(n_tiles), n_valid - 1)]
    return (pos.reshape(-1, LANES), fill.astype(jnp.int32), nchunk.astype(jnp.int32), tile_e,
            n_valid.reshape(1).astype(jnp.int32))


def _deinterleave_perm(ff):
    col = jnp.arange(2 * ff)
    src = jnp.where(col < ff, 2 * col, 2 * (col - ff) + 1)
    return (jnp.arange(2 * ff)[:, None] == src[None, :]).astype(BF16)


def _forward(cfg, x, c, ctx, c_ctx, ada_a, ada_b, ada_bias, norm1_g, norm2_g, w_in, conv_a_w, lam_qk,
             subln_g, lru_conv_w, lru_conv_b, lru_w_a, lru_b_a, lru_w_x, lru_b_x, lru_lambda,
             w_branch, w_out, router_w, router_b, w_gu, b_gu, w_dn, b_dn, final_g):
    d, wd, bsz, s, lc = cfg.d_model, cfg.branch_w, cfg.batch, cfg.seq, cfg.ctx_len
    m_lat, m_all, ne = cfg.m_lat, cfg.m_all, cfg.n_experts
    assert bsz * lc <= s and s % lc == 0 and bsz + 1 <= SUBLANES
    bm = min(512, bsz * lc)
    bn = min(1024, d)
    bt = min(256, lc)
    ep = -(-ne // LANES) * LANES

    xs = jnp.concatenate([x.reshape(m_lat, d), ctx.reshape(bsz * lc, d)], axis=0)
    cond8 = jnp.zeros((SUBLANES, d), F32).at[:bsz].set(c).at[bsz].set(c_ctx)
    ctab, stab = _rope_tables(cfg, bt)
    ctx_blk = lambda b: m_lat // lc + b
    perm = _deinterleave_perm(cfg.expert_ff)

    for l in range(cfg.depth):
        lam_init = LAMBDA_INIT_BASE - LAMBDA_INIT_AMP * math.exp(-LAMBDA_INIT_RATE * l)
        mod = _ada(cfg, cond8, ada_a[l], ada_b[l], ada_bias[l])

        xn = _norm_mod(cfg, xs, norm1_g[l], mod, 0, 1)
        p = _matmul(xn, w_in, l, bm, bn)
        oa = (_conv_mixer(cfg, p, conv_a_w[l], s, 0, bsz, m_lat),
              _conv_mixer(cfg, p, conv_a_w[l], lc, m_lat // lc, bsz, bsz * lc))
        qt, kr, vt = _att_prep(cfg, p, ctab, stab, bt)
        nh = wd // ATT_VD
        ctx_seg = (p, ctx_blk, 4 * nh, vt, lc)
        lat_seg = (kr, lambda b: b, 0, vt, s)
        bq_lat = min(ATT_LATENT_BQ, s)
        ob_lat = _attention(cfg, lam_qk[l], subln_g[l], qt, [ctx_seg, lat_seg], 0, s // bq_lat, m_lat, lam_init,
                            bq=bq_lat)
        ob_ctx = _attention(cfg, lam_qk[l], subln_g[l], qt, [ctx_seg], m_lat // bt, lc // bt, bsz * lc, lam_init, bq=bt)
        oc = _lru(cfg, p, lru_conv_w[l], lru_conv_b[l], lru_w_a[l], lru_b_a[l],
                  lru_w_x[l], lru_b_x[l], lru_lambda[l])
        y = _merge(cfg, (oa, (ob_lat, ob_ctx), oc), p, w_branch, l, bm, bn)
        xs = _matmul_residual(cfg, y, w_out, l, xs, mod, 2, bm, bn)

        wr_pad = jnp.zeros((d, ep), F32).at[:, :ne].set(router_w[l])
        br_pad = jnp.full((1, ep), -jnp.inf, F32).at[0, :ne].set(router_b[l])
        xq, route, counts8 = _norm_router(cfg, xs, norm2_g[l], mod, 3, 4, wr_pad, br_pad)
        pos2d, fill, nchunk, tile_e, n_valid = _moe_plan(cfg, route, counts8)
        slots = _dispatch(cfg, fill, nchunk, n_valid, pos2d, xq)
        b_gu_perm = jnp.concatenate([b_gu[l][:, 0::2], b_gu[l][:, 1::2]], axis=-1)
        ys = _experts(cfg, tile_e, n_valid, slots, perm, w_gu, b_gu_perm, w_dn, b_dn[l], l)
        xs = _combine(cfg, pos2d, route, xs, mod, ys, 5)

    return _norm_plain(xs, final_g, m_lat).reshape(bsz, s, d)


def kernel(x, c, ctx, c_ctx, ada_a, ada_b, ada_bias, norm1_g, norm2_g, w_in, conv_a_w, lam_qk, subln_g, lru_conv_w, lru_conv_b, lru_w_a, lru_b_a, lru_w_x, lru_b_x, lru_lambda, w_branch, w_out, router_w, router_b, w_gu, b_gu, w_dn, b_dn, final_g):
    return _forward(_Cfg(), x, c, ctx, c_ctx, ada_a, ada_b, ada_bias, norm1_g, norm2_g, w_in, conv_a_w, lam_qk,
                    subln_g, lru_conv_w, lru_conv_b, lru_w_a, lru_b_a, lru_w_x, lru_b_x, lru_lambda,
                    w_branch, w_out, router_w, router_b, w_gu, b_gu, w_dn, b_dn, final_g)
```

```python
import functools
import math
from typing import NamedTuple

import jax
import jax.numpy as jnp
from jax import lax
from jax.experimental import pallas as pl
from jax.experimental.pallas import tpu as pltpu

F32 = jnp.float32
BF16 = jnp.bfloat16

LANES = 128
SUBLANES = 8
EPS = 1e-6
ATT_HD = 64
ATT_VD = 2 * ATT_HD
ATT_UNROLL = 8
ATT_LATENT_BQ = 512
ATT_Q_SCALE = ATT_HD ** -0.5 * math.log2(math.e)
ROPE_THETA = 10000.0
ROPE_AXIS_DIM = ATT_HD // 2
ROPE_PAIRS = ROPE_AXIS_DIM // 2
LAMBDA_INIT_BASE = 0.8
LAMBDA_INIT_AMP = 0.6
LAMBDA_INIT_RATE = 0.3
LRU_C = 8.0
SWIGLU_ALPHA = 1.702
SWIGLU_LIMIT = 7.0
N_MOD = 6
VMEM_LIMIT = 56 * 1024 * 1024


class _Cfg(NamedTuple):
    d_model: int = 4096
    batch: int = 2
    seq: int = 4096
    depth: int = 4
    grid_w: int = 64
    ctx_len: int = 256
    branch_w: int = 1024
    n_experts: int = 32
    top_k: int = 4
    expert_ff: int = 256

    @property
    def heads(self):
        return self.branch_w // ATT_VD

    @property
    def m_lat(self):
        return self.batch * self.seq

    @property
    def m_all(self):
        return self.batch * (self.seq + self.ctx_len)

    @property
    def in_cols(self):
        return 8 * self.branch_w + 3 * self.d_model


def _params(n_grid_dims):
    return pltpu.CompilerParams(
        dimension_semantics=("arbitrary",) * n_grid_dims, vmem_limit_bytes=VMEM_LIMIT)


def _dot(a, b):
    return jnp.dot(a, b, preferred_element_type=F32)


def _split_bf16(x):
    hi = x.astype(BF16)
    lo = (x - hi.astype(F32)).astype(BF16)
    return hi, lo


def _dot3(x, w):
    xh, xl = _split_bf16(x)
    wh, wl = _split_bf16(w)
    return _dot(xh, wh) + (_dot(xl, wh) + _dot(xh, wl))


def _sigmoid(x):
    return 1.0 / (1.0 + jnp.exp(-x))


def _ada_kernel(cond_ref, a_ref, b_ref, bias_ref, o_ref):
    cond = cond_ref[...]
    t = _dot3(cond * _sigmoid(cond), a_ref[...])
    o_ref[...] = _dot3(t, b_ref[...]) + bias_ref[...]


def _ada(cfg, cond8, a, b, bias):
    d = cfg.d_model
    rank = a.shape[1]
    out = pl.pallas_call(
        _ada_kernel,
        out_shape=jax.ShapeDtypeStruct((SUBLANES, N_MOD * d), F32),
        grid=(N_MOD,),
        in_specs=[
            pl.BlockSpec((SUBLANES, d), lambda j: (0, 0)),
            pl.BlockSpec((d, rank), lambda j: (0, 0)),
            pl.BlockSpec((rank, d), lambda j: (0, j)),
            pl.BlockSpec((1, d), lambda j: (0, j)),
        ],
        out_specs=pl.BlockSpec((SUBLANES, d), lambda j: (0, j)),
        compiler_params=_params(1),
        name="ada_modulation",
    )(cond8, a, b, bias.reshape(1, N_MOD * d))
    return out.reshape(SUBLANES, N_MOD, d)[: cfg.batch + 1]


def _rms(x, g):
    return x * lax.rsqrt(jnp.mean(x * x, axis=-1, keepdims=True) + EPS) * g


def _norm_mod_kernel(x_ref, g_ref, mod_ref, o_ref, *, shift, scale):
    y = _rms(x_ref[...], g_ref[...])
    y = y * (1.0 + mod_ref[0, scale:scale + 1, :]) + mod_ref[0, shift:shift + 1, :]
    o_ref[...] = y.astype(o_ref.dtype)


def _norm_plain_kernel(x_ref, g_ref, o_ref):
    o_ref[...] = _rms(x_ref[...], g_ref[...]).astype(o_ref.dtype)


def _pack_pair(lo, hi):
    lo_bits = lax.bitcast_convert_type(lo.astype(BF16).astype(F32), jnp.uint32)
    hi_bits = lax.bitcast_convert_type(hi.astype(BF16).astype(F32), jnp.uint32)
    return (lo_bits >> 16) | (hi_bits & jnp.uint32(0xFFFF0000))


def _unpack_pair(word):
    lo = lax.bitcast_convert_type(word << 16, F32)
    hi = lax.bitcast_convert_type(word & jnp.uint32(0xFFFF0000), F32)
    return lo, hi


def _norm_router_kernel(x_ref, g_ref, mod_ref, wr_ref, br_ref, xq_ref, route_ref, counts_ref, carry_ref,
                        *, shift, scale, top_k):
    @pl.when(pl.program_id(0) == 0)
    def _():
        carry_ref[...] = jnp.zeros_like(carry_ref)

    y = _rms(x_ref[...], g_ref[...])
    y = y * (1.0 + mod_ref[0, scale:scale + 1, :]) + mod_ref[0, shift:shift + 1, :]
    half = y.shape[1] // 2
    xq_ref[...] = _pack_pair(y[:, :half], y[:, half:])
    work = _dot3(y, wr_ref[...]) + br_ref[...]
    bm, n_lane = work.shape
    lane = lax.broadcasted_iota(jnp.int32, work.shape, 1).astype(F32)
    vals, hots, ids = [], [], []
    for _ in range(top_k):
        m = jnp.max(work, axis=-1, keepdims=True)
        idx = jnp.min(jnp.where(work == m, lane, float(n_lane)), axis=-1, keepdims=True)
        hot = lane == idx
        vals.append(m)
        hots.append(hot)
        ids.append(idx)
        work = jnp.where(hot, -jnp.inf, work)
    exps = [jnp.exp(v - vals[0]) for v in vals]
    denom = exps[0]
    for e in exps[1:]:
        denom = denom + e
    picked = jnp.zeros(work.shape, F32)
    for hot in hots:
        picked = jnp.where(hot, 1.0, picked)
    tri = (lax.broadcasted_iota(jnp.int32, (bm, bm), 0) > lax.broadcasted_iota(jnp.int32, (bm, bm), 1))
    before = _dot(jnp.where(tri, 1.0, 0.0).astype(BF16), picked.astype(BF16)) + carry_ref[0:1, :]
    carry_ref[...] = carry_ref[...] + jnp.sum(picked, axis=0, keepdims=True)
    counts_ref[...] = carry_ref[...]
    route = jnp.zeros(work.shape, F32)
    for k in range(top_k):
        rank = jnp.sum(jnp.where(hots[k], before, 0.0), axis=-1, keepdims=True)
        route = jnp.where(lane == float(k), ids[k], route)
        route = jnp.where(lane == float(top_k + k), exps[k] / denom, route)
        route = jnp.where(lane == float(2 * top_k + k), rank, route)
    route_ref[...] = route


def _seg_of_block(cfg, i, bm):
    return (i * bm) // cfg.seq


def _norm_mod(cfg, x, g, mod, shift, scale, bm=256):
    m, d = x.shape
    return pl.pallas_call(
        functools.partial(_norm_mod_kernel, shift=shift, scale=scale),
        out_shape=jax.ShapeDtypeStruct((m, d), BF16),
        grid=(m // bm,),
        in_specs=[
            pl.BlockSpec((bm, d), lambda i: (i, 0)),
            pl.BlockSpec((1, d), lambda i: (0, 0)),
            pl.BlockSpec((1, N_MOD, d), lambda i: (_seg_of_block(cfg, i, bm), 0, 0)),
        ],
        out_specs=pl.BlockSpec((bm, d), lambda i: (i, 0)),
        compiler_params=_params(1),
        name="norm_mod",
    )(x, g.reshape(1, d), mod)


def _norm_plain(x, g, m_rows, bm=256):
    d = x.shape[1]
    return pl.pallas_call(
        _norm_plain_kernel,
        out_shape=jax.ShapeDtypeStruct((m_rows, d), F32),
        grid=(m_rows // bm,),
        in_specs=[
            pl.BlockSpec((bm, d), lambda i: (i, 0)),
            pl.BlockSpec((1, d), lambda i: (0, 0)),
        ],
        out_specs=pl.BlockSpec((bm, d), lambda i: (i, 0)),
        compiler_params=_params(1),
        name="final_norm",
    )(x, g.reshape(1, d))


def _norm_router(cfg, x, g, mod, shift, scale, wr_pad, br_pad, bm=256):
    m, d = x.shape
    ep = wr_pad.shape[1]
    return pl.pallas_call(
        functools.partial(_norm_router_kernel, shift=shift, scale=scale, top_k=cfg.top_k),
        out_shape=(jax.ShapeDtypeStruct((m, d // 2), jnp.uint32), jax.ShapeDtypeStruct((m, ep), F32),
                   jax.ShapeDtypeStruct((SUBLANES, ep), F32)),
        grid=(m // bm,),
        in_specs=[
            pl.BlockSpec((bm, d), lambda i: (i, 0)),
            pl.BlockSpec((1, d), lambda i: (0, 0)),
            pl.BlockSpec((1, N_MOD, d), lambda i: (_seg_of_block(cfg, i, bm), 0, 0)),
            pl.BlockSpec((d, ep), lambda i: (0, 0)),
            pl.BlockSpec((1, ep), lambda i: (0, 0)),
        ],
        out_specs=(pl.BlockSpec((bm, d // 2), lambda i: (i, 0)), pl.BlockSpec((bm, ep), lambda i: (i, 0)),
                   pl.BlockSpec((SUBLANES, ep), lambda i: (0, 0))),
        scratch_shapes=[pltpu.VMEM((SUBLANES, ep), F32)],
        compiler_params=_params(1),
        name="norm_router",
    )(x, g.reshape(1, d), mod, wr_pad, br_pad)


CAST_ROWS = 256


def _cast_weight_block(w_ref, wbf_ref):
    @pl.when(pl.program_id(1) == 0)
    def _():
        rows = min(CAST_ROWS, w_ref.shape[0])

        def body(c, carry):
            r0 = pl.multiple_of(c * rows, rows)
            wbf_ref[pl.ds(r0, rows), :] = w_ref[pl.ds(r0, rows), :].astype(BF16)
            return carry

        lax.fori_loop(0, w_ref.shape[0] // rows, body, 0)


def _mm_kernel(x_ref, w_ref, o_ref, wbf_ref):
    _cast_weight_block(w_ref, wbf_ref)
    o_ref[...] = _dot(x_ref[...], wbf_ref[...]).astype(o_ref.dtype)


def _mm_res_kernel(x_ref, w_ref, res_ref, mod_ref, o_ref, wbf_ref, *, gate):
    _cast_weight_block(w_ref, wbf_ref)
    o_ref[...] = res_ref[...] + mod_ref[0, gate:gate + 1, :] * _dot(x_ref[...], wbf_ref[...])


def _matmul(x, w, layer, bm, bn, out_dtype=BF16):
    m, k = x.shape
    n = w.shape[2]
    return pl.pallas_call(
        _mm_kernel,
        out_shape=jax.ShapeDtypeStruct((m, n), out_dtype),
        grid=(n // bn, m // bm),
        in_specs=[
            pl.BlockSpec((bm, k), lambda j, i: (i, 0)),
            pl.BlockSpec((None, k, bn), lambda j, i: (layer, 0, j)),
        ],
        out_specs=pl.BlockSpec((bm, bn), lambda j, i: (i, j)),
        scratch_shapes=[pltpu.VMEM((k, bn), BF16)],
        compiler_params=_params(2),
        name="matmul",
    )(x, w)


def _matmul_residual(cfg, x, w, layer, res, mod, gate, bm, bn):
    m, k = x.shape
    n = w.shape[2]
    return pl.pallas_call(
        functools.partial(_mm_res_kernel, gate=gate),
        out_shape=jax.ShapeDtypeStruct((m, n), F32),
        grid=(n // bn, m // bm),
        in_specs=[
            pl.BlockSpec((bm, k), lambda j, i: (i, 0)),
            pl.BlockSpec((None, k, bn), lambda j, i: (layer, 0, j), pipeline_mode=pl.Buffered(1)),
            pl.BlockSpec((bm, bn), lambda j, i: (i, j)),
            pl.BlockSpec((1, N_MOD, bn), lambda j, i: (_seg_of_block(cfg, i, bm), 0, j)),
        ],
        out_specs=pl.BlockSpec((bm, bn), lambda j, i: (i, j)),
        scratch_shapes=[pltpu.VMEM((k, bn), BF16)],
        compiler_params=_params(2),
        name="matmul_residual",
    )(x, w, res, mod)


def _conv_a_kernel(b_ref, c_ref, h_ref, w_ref, o_ref, *, tt):
    t_len = b_ref.shape[0]
    w = w_ref[...]
    halo = 16

    def prod(r0, n):
        return c_ref[pl.ds(r0, n), :].astype(F32) * h_ref[pl.ds(r0, n), :].astype(F32)

    def chunk(ci, carry):
        r0 = pl.multiple_of(ci * tt, tt)
        u = prod(r0, tt)
        prev = prod(pl.multiple_of(jnp.maximum(r0 - halo, 0), halo), halo)[halo - 1:halo, :]
        prev = jnp.where(r0 == 0, 0.0, prev)
        nxt = prod(pl.multiple_of(jnp.minimum(r0 + tt, t_len - halo), halo), halo)[0:1, :]
        nxt = jnp.where(r0 + tt == t_len, 0.0, nxt)
        row = lax.broadcasted_iota(jnp.int32, u.shape, 0)
        up = jnp.where(row == 0, prev, pltpu.roll(u, 1, 0))
        dn = jnp.where(row == tt - 1, nxt, pltpu.roll(u, tt - 1, 0))
        conv = w[0:1, :] * up + w[1:2, :] * u + w[2:3, :] * dn
        o_ref[pl.ds(r0, tt), :] = (b_ref[pl.ds(r0, tt), :].astype(F32) * conv).astype(o_ref.dtype)
        return carry

    lax.fori_loop(0, t_len // tt, chunk, 0)


def _conv_mixer(cfg, p, w, t_len, row_blk0, n_seq, out_rows, bc=256):
    wd = cfg.branch_w
    ncb = wd // bc
    tt = min(256, t_len)
    spec = lambda off: pl.BlockSpec((t_len, bc), lambda s, j: (row_blk0 + s, off * ncb + j))
    return pl.pallas_call(
        functools.partial(_conv_a_kernel, tt=tt),
        out_shape=jax.ShapeDtypeStruct((out_rows, wd), BF16),
        grid=(n_seq, ncb),
        in_specs=[spec(0), spec(1), spec(2), pl.BlockSpec((3, bc), lambda s, j: (0, j))],
        out_specs=pl.BlockSpec((t_len, bc), lambda s, j: (s, j)),
        compiler_params=_params(2),
        name="conv_mixer",
    )(p, p, p, w)


def _rope_tables(cfg, bt):
    t = jnp.arange(cfg.seq)
    pos = jnp.stack([t // cfg.grid_w, t % cfg.grid_w], axis=-1).astype(F32)
    inv = ROPE_THETA ** (-jnp.arange(ROPE_PAIRS, dtype=F32) * 2.0 / ROPE_AXIS_DIM)
    ang = pos[:, :, None] * inv
    cos, sin = jnp.cos(ang), jnp.sin(ang)
    cmap = jnp.concatenate([cos[:, 0], cos[:, 0], cos[:, 1], cos[:, 1]], axis=-1)
    smap = jnp.concatenate([-sin[:, 0], sin[:, 0], -sin[:, 1], sin[:, 1]], axis=-1)
    ctab = jnp.concatenate([jnp.tile(cmap, (1, 2)), jnp.ones((bt, ATT_VD), F32)], axis=0)
    stab = jnp.concatenate([jnp.tile(smap, (1, 2)), jnp.zeros((bt, ATT_VD), F32)], axis=0)
    return ctab, stab


def _att_prep_kernel(q_ref, k_ref, v_ref, c_ref, s_ref, qt_ref, kr_ref, vt_ref, *, heads):
    cos = c_ref[...]
    sin = s_ref[...]
    lane = lax.broadcasted_iota(jnp.int32, cos.shape, 1)
    low = (lane % ROPE_AXIS_DIM) < ROPE_PAIRS

    def rope(u):
        swapped = jnp.where(low, pltpu.roll(u, ATT_VD - ROPE_PAIRS, 1), pltpu.roll(u, ROPE_PAIRS, 1))
        return u * cos + swapped * sin

    for h in range(heads):
        sl = slice(h * ATT_VD, (h + 1) * ATT_VD)
        qt_ref[sl, :] = (rope(q_ref[:, sl].astype(F32)) * ATT_Q_SCALE).T.astype(qt_ref.dtype)
        kr_ref[:, sl] = rope(k_ref[:, sl].astype(F32)).astype(kr_ref.dtype)
        vt_ref[sl, :] = v_ref[:, sl].astype(F32).T.astype(vt_ref.dtype)


def _att_prep(cfg, p, ctab, stab, bt=256):
    m, wd = cfg.m_all, cfg.branch_w
    n_lat_blk = cfg.m_lat // bt
    n_seq_blk = cfg.seq // bt
    tab_map = lambda i: (jnp.where(i < n_lat_blk, i % n_seq_blk, n_seq_blk), 0)
    return pl.pallas_call(
        functools.partial(_att_prep_kernel, heads=cfg.heads),
        out_shape=(jax.ShapeDtypeStruct((wd, m), BF16), jax.ShapeDtypeStruct((m, wd), BF16),
                   jax.ShapeDtypeStruct((wd, m), BF16)),
        grid=(m // bt,),
        in_specs=[
            pl.BlockSpec((bt, wd), lambda i: (i, 3)),
            pl.BlockSpec((bt, wd), lambda i: (i, 4)),
            pl.BlockSpec((bt, wd), lambda i: (i, 5)),
            pl.BlockSpec((bt, ATT_VD), tab_map),
            pl.BlockSpec((bt, ATT_VD), tab_map),
        ],
        out_specs=(pl.BlockSpec((wd, bt), lambda i: (0, i)), pl.BlockSpec((bt, wd), lambda i: (i, 0)),
                   pl.BlockSpec((wd, bt), lambda i: (0, i))),
        compiler_params=_params(1),
        name="att_prep",
    )(p, p, p, ctab, stab)


def _attn_kernel(lam_ref, g_ref, qt_ref, *refs, seg_lens, tk, lam_init):
    n_seg = len(seg_lens)
    kv = refs[:2 * n_seg]
    o_ref = refs[2 * n_seg]
    s1_ref, s2_ref, acc1_ref, acc2_ref = refs[2 * n_seg + 1:]
    bq = qt_ref.shape[1]

    qt = qt_ref[...].astype(F32)
    sub = lax.broadcasted_iota(jnp.int32, qt.shape, 0)
    q1 = jnp.where(sub < ATT_HD, qt, 0.0).astype(qt_ref.dtype)
    q2 = jnp.where(sub >= ATT_HD, qt, 0.0).astype(qt_ref.dtype)

    def chunks(seg):
        tkk = min(tk, seg_lens[seg])
        return tkk, seg_lens[seg] // tkk

    def pass_a(q, s_ref):
        m = jnp.full((1, bq), -jnp.inf, F32)
        off = 0
        for seg in range(n_seg):
            k_ref = kv[2 * seg]
            tkk, n_chunk = chunks(seg)

            def body(c, m, k_ref=k_ref, tkk=tkk, off=off):
                r0 = pl.multiple_of(c * tkk, tkk)
                s = _dot(k_ref[pl.ds(r0, tkk), :], q)
                s_ref[pl.ds(off + r0, tkk), :] = s
                return jnp.maximum(m, jnp.max(s, axis=0, keepdims=True))

            m = lax.fori_loop(0, n_chunk, body, m, unroll=min(ATT_UNROLL, n_chunk))
            off += seg_lens[seg]
        return m

    def pass_b(s_ref, m, acc_ref):
        acc_ref[...] = jnp.zeros_like(acc_ref)
        l = jnp.zeros((1, bq), F32)
        off = 0
        for seg in range(n_seg):
            vt_ref = kv[2 * seg + 1]
            tkk, n_chunk = chunks(seg)

            def body(c, l, vt_ref=vt_ref, tkk=tkk, off=off):
                r0 = pl.multiple_of(c * tkk, tkk)
                p = jnp.exp2(s_ref[pl.ds(off + r0, tkk), :] - m)
                vt = vt_ref[:, pl.ds(r0, tkk)]
                acc_ref[...] += _dot(vt, p.astype(vt.dtype))
                return l + jnp.sum(p, axis=0, keepdims=True)

            l = lax.fori_loop(0, n_chunk, body, l, unroll=min(ATT_UNROLL, n_chunk))
            off += seg_lens[seg]
        return l

    m1 = pass_a(q1, s1_ref)
    acc1_ref[...] = jnp.zeros_like(acc1_ref)
    l1 = jnp.zeros((1, bq), F32)
    m2 = jnp.full((1, bq), -jnp.inf, F32)
    off = 0
    for seg in range(n_seg):
        k_ref, vt_ref = kv[2 * seg], kv[2 * seg + 1]
        tkk, n_chunk = chunks(seg)
        for c in range(n_chunk):
            r0 = c * tkk
            s = _dot(k_ref[pl.ds(r0, tkk), :], q2)
            s2_ref[pl.ds(off + r0, tkk), :] = s
            m2 = jnp.maximum(m2, jnp.max(s, axis=0, keepdims=True))
            p = jnp.exp2(s1_ref[pl.ds(off + r0, tkk), :] - m1)
            vt = vt_ref[:, pl.ds(r0, tkk)]
            acc1_ref[...] += _dot(vt, p.astype(vt.dtype))
            l1 = l1 + jnp.sum(p, axis=0, keepdims=True)
        off += seg_lens[seg]
    l2 = pass_b(s2_ref, m2, acc2_ref)

    lq = lam_ref[...]
    lam = (jnp.exp(jnp.sum(lq[0:1] * lq[1:2], axis=-1, keepdims=True))
           - jnp.exp(jnp.sum(lq[2:3] * lq[3:4], axis=-1, keepdims=True)) + lam_init)
    ot = acc1_ref[...] * (1.0 / l1) - acc2_ref[...] * (lam / l2)
    ms = jnp.mean(ot * ot, axis=0, keepdims=True)
    y = ot * lax.rsqrt(ms + EPS) * g_ref[...] * (1.0 - lam_init)
    o_ref[...] = y.T.astype(o_ref.dtype)


def _attention(cfg, lam_qk, subln_g, qt, segs, q_col_blk0, n_q_per_batch, out_rows, lam_init, bq=256, tk=1024):
    wd = cfg.branch_w
    in_specs = [
        pl.BlockSpec((4, ATT_HD), lambda b, h, i: (0, 0)),
        pl.BlockSpec((ATT_VD, 1), lambda b, h, i: (0, 0)),
        pl.BlockSpec((ATT_VD, bq), lambda b, h, i: (h, q_col_blk0 + b * n_q_per_batch + i)),
    ]
    args = [lam_qk, subln_g.reshape(ATT_VD, 1), qt]
    seg_lens = []
    for k_arr, k_blk, k_col0, vt_arr, length in segs:
        in_specs.append(pl.BlockSpec((length, ATT_VD), lambda b, h, i, f=k_blk, c0=k_col0: (f(b), c0 + h)))
        in_specs.append(pl.BlockSpec((ATT_VD, length), lambda b, h, i, f=k_blk: (h, f(b))))
        args += [k_arr, vt_arr]
        seg_lens.append(length)
    l_tot = sum(seg_lens)
    return pl.pallas_call(
        functools.partial(_attn_kernel, seg_lens=tuple(seg_lens), tk=tk, lam_init=lam_init),
        out_shape=jax.ShapeDtypeStruct((out_rows, wd), BF16),
        grid=(cfg.batch, cfg.heads, n_q_per_batch),
        in_specs=in_specs,
        out_specs=pl.BlockSpec((bq, ATT_VD), lambda b, h, i: (b * n_q_per_batch + i, h)),
        scratch_shapes=[pltpu.VMEM((l_tot, bq), F32), pltpu.VMEM((l_tot, bq), F32),
                        pltpu.VMEM((ATT_VD, bq), F32), pltpu.VMEM((ATT_VD, bq), F32)],
        compiler_params=_params(3),
        name="diff_attention",
    )(*args)


SCAN_PAD = SUBLANES


def _scan_levels(n):
    plan = []
    while n > 1:
        fan = SUBLANES if n % SUBLANES == 0 else n
        assert fan <= SUBLANES, "sequence length must factor into groups of at most 8"
        plan.append((n, fan, n // fan))
        n //= fan
    return plan


def _scan_scratch_rows(n):
    rows = 0
    for _, _, parts in _scan_levels(n):
        rows += 2 * SCAN_PAD + -(-parts // SUBLANES) * SUBLANES
    return max(rows, SUBLANES)


def _scan_inplace(a_ref, b_ref, base, n, rev, sa_ref, sb_ref, off=0):
    if n == 1:
        return
    fan = SUBLANES if n % SUBLANES == 0 else n
    parts = n // fan
    order = list(range(fan))
    if rev:
        order.reverse()

    def rows(r):
        return pl.ds(base + r, parts, stride=fan) if parts > 1 else pl.ds(base + r, 1)

    acc_a = acc_b = None
    for r in order:
        a_r = a_ref[rows(r), :]
        b_r = b_ref[rows(r), :]
        if acc_a is None:
            acc_a, acc_b = a_r, b_r
        else:
            acc_b = a_r * acc_b + b_r
            acc_a = a_r * acc_a
            a_ref[rows(r), :] = acc_a
            b_ref[rows(r), :] = acc_b
    if parts == 1:
        return
    data = off + SCAN_PAD
    ones = jnp.ones((SCAN_PAD, LANES), F32)
    zeros = jnp.zeros((SCAN_PAD, LANES), F32)
    sa_ref[pl.ds(off, SCAN_PAD), :] = ones
    sb_ref[pl.ds(off, SCAN_PAD), :] = zeros
    sa_ref[pl.ds(data + parts, SCAN_PAD), :] = ones
    sb_ref[pl.ds(data + parts, SCAN_PAD), :] = zeros
    sa_ref[pl.ds(data, parts), :] = acc_a
    sb_ref[pl.ds(data, parts), :] = acc_b
    nxt = off + 2 * SCAN_PAD + -(-parts // SUBLANES) * SUBLANES
    _scan_inplace(sa_ref, sb_ref, data, parts, rev, sa_ref, sb_ref, nxt)
    shift = data + 1 if rev else data - 1
    carry_a = sa_ref[pl.ds(shift, parts), :]
    carry_b = sb_ref[pl.ds(shift, parts), :]
    for r in order:
        a_r = a_ref[rows(r), :]
        b_ref[rows(r), :] = b_ref[rows(r), :] + a_r * carry_b
        a_ref[rows(r), :] = a_r * carry_a


def _softplus(x):
    return jnp.maximum(x, 0.0) + jnp.log1p(jnp.exp(-jnp.abs(x)))


def _gelu_tanh(x):
    return 0.5 * x * (1.0 + jnp.tanh(math.sqrt(2.0 / math.pi) * (x + 0.044715 * (x * x * x))))


def _lru_kernel(lxl_ref, lyl_ref, lxc_ref, lyc_ref, cw_ref, cb_ref, wa_ref, ba_ref, wx_ref, bx_ref, lam_ref,
                ol_ref, oc_ref, pad_ref, a0_ref, b0_ref, a1_ref, b1_ref, sa_ref, sb_ref, *, tt):
    l_ctx = lxc_ref.shape[0]
    l_lat = lxl_ref.shape[0]
    cw = cw_ref[...]
    cb = cb_ref[...]
    ab_refs = ((a0_ref, b0_ref), (a1_ref, b1_ref))
    neg_sp = [-LRU_C * _softplus(-lam_ref[d:d + 1, :]) for d in range(2)]
    wa = [wa_ref[d].astype(BF16) for d in range(2)]
    wx = [wx_ref[d].astype(BF16) for d in range(2)]

    for lx_ref, base, length, is_ctx in ((lxc_ref, 0, l_ctx, True), (lxl_ref, l_ctx, l_lat, False)):
        ttk = min(tt, length)
        pad_ref[pl.ds(0, SUBLANES), :] = jnp.zeros((SUBLANES, LANES), F32)
        pad_ref[pl.ds(SUBLANES + length, SUBLANES), :] = jnp.zeros((SUBLANES, LANES), F32)

        def fill(ci, carry, lx_ref=lx_ref, ttk=ttk):
            r0 = pl.multiple_of(ci * ttk, ttk)
            pad_ref[pl.ds(SUBLANES + r0, ttk), :] = lx_ref[pl.ds(r0, ttk), :].astype(F32)
            return carry

        lax.fori_loop(0, length // ttk, fill, 0)

        def gates(ci, carry, base=base, length=length, is_ctx=is_ctx, ttk=ttk):
            r0 = pl.multiple_of(ci * ttk, ttk)
            xx = pad_ref[pl.ds(r0, ttk + 2 * SUBLANES), :]
            u = (cw[0:1, :] * xx[6:6 + ttk] + cw[1:2, :] * xx[7:7 + ttk]
                 + cw[2:3, :] * xx[8:8 + ttk] + cw[3:4, :] * xx[9:9 + ttk]) + cb
            ub = u.astype(BF16)
            row = r0 + lax.broadcasted_iota(jnp.int32, u.shape, 0)
            for d in range(2):
                r_gate = _sigmoid(_dot(ub, wa[d]) + ba_ref[d:d + 1, :])
                i_gate = _sigmoid(_dot(ub, wx[d]) + bx_ref[d:d + 1, :])
                log_a = neg_sp[d] * r_gate
                a = jnp.exp(log_a)
                mult = jnp.sqrt(-jnp.tanh(log_a) * (a * a + 1.0))
                if is_ctx:
                    mult = jnp.where(row == (length - 1 if d == 1 else 0), 1.0, mult)
                a_ref, b_ref = ab_refs[d]
                a_ref[pl.ds(base + r0, ttk), :] = a
                b_ref[pl.ds(base + r0, ttk), :] = mult * (i_gate * u)
            return carry

        lax.fori_loop(0, length // ttk, gates, 0, unroll=min(2, length // ttk))

    for d, rev in ((0, False), (1, True)):
        a_ref, b_ref = ab_refs[d]
        _scan_inplace(a_ref, b_ref, 0, l_ctx, rev, sa_ref, sb_ref)
        _scan_inplace(a_ref, b_ref, l_ctx, l_lat, rev, sa_ref, sb_ref)

    h0 = (b0_ref[pl.ds(l_ctx - 1, 1), :], b1_ref[pl.ds(0, 1), :])

    for ly_ref, o_ref, base, length, is_ctx in ((lyc_ref, oc_ref, 0, l_ctx, True), (lyl_ref, ol_ref, l_ctx, l_lat, False)):
        ttk = min(tt, length)

        def emit(ci, carry, ly_ref=ly_ref, o_ref=o_ref, base=base, is_ctx=is_ctx, ttk=ttk):
            r0 = pl.multiple_of(ci * ttk, ttk)
            h = b0_ref[pl.ds(base + r0, ttk), :] + b1_ref[pl.ds(base + r0, ttk), :]
            if not is_ctx:
                h = h + a0_ref[pl.ds(base + r0, ttk), :] * h0[0] + a1_ref[pl.ds(base + r0, ttk), :] * h0[1]
            o_ref[pl.ds(r0, ttk), :] = (h * _gelu_tanh(ly_ref[pl.ds(r0, ttk), :].astype(F32))).astype(o_ref.dtype)
            return carry

        lax.fori_loop(0, length // ttk, emit, 0)


def _lru(cfg, p, conv_w, conv_b, w_a, b_a, w_x, b_x, lam_p, tt=256):
    wd, s, lc = cfg.branch_w, cfg.seq, cfg.ctx_len
    nb = wd // LANES
    l_tot = s + lc
    ctx_blk0 = cfg.m_lat // lc
    lx_col, ly_col = 6 * nb, 7 * nb
    vec2 = pl.BlockSpec((2, LANES), lambda b, n: (0, n))
    wspec = pl.BlockSpec((2, None, LANES, LANES), lambda b, n: (0, n, 0, 0))
    scan_rows = max(_scan_scratch_rows(s), _scan_scratch_rows(lc))
    return pl.pallas_call(
        functools.partial(_lru_kernel, tt=tt),
        out_shape=(jax.ShapeDtypeStruct((cfg.m_lat, wd), BF16), jax.ShapeDtypeStruct((cfg.batch * lc, wd), BF16)),
        grid=(cfg.batch, nb),
        in_specs=[
            pl.BlockSpec((s, LANES), lambda b, n: (b, lx_col + n)),
            pl.BlockSpec((s, LANES), lambda b, n: (b, ly_col + n)),
            pl.BlockSpec((lc, LANES), lambda b, n: (ctx_blk0 + b, lx_col + n)),
            pl.BlockSpec((lc, LANES), lambda b, n: (ctx_blk0 + b, ly_col + n)),
            pl.BlockSpec((4, LANES), lambda b, n: (0, n)),
            pl.BlockSpec((1, LANES), lambda b, n: (0, n)),
            wspec, vec2, wspec, vec2, vec2,
        ],
        out_specs=(pl.BlockSpec((s, LANES), lambda b, n: (b, n)), pl.BlockSpec((lc, LANES), lambda b, n: (b, n))),
        scratch_shapes=[pltpu.VMEM((s + 2 * SUBLANES, LANES), F32)]
        + [pltpu.VMEM((l_tot, LANES), F32)] * 4
        + [pltpu.VMEM((scan_rows, LANES), F32)] * 2,
        compiler_params=_params(2),
        name="rglru",
    )(p, p, p, p, conv_w, conv_b.reshape(1, wd), w_a, b_a, w_x, b_x, lam_p)


def _merge_kernel(*refs, wd, n_lat_blk):
    branch_refs, (ga_ref, gb_ref, gc_ref, w_ref, o_ref, wbf_ref) = refs[:6], refs[6:]
    _cast_weight_block(w_ref, wbf_ref)
    is_lat = pl.program_id(1) < n_lat_blk
    acc = None
    for i, g_ref in enumerate((ga_ref, gb_ref, gc_ref)):
        x = jnp.where(is_lat, branch_refs[2 * i][...], branch_refs[2 * i + 1][...])
        g = _sigmoid(g_ref[...].astype(F32))
        term = g * _dot(x, wbf_ref[i * wd:(i + 1) * wd, :])
        acc = term if acc is None else acc + term
    o_ref[...] = acc.astype(o_ref.dtype)


def _merge(cfg, branches, p, wb, layer, bm, bn):
    m, wd, d = cfg.m_all, cfg.branch_w, cfg.d_model
    g_blk0 = 8 * wd // bn
    nd = d // bn
    n_lat_blk = cfg.m_lat // bm
    lat_spec = pl.BlockSpec((bm, wd), lambda j, i: (jnp.minimum(i, n_lat_blk - 1), 0))
    ctx_spec = pl.BlockSpec((bm, wd), lambda j, i: (jnp.maximum(i - n_lat_blk, 0), 0))
    gspec = lambda br: pl.BlockSpec((bm, bn), lambda j, i: (i, g_blk0 + br * nd + j))
    args = [a for pair in branches for a in pair]
    return pl.pallas_call(
        functools.partial(_merge_kernel, wd=wd, n_lat_blk=n_lat_blk),
        out_shape=jax.ShapeDtypeStruct((m, d), BF16),
        grid=(nd, m // bm),
        in_specs=[lat_spec, ctx_spec] * 3 + [gspec(0), gspec(1), gspec(2),
                  pl.BlockSpec((None, 3 * wd, bn), lambda j, i: (layer, 0, j))],
        out_specs=pl.BlockSpec((bm, bn), lambda j, i: (i, j)),
        scratch_shapes=[pltpu.VMEM((3 * wd, bn), BF16)],
        compiler_params=_params(2),
        name="merge_branches",
    )(*args, p, p, p, wb)


MOE_TILE = 256
MOE_BLOCK = 256


def _moe_slots(cfg):
    n_tiles = cfg.m_all * cfg.top_k // MOE_TILE + cfg.n_experts
    return n_tiles, n_tiles * MOE_TILE


def _assignment_index(t, top_k):
    per_row = LANES // top_k
    assert per_row * top_k == LANES and per_row & (per_row - 1) == 0
    shift = per_row.bit_length() - 1
    return lax.shift_right_logical(t, shift), (t & (per_row - 1)) * top_k


def _dispatch_kernel(fill_ref, nchunk_ref, nv_ref, pos_ref, xq_ref, xs_hbm, zero_ref, pos_smem, sem, idx_sem,
                     *, n_experts, top_k, n_tiles):
    i = pl.program_id(0)
    n_assign = pos_ref.shape[0] * pos_ref.shape[1]
    bt = n_assign // top_k

    @pl.when(i == 0)
    def _():
        zero_ref[...] = jnp.zeros_like(zero_ref)
        piece = zero_ref.at[pl.ds(0, SUBLANES)]
        total = 0
        for e in range(n_experts):
            def fill_piece(c, carry, e=e):
                start = pl.multiple_of(fill_ref[e] + c * SUBLANES, SUBLANES)
                pltpu.make_async_copy(piece, xs_hbm.at[pl.ds(start, SUBLANES)], sem).start()
                return carry

            lax.fori_loop(0, nchunk_ref[e], fill_piece, 0)
            total = total + nchunk_ref[e]

        def idle_tile(j, carry):
            start = pl.multiple_of(j * MOE_TILE, MOE_TILE)
            pltpu.make_async_copy(zero_ref, xs_hbm.at[pl.ds(start, MOE_TILE)], sem).start()
            return carry

        lax.fori_loop(nv_ref[0], n_tiles, idle_tile, 0)

        def drain_piece(c, carry):
            pltpu.make_async_copy(piece, xs_hbm.at[pl.ds(0, SUBLANES)], sem).wait()
            return carry

        lax.fori_loop(0, total, drain_piece, 0)

        def drain_tile(j, carry):
            pltpu.make_async_copy(zero_ref, xs_hbm.at[pl.ds(0, MOE_TILE)], sem).wait()
            return carry

        lax.fori_loop(nv_ref[0], n_tiles, drain_tile, 0)

    idx_copy = pltpu.make_async_copy(pos_ref, pos_smem, idx_sem)
    idx_copy.start()
    idx_copy.wait()

    def issue(t, carry):
        row, col = _assignment_index(t, top_k)
        for k in range(top_k):
            dst = pos_smem[row, col + k]
            pltpu.make_async_copy(xq_ref.at[pl.ds(t, 1)], xs_hbm.at[pl.ds(dst, 1)], sem).start()
        return carry

    lax.fori_loop(0, bt, issue, 0, unroll=2)
    pltpu.make_async_copy(xs_hbm.at[pl.ds(0, n_assign)], xs_hbm.at[pl.ds(0, n_assign)], sem).wait()


def _dispatch(cfg, fill, nchunk, n_valid, pos2d, xq):
    m, half = xq.shape
    n_tiles, slot_rows = _moe_slots(cfg)
    bt = 2 * MOE_BLOCK if m % (2 * MOE_BLOCK) == 0 else MOE_BLOCK
    rows_per_step = bt * cfg.top_k // LANES
    return pl.pallas_call(
        functools.partial(_dispatch_kernel, n_experts=cfg.n_experts, top_k=cfg.top_k, n_tiles=n_tiles),
        out_shape=jax.ShapeDtypeStruct((slot_rows, half), jnp.uint32),
        grid_spec=pltpu.PrefetchScalarGridSpec(
            num_scalar_prefetch=3,
            grid=(m // bt,),
            in_specs=[pl.BlockSpec((rows_per_step, LANES), lambda i, *_: (i, 0)),
                      pl.BlockSpec((bt, half), lambda i, *_: (i, 0))],
            out_specs=pl.BlockSpec(memory_space=pl.ANY),
            scratch_shapes=[pltpu.VMEM((MOE_TILE, half), jnp.uint32),
                            pltpu.SMEM((rows_per_step, LANES), jnp.int32),
                            pltpu.SemaphoreType.DMA(()), pltpu.SemaphoreType.DMA(())]),
        compiler_params=_params(1),
        name="moe_dispatch",
    )(fill, nchunk, n_valid, pos2d, xq)


def _expert_kernel(te_ref, nv_ref, xs_ref, perm_ref, wgu_ref, bgu_ref, wdn_ref, bdn_ref, ys_ref,
                   wgu_bf, wdn_bf, *, ff):
    j = pl.program_id(0)
    d = wgu_ref.shape[0]
    half = d // 2
    e = te_ref[j]
    e_prev = te_ref[jnp.maximum(j - 1, 0)]

    @pl.when((j == 0) | (e != e_prev))
    def _():
        rows = min(512, d)

        def cast_gu(c, carry):
            r0 = pl.multiple_of(c * rows, rows)
            w = wgu_ref[pl.ds(r0, rows), :].astype(BF16)
            wgu_bf[pl.ds(r0, rows), :] = _dot(w, perm_ref[...]).astype(BF16)
            return carry

        lax.fori_loop(0, d // rows, cast_gu, 0, unroll=min(4, d // rows))
        rows_dn = 32

        def cast_dn(c, carry):
            r0 = pl.multiple_of(c * rows_dn, rows_dn)
            wdn_bf[pl.ds(r0, rows_dn), :] = wdn_ref[pl.ds(r0, rows_dn), :].astype(BF16)
            return carry

        lax.fori_loop(0, ff // rows_dn, cast_dn, 0)

    @pl.when(j < nv_ref[0])
    def _():
        lo, hi = _unpack_pair(xs_ref[...])
        h = (_dot(lo.astype(BF16), wgu_bf[pl.ds(0, half), :]) + _dot(hi.astype(BF16), wgu_bf[pl.ds(half, half), :])
             + bgu_ref[...])
        h_glu = jnp.minimum(h[:, :ff], SWIGLU_LIMIT)
        h_lin = jnp.clip(h[:, ff:], -SWIGLU_LIMIT, SWIGLU_LIMIT)
        act = (h_glu * _sigmoid(SWIGLU_ALPHA * h_glu) * (h_lin + 1.0)).astype(BF16)
        cw = min(512, half)
        for c in range(half // cw):
            lo_sl = slice(c * cw, (c + 1) * cw)
            hi_sl = slice(half + c * cw, half + (c + 1) * cw)
            y_lo = _dot(act, wdn_bf[:, lo_sl]) + bdn_ref[:, lo_sl]
            y_hi = _dot(act, wdn_bf[:, hi_sl]) + bdn_ref[:, hi_sl]
            ys_ref[:, lo_sl] = _pack_pair(y_lo, y_hi)

    @pl.when(j >= nv_ref[0])
    def _():
        ys_ref[...] = jnp.zeros_like(ys_ref)


def _experts(cfg, tile_e, n_valid, xs, perm, w_gu, b_gu_perm, w_dn, b_dn, layer):
    d, ff, ne = cfg.d_model, cfg.expert_ff, cfg.n_experts
    half = d // 2
    n_tiles, slot_rows = _moe_slots(cfg)
    tile_blk = lambda j, te, nv: (jnp.minimum(j, nv[0] - 1), 0)
    by_expert = lambda j, te, nv: (te[j], 0, 0)
    by_layer_expert = lambda j, te, nv: (layer, te[j], 0, 0)
    return pl.pallas_call(
        functools.partial(_expert_kernel, ff=ff),
        out_shape=jax.ShapeDtypeStruct((slot_rows, half), jnp.uint32),
        grid_spec=pltpu.PrefetchScalarGridSpec(
            num_scalar_prefetch=2,
            grid=(n_tiles,),
            in_specs=[
                pl.BlockSpec((MOE_TILE, half), tile_blk),
                pl.BlockSpec((2 * ff, 2 * ff), lambda j, te, nv: (0, 0)),
                pl.BlockSpec((None, None, d, 2 * ff), by_layer_expert),
                pl.BlockSpec((None, 1, 2 * ff), by_expert),
                pl.BlockSpec((None, None, ff, d), by_layer_expert),
                pl.BlockSpec((None, 1, d), by_expert),
            ],
            out_specs=pl.BlockSpec((MOE_TILE, half), lambda j, te, nv: (j, 0)),
            scratch_shapes=[pltpu.VMEM((d, 2 * ff), BF16), pltpu.VMEM((ff, d), BF16)]),
        compiler_params=_params(1),
        name="moe_experts",
    )(tile_e, n_valid, xs, perm, w_gu, b_gu_perm.reshape(ne, 1, 2 * ff), w_dn, b_dn.reshape(ne, 1, d))


def _combine_kernel(pos_ref, pos_next_ref, route_ref, res_ref, mod_ref, ys_hbm, o_ref, buf_ref, pos_smem,
                    sems, idx_sem, *, gate, top_k):
    i = pl.program_id(0)
    n = pl.num_programs(0)
    bt, d = res_ref.shape
    half = d // 2
    slot = lax.rem(i, 2)
    other = 1 - slot

    def load_positions(src_ref):
        idx_copy = pltpu.make_async_copy(src_ref, pos_smem, idx_sem)
        idx_copy.start()
        idx_copy.wait()

    def gather_token(t, to_slot):
        row, col = _assignment_index(t, top_k)
        for k in range(top_k):
            src = pos_smem[row, col + k]
            pltpu.make_async_copy(ys_hbm.at[pl.ds(src, 1)], buf_ref.at[to_slot, k, pl.ds(t, 1)],
                                  sems.at[to_slot]).start()

    def wait_slot(s):
        pltpu.make_async_copy(buf_ref.at[s], buf_ref.at[s], sems.at[s]).wait()

    @pl.when(i == 0)
    def _():
        load_positions(pos_ref)

        def first(t, carry):
            gather_token(t, 0)
            return carry

        lax.fori_loop(0, bt, first, 0, unroll=2)

    wait_slot(slot)
    load_positions(pos_next_ref)

    rb = 2 * SUBLANES
    cw = min(512, half)

    def rows(r, carry):
        r0 = pl.multiple_of(r * rb, rb)
        rs = pl.ds(r0, rb)
        for u in range(rb):
            gather_token(r0 + u, other)
        weights = [route_ref[rs, top_k + k:top_k + k + 1] for k in range(top_k)]
        for c in range(half // cw):
            lo_sl = slice(c * cw, (c + 1) * cw)
            hi_sl = slice(half + c * cw, half + (c + 1) * cw)
            acc_lo = acc_hi = None
            for k in range(top_k):
                lo, hi = _unpack_pair(buf_ref[slot, k, rs, lo_sl])
                acc_lo = weights[k] * lo if acc_lo is None else acc_lo + weights[k] * lo
                acc_hi = weights[k] * hi if acc_hi is None else acc_hi + weights[k] * hi
            o_ref[rs, lo_sl] = res_ref[rs, lo_sl] + mod_ref[0, gate:gate + 1, lo_sl] * acc_lo
            o_ref[rs, hi_sl] = res_ref[rs, hi_sl] + mod_ref[0, gate:gate + 1, hi_sl] * acc_hi
        return carry

    lax.fori_loop(0, bt // rb, rows, 0)

    @pl.when(i == n - 1)
    def _():
        wait_slot(other)


def _combine(cfg, pos2d, route, res, mod, ys, gate):
    m, d = res.shape
    half = d // 2
    ep = route.shape[1]
    bt = MOE_BLOCK
    rows_per_step = bt * cfg.top_k // LANES
    return pl.pallas_call(
        functools.partial(_combine_kernel, gate=gate, top_k=cfg.top_k),
        out_shape=jax.ShapeDtypeStruct((m, d), F32),
        grid=(m // bt,),
        in_specs=[
            pl.BlockSpec((rows_per_step, LANES), lambda i: (i, 0)),
            pl.BlockSpec((rows_per_step, LANES), lambda i: (jnp.minimum(i + 1, m // bt - 1), 0)),
            pl.BlockSpec((bt, ep), lambda i: (i, 0)),
            pl.BlockSpec((bt, d), lambda i: (i, 0)),
            pl.BlockSpec((1, N_MOD, d), lambda i: (_seg_of_block(cfg, i, bt), 0, 0)),
            pl.BlockSpec(memory_space=pl.ANY),
        ],
        out_specs=pl.BlockSpec((bt, d), lambda i: (i, 0)),
        scratch_shapes=[pltpu.VMEM((2, cfg.top_k, bt, half), jnp.uint32),
                        pltpu.SMEM((rows_per_step, LANES), jnp.int32),
                        pltpu.SemaphoreType.DMA((2,)), pltpu.SemaphoreType.DMA(())],
        compiler_params=_params(1),
        name="moe_combine",
    )(pos2d, pos2d, route, res, mod, ys)


def _moe_plan(cfg, route, counts8):
    k, ne = cfg.top_k, cfg.n_experts
    n_tiles, _ = _moe_slots(cfg)
    expert = route[:, :k].astype(jnp.int32)
    rank = route[:, 2 * k:3 * k].astype(jnp.int32)
    counts = counts8[0, :ne].astype(jnp.int32)
    padded = (counts + MOE_TILE - 1) // MOE_TILE * MOE_TILE
    region_end = jnp.cumsum(padded)
    region_start = region_end - padded
    pos = region_start[expert] + rank
    fill = (region_start + counts) // SUBLANES * SUBLANES
    nchunk = (region_end - fill) // SUBLANES
    n_valid = region_end[-1] // MOE_TILE
    tile_start = jnp.arange(n_tiles, dtype=jnp.int32) * MOE_TILE
    tile_e = jnp.sum(region_end[None, :] <= tile_start[:, None], axis=1)
    tile_e = jnp.minimum(tile_e, ne - 1).astype(jnp.int32)
    tile_e = tile_e[jnp.minimum(jnp.arange(n_tiles), n_valid - 1)]
    return (pos.reshape(-1, LANES), fill.astype(jnp.int32), nchunk.astype(jnp.int32), tile_e,
            n_valid.reshape(1).astype(jnp.int32))


def _deinterleave_perm(ff):
    col = jnp.arange(2 * ff)
    src = jnp.where(col < ff, 2 * col, 2 * (col - ff) + 1)
    return (jnp.arange(2 * ff)[:, None] == src[None, :]).astype(BF16)


def _forward(cfg, x, c, ctx, c_ctx, ada_a, ada_b, ada_bias, norm1_g, norm2_g, w_in, conv_a_w, lam_qk,
             subln_g, lru_conv_w, lru_conv_b, lru_w_a, lru_b_a, lru_w_x, lru_b_x, lru_lambda,
             w_branch, w_out, router_w, router_b, w_gu, b_gu, w_dn, b_dn, final_g):
    d, wd, bsz, s, lc = cfg.d_model, cfg.branch_w, cfg.batch, cfg.seq, cfg.ctx_len
    m_lat, m_all, ne = cfg.m_lat, cfg.m_all, cfg.n_experts
    assert bsz * lc <= s and s % lc == 0 and bsz + 1 <= SUBLANES
    bm = min(512, bsz * lc)
    bn = min(1024, d)
    bt = min(256, lc)
    ep = -(-ne // LANES) * LANES

    xs = jnp.concatenate([x.reshape(m_lat, d), ctx.reshape(bsz * lc, d)], axis=0)
    cond8 = jnp.zeros((SUBLANES, d), F32).at[:bsz].set(c).at[bsz].set(c_ctx)
    ctab, stab = _rope_tables(cfg, bt)
    ctx_blk = lambda b: m_lat // lc + b
    perm = _deinterleave_perm(cfg.expert_ff)

    for l in range(cfg.depth):
        lam_init = LAMBDA_INIT_BASE - LAMBDA_INIT_AMP * math.exp(-LAMBDA_INIT_RATE * l)
        mod = _ada(cfg, cond8, ada_a[l], ada_b[l], ada_bias[l])

        xn = _norm_mod(cfg, xs, norm1_g[l], mod, 0, 1)
        p = _matmul(xn, w_in, l, bm, bn)
        oa = (_conv_mixer(cfg, p, conv_a_w[l], s, 0, bsz, m_lat),
              _conv_mixer(cfg, p, conv_a_w[l], lc, m_lat // lc, bsz, bsz * lc))
        qt, kr, vt = _att_prep(cfg, p, ctab, stab, bt)
        nh = wd // ATT_VD
        ctx_seg = (p, ctx_blk, 4 * nh, vt, lc)
        lat_seg = (kr, lambda b: b, 0, vt, s)
        bq_lat = min(ATT_LATENT_BQ, s)
        ob_lat = _attention(cfg, lam_qk[l], subln_g[l], qt, [ctx_seg, lat_seg], 0, s // bq_lat, m_lat, lam_init,
                            bq=bq_lat)
        ob_ctx = _attention(cfg, lam_qk[l], subln_g[l], qt, [ctx_seg], m_lat // bt, lc // bt, bsz * lc, lam_init, bq=bt)
        oc = _lru(cfg, p, lru_conv_w[l], lru_conv_b[l], lru_w_a[l], lru_b_a[l],
                  lru_w_x[l], lru_b_x[l], lru_lambda[l])
        y = _merge(cfg, (oa, (ob_lat, ob_ctx), oc), p, w_branch, l, bm, bn)
        xs = _matmul_residual(cfg, y, w_out, l, xs, mod, 2, bm, bn)

        wr_pad = jnp.zeros((d, ep), F32).at[:, :ne].set(router_w[l])
        br_pad = jnp.full((1, ep), -jnp.inf, F32).at[0, :ne].set(router_b[l])
        xq, route, counts8 = _norm_router(cfg, xs, norm2_g[l], mod, 3, 4, wr_pad, br_pad)
        pos2d, fill, nchunk, tile_e, n_valid = _moe_plan(cfg, route, counts8)
        slots = _dispatch(cfg, fill, nchunk, n_valid, pos2d, xq)
        b_gu_perm = jnp.concatenate([b_gu[l][:, 0::2], b_gu[l][:, 1::2]], axis=-1)
        ys = _experts(cfg, tile_e, n_valid, slots, perm, w_gu, b_gu_perm, w_dn, b_dn[l], l)
        xs = _combine(cfg, pos2d, route, xs, mod, ys, 5)

    return _norm_plain(xs, final_g, m_lat).reshape(bsz, s, d)


def kernel(x, c, ctx, c_ctx, ada_a, ada_b, ada_bias, norm1_g, norm2_g, w_in, conv_a_w, lam_qk, subln_g, lru_conv_w, lru_conv_b, lru_w_a, lru_b_a, lru_w_x, lru_b_x, lru_lambda, w_branch, w_out, router_w, router_b, w_gu, b_gu, w_dn, b_dn, final_g):
    return _forward(_Cfg(), x, c, ctx, c_ctx, ada_a, ada_b, ada_bias, norm1_g, norm2_g, w_in, conv_a_w, lam_qk,
                    subln_g, lru_conv_w, lru_conv_b, lru_w_a, lru_b_a, lru_w_x, lru_b_x, lru_lambda,
                    w_branch, w_out, router_w, router_b, w_gu, b_gu, w_dn, b_dn, final_g)
```

```python
import functools
import math
from typing import NamedTuple

import jax
import jax.numpy as jnp
from jax import lax
from jax.experimental import pallas as pl
from jax.experimental.pallas import tpu as pltpu

F32 = jnp.float32
BF16 = jnp.bfloat16

LANES = 128
SUBLANES = 8
EPS = 1e-6
ATT_HD = 64
ATT_VD = 2 * ATT_HD
ATT_UNROLL = 8
ATT_LATENT_BQ = 512
ATT_Q_SCALE = ATT_HD ** -0.5 * math.log2(math.e)
ROPE_THETA = 10000.0
ROPE_AXIS_DIM = ATT_HD // 2
ROPE_PAIRS = ROPE_AXIS_DIM // 2
LAMBDA_INIT_BASE = 0.8
LAMBDA_INIT_AMP = 0.6
LAMBDA_INIT_RATE = 0.3
LRU_C = 8.0
SWIGLU_ALPHA = 1.702
SWIGLU_LIMIT = 7.0
N_MOD = 6
VMEM_LIMIT = 56 * 1024 * 1024


class _Cfg(NamedTuple):
    d_model: int = 4096
    batch: int = 2
    seq: int = 4096
    depth: int = 4
    grid_w: int = 64
    ctx_len: int = 256
    branch_w: int = 1024
    n_experts: int = 32
    top_k: int = 4
    expert_ff: int = 256

    @property
    def heads(self):
        return self.branch_w // ATT_VD

    @property
    def m_lat(self):
        return self.batch * self.seq

    @property
    def m_all(self):
        return self.batch * (self.seq + self.ctx_len)

    @property
    def in_cols(self):
        return 8 * self.branch_w + 3 * self.d_model


def _params(n_grid_dims):
    return pltpu.CompilerParams(
        dimension_semantics=("arbitrary",) * n_grid_dims, vmem_limit_bytes=VMEM_LIMIT)


def _dot(a, b):
    return jnp.dot(a, b, preferred_element_type=F32)


def _split_bf16(x):
    hi = x.astype(BF16)
    lo = (x - hi.astype(F32)).astype(BF16)
    return hi, lo


def _dot3(x, w):
    xh, xl = _split_bf16(x)
    wh, wl = _split_bf16(w)
    return _dot(xh, wh) + (_dot(xl, wh) + _dot(xh, wl))


def _sigmoid(x):
    return 1.0 / (1.0 + jnp.exp(-x))


def _ada_kernel(cond_ref, a_ref, b_ref, bias_ref, o_ref):
    cond = cond_ref[...]
    t = _dot3(cond * _sigmoid(cond), a_ref[...])
    o_ref[...] = _dot3(t, b_ref[...]) + bias_ref[...]


def _ada(cfg, cond8, a, b, bias):
    d = cfg.d_model
    rank = a.shape[1]
    out = pl.pallas_call(
        _ada_kernel,
        out_shape=jax.ShapeDtypeStruct((SUBLANES, N_MOD * d), F32),
        grid=(N_MOD,),
        in_specs=[
            pl.BlockSpec((SUBLANES, d), lambda j: (0, 0)),
            pl.BlockSpec((d, rank), lambda j: (0, 0)),
            pl.BlockSpec((rank, d), lambda j: (0, j)),
            pl.BlockSpec((1, d), lambda j: (0, j)),
        ],
        out_specs=pl.BlockSpec((SUBLANES, d), lambda j: (0, j)),
        compiler_params=_params(1),
        name="ada_modulation",
    )(cond8, a, b, bias.reshape(1, N_MOD * d))
    return out.reshape(SUBLANES, N_MOD, d)[: cfg.batch + 1]


def _rms(x, g):
    return x * lax.rsqrt(jnp.mean(x * x, axis=-1, keepdims=True) + EPS) * g


def _norm_mod_kernel(x_ref, g_ref, mod_ref, o_ref, *, shift, scale):
    y = _rms(x_ref[...], g_ref[...])
    y = y * (1.0 + mod_ref[0, scale:scale + 1, :]) + mod_ref[0, shift:shift + 1, :]
    o_ref[...] = y.astype(o_ref.dtype)


def _norm_plain_kernel(x_ref, g_ref, o_ref):
    o_ref[...] = _rms(x_ref[...], g_ref[...]).astype(o_ref.dtype)


def _pack_pair(lo, hi):
    lo_bits = lax.bitcast_convert_type(lo.astype(BF16).astype(F32), jnp.uint32)
    hi_bits = lax.bitcast_convert_type(hi.astype(BF16).astype(F32), jnp.uint32)
    return (lo_bits >> 16) | (hi_bits & jnp.uint32(0xFFFF0000))


def _unpack_pair(word):
    lo = lax.bitcast_convert_type(word << 16, F32)
    hi = lax.bitcast_convert_type(word & jnp.uint32(0xFFFF0000), F32)
    return lo, hi


def _norm_router_kernel(x_ref, g_ref, mod_ref, wr_ref, br_ref, xq_ref, route_ref, counts_ref, carry_ref,
                        *, shift, scale, top_k):
    @pl.when(pl.program_id(0) == 0)
    def _():
        carry_ref[...] = jnp.zeros_like(carry_ref)

    y = _rms(x_ref[...], g_ref[...])
    y = y * (1.0 + mod_ref[0, scale:scale + 1, :]) + mod_ref[0, shift:shift + 1, :]
    half = y.shape[1] // 2
    xq_ref[...] = _pack_pair(y[:, :half], y[:, half:])
    work = _dot3(y, wr_ref[...]) + br_ref[...]
    bm, n_lane = work.shape
    lane = lax.broadcasted_iota(jnp.int32, work.shape, 1).astype(F32)
    vals, hots, ids = [], [], []
    for _ in range(top_k):
        m = jnp.max(work, axis=-1, keepdims=True)
        idx = jnp.min(jnp.where(work == m, lane, float(n_lane)), axis=-1, keepdims=True)
        hot = lane == idx
        vals.append(m)
        hots.append(hot)
        ids.append(idx)
        work = jnp.where(hot, -jnp.inf, work)
    exps = [jnp.exp(v - vals[0]) for v in vals]
    denom = exps[0]
    for e in exps[1:]:
        denom = denom + e
    picked = jnp.zeros(work.shape, F32)
    for hot in hots:
        picked = jnp.where(hot, 1.0, picked)
    tri = (lax.broadcasted_iota(jnp.int32, (bm, bm), 0) > lax.broadcasted_iota(jnp.int32, (bm, bm), 1))
    before = _dot(jnp.where(tri, 1.0, 0.0).astype(BF16), picked.astype(BF16)) + carry_ref[0:1, :]
    carry_ref[...] = carry_ref[...] + jnp.sum(picked, axis=0, keepdims=True)
    counts_ref[...] = carry_ref[...]
    route = jnp.zeros(work.shape, F32)
    for k in range(top_k):
        rank = jnp.sum(jnp.where(hots[k], before, 0.0), axis=-1, keepdims=True)
        route = jnp.where(lane == float(k), ids[k], route)
        route = jnp.where(lane == float(top_k + k), exps[k] / denom, route)
        route = jnp.where(lane == float(2 * top_k + k), rank, route)
    route_ref[...] = route


def _seg_of_block(cfg, i, bm):
    return (i * bm) // cfg.seq


def _norm_mod(cfg, x, g, mod, shift, scale, bm=256):
    m, d = x.shape
    return pl.pallas_call(
        functools.partial(_norm_mod_kernel, shift=shift, scale=scale),
        out_shape=jax.ShapeDtypeStruct((m, d), BF16),
        grid=(m // bm,),
        in_specs=[
            pl.BlockSpec((bm, d), lambda i: (i, 0)),
            pl.BlockSpec((1, d), lambda i: (0, 0)),
            pl.BlockSpec((1, N_MOD, d), lambda i: (_seg_of_block(cfg, i, bm), 0, 0)),
        ],
        out_specs=pl.BlockSpec((bm, d), lambda i: (i, 0)),
        compiler_params=_params(1),
        name="norm_mod",
    )(x, g.reshape(1, d), mod)


def _norm_plain(x, g, m_rows, bm=256):
    d = x.shape[1]
    return pl.pallas_call(
        _norm_plain_kernel,
        out_shape=jax.ShapeDtypeStruct((m_rows, d), F32),
        grid=(m_rows // bm,),
        in_specs=[
            pl.BlockSpec((bm, d), lambda i: (i, 0)),
            pl.BlockSpec((1, d), lambda i: (0, 0)),
        ],
        out_specs=pl.BlockSpec((bm, d), lambda i: (i, 0)),
        compiler_params=_params(1),
        name="final_norm",
    )(x, g.reshape(1, d))


def _norm_router(cfg, x, g, mod, shift, scale, wr_pad, br_pad, bm=256):
    m, d = x.shape
    ep = wr_pad.shape[1]
    return pl.pallas_call(
        functools.partial(_norm_router_kernel, shift=shift, scale=scale, top_k=cfg.top_k),
        out_shape=(jax.ShapeDtypeStruct((m, d // 2), jnp.uint32), jax.ShapeDtypeStruct((m, ep), F32),
                   jax.ShapeDtypeStruct((SUBLANES, ep), F32)),
        grid=(m // bm,),
        in_specs=[
            pl.BlockSpec((bm, d), lambda i: (i, 0)),
            pl.BlockSpec((1, d), lambda i: (0, 0)),
            pl.BlockSpec((1, N_MOD, d), lambda i: (_seg_of_block(cfg, i, bm), 0, 0)),
            pl.BlockSpec((d, ep), lambda i: (0, 0)),
            pl.BlockSpec((1, ep), lambda i: (0, 0)),
        ],
        out_specs=(pl.BlockSpec((bm, d // 2), lambda i: (i, 0)), pl.BlockSpec((bm, ep), lambda i: (i, 0)),
                   pl.BlockSpec((SUBLANES, ep), lambda i: (0, 0))),
        scratch_shapes=[pltpu.VMEM((SUBLANES, ep), F32)],
        compiler_params=_params(1),
        name="norm_router",
    )(x, g.reshape(1, d), mod, wr_pad, br_pad)


CAST_ROWS = 256


def _cast_weight_block(w_ref, wbf_ref):
    @pl.when(pl.program_id(1) == 0)
    def _():
        rows = min(CAST_ROWS, w_ref.shape[0])

        def body(c, carry):
            r0 = pl.multiple_of(c * rows, rows)
            wbf_ref[pl.ds(r0, rows), :] = w_ref[pl.ds(r0, rows), :].astype(BF16)
            return carry

        lax.fori_loop(0, w_ref.shape[0] // rows, body, 0)


def _mm_kernel(x_ref, w_ref, o_ref, wbf_ref):
    _cast_weight_block(w_ref, wbf_ref)
    o_ref[...] = _dot(x_ref[...], wbf_ref[...]).astype(o_ref.dtype)


def _mm_res_kernel(x_ref, w_ref, res_ref, mod_ref, o_ref, wbf_ref, *, gate):
    _cast_weight_block(w_ref, wbf_ref)
    o_ref[...] = res_ref[...] + mod_ref[0, gate:gate + 1, :] * _dot(x_ref[...], wbf_ref[...])


def _matmul(x, w, layer, bm, bn, out_dtype=BF16):
    m, k = x.shape
    n = w.shape[2]
    return pl.pallas_call(
        _mm_kernel,
        out_shape=jax.ShapeDtypeStruct((m, n), out_dtype),
        grid=(n // bn, m // bm),
        in_specs=[
            pl.BlockSpec((bm, k), lambda j, i: (i, 0)),
            pl.BlockSpec((None, k, bn), lambda j, i: (layer, 0, j)),
        ],
        out_specs=pl.BlockSpec((bm, bn), lambda j, i: (i, j)),
        scratch_shapes=[pltpu.VMEM((k, bn), BF16)],
        compiler_params=_params(2),
        name="matmul",
    )(x, w)


def _matmul_residual(cfg, x, w, layer, res, mod, gate, bm, bn):
    m, k = x.shape
    n = w.shape[2]
    return pl.pallas_call(
        functools.partial(_mm_res_kernel, gate=gate),
        out_shape=jax.ShapeDtypeStruct((m, n), F32),
        grid=(n // bn, m // bm),
        in_specs=[
            pl.BlockSpec((bm, k), lambda j, i: (i, 0)),
            pl.BlockSpec((None, k, bn), lambda j, i: (layer, 0, j), pipeline_mode=pl.Buffered(1)),
            pl.BlockSpec((bm, bn), lambda j, i: (i, j)),
            pl.BlockSpec((1, N_MOD, bn), lambda j, i: (_seg_of_block(cfg, i, bm), 0, j)),
        ],
        out_specs=pl.BlockSpec((bm, bn), lambda j, i: (i, j)),
        scratch_shapes=[pltpu.VMEM((k, bn), BF16)],
        compiler_params=_params(2),
        name="matmul_residual",
    )(x, w, res, mod)


def _conv_a_kernel(b_ref, c_ref, h_ref, w_ref, o_ref, *, tt):
    t_len = b_ref.shape[0]
    w = w_ref[...]
    halo = 16

    def prod(r0, n):
        return c_ref[pl.ds(r0, n), :].astype(F32) * h_ref[pl.ds(r0, n), :].astype(F32)

    def chunk(ci, carry):
        r0 = pl.multiple_of(ci * tt, tt)
        u = prod(r0, tt)
        prev = prod(pl.multiple_of(jnp.maximum(r0 - halo, 0), halo), halo)[halo - 1:halo, :]
        prev = jnp.where(r0 == 0, 0.0, prev)
        nxt = prod(pl.multiple_of(jnp.minimum(r0 + tt, t_len - halo), halo), halo)[0:1, :]
        nxt = jnp.where(r0 + tt == t_len, 0.0, nxt)
        row = lax.broadcasted_iota(jnp.int32, u.shape, 0)
        up = jnp.where(row == 0, prev, pltpu.roll(u, 1, 0))
        dn = jnp.where(row == tt - 1, nxt, pltpu.roll(u, tt - 1, 0))
        conv = w[0:1, :] * up + w[1:2, :] * u + w[2:3, :] * dn
        o_ref[pl.ds(r0, tt), :] = (b_ref[pl.ds(r0, tt), :].astype(F32) * conv).astype(o_ref.dtype)
        return carry

    lax.fori_loop(0, t_len // tt, chunk, 0)


def _conv_mixer(cfg, p, w, t_len, row_blk0, n_seq, out_rows, bc=256):
    wd = cfg.branch_w
    ncb = wd // bc
    tt = min(256, t_len)
    spec = lambda off: pl.BlockSpec((t_len, bc), lambda s, j: (row_blk0 + s, off * ncb + j))
    return pl.pallas_call(
        functools.partial(_conv_a_kernel, tt=tt),
        out_shape=jax.ShapeDtypeStruct((out_rows, wd), BF16),
        grid=(n_seq, ncb),
        in_specs=[spec(0), spec(1), spec(2), pl.BlockSpec((3, bc), lambda s, j: (0, j))],
        out_specs=pl.BlockSpec((t_len, bc), lambda s, j: (s, j)),
        compiler_params=_params(2),
        name="conv_mixer",
    )(p, p, p, w)


def _rope_tables(cfg, bt):
    t = jnp.arange(cfg.seq)
    pos = jnp.stack([t // cfg.grid_w, t % cfg.grid_w], axis=-1).astype(F32)
    inv = ROPE_THETA ** (-jnp.arange(ROPE_PAIRS, dtype=F32) * 2.0 / ROPE_AXIS_DIM)
    ang = pos[:, :, None] * inv
    cos, sin = jnp.cos(ang), jnp.sin(ang)
    cmap = jnp.concatenate([cos[:, 0], cos[:, 0], cos[:, 1], cos[:, 1]], axis=-1)
    smap = jnp.concatenate([-sin[:, 0], sin[:, 0], -sin[:, 1], sin[:, 1]], axis=-1)
    ctab = jnp.concatenate([jnp.tile(cmap, (1, 2)), jnp.ones((bt, ATT_VD), F32)], axis=0)
    stab = jnp.concatenate([jnp.tile(smap, (1, 2)), jnp.zeros((bt, ATT_VD), F32)], axis=0)
    return ctab, stab


def _att_prep_kernel(q_ref, k_ref, v_ref, c_ref, s_ref, qt_ref, kr_ref, vt_ref, *, heads):
    cos = c_ref[...]
    sin = s_ref[...]
    lane = lax.broadcasted_iota(jnp.int32, cos.shape, 1)
    low = (lane % ROPE_AXIS_DIM) < ROPE_PAIRS

    def rope(u):
        swapped = jnp.where(low, pltpu.roll(u, ATT_VD - ROPE_PAIRS, 1), pltpu.roll(u, ROPE_PAIRS, 1))
        return u * cos + swapped * sin

    for h in range(heads):
        sl = slice(h * ATT_VD, (h + 1) * ATT_VD)
        qt_ref[sl, :] = (rope(q_ref[:, sl].astype(F32)) * ATT_Q_SCALE).T.astype(qt_ref.dtype)
        kr_ref[:, sl] = rope(k_ref[:, sl].astype(F32)).astype(kr_ref.dtype)
        vt_ref[sl, :] = v_ref[:, sl].astype(F32).T.astype(vt_ref.dtype)


def _att_prep(cfg, p, ctab, stab, bt=256):
    m, wd = cfg.m_all, cfg.branch_w
    n_lat_blk = cfg.m_lat // bt
    n_seq_blk = cfg.seq // bt
    tab_map = lambda i: (jnp.where(i < n_lat_blk, i % n_seq_blk, n_seq_blk), 0)
    return pl.pallas_call(
        functools.partial(_att_prep_kernel, heads=cfg.heads),
        out_shape=(jax.ShapeDtypeStruct((wd, m), BF16), jax.ShapeDtypeStruct((m, wd), BF16),
                   jax.ShapeDtypeStruct((wd, m), BF16)),
        grid=(m // bt,),
        in_specs=[
            pl.BlockSpec((bt, wd), lambda i: (i, 3)),
            pl.BlockSpec((bt, wd), lambda i: (i, 4)),
            pl.BlockSpec((bt, wd), lambda i: (i, 5)),
            pl.BlockSpec((bt, ATT_VD), tab_map),
            pl.BlockSpec((bt, ATT_VD), tab_map),
        ],
        out_specs=(pl.BlockSpec((wd, bt), lambda i: (0, i)), pl.BlockSpec((bt, wd), lambda i: (i, 0)),
                   pl.BlockSpec((wd, bt), lambda i: (0, i))),
        compiler_params=_params(1),
        name="att_prep",
    )(p, p, p, ctab, stab)


def _attn_kernel(lam_ref, g_ref, qt_ref, *refs, seg_lens, tk, lam_init):
    n_seg = len(seg_lens)
    kv = refs[:2 * n_seg]
    o_ref = refs[2 * n_seg]
    s1_ref, s2_ref, acc1_ref, acc2_ref = refs[2 * n_seg + 1:]
    bq = qt_ref.shape[1]

    qt = qt_ref[...].astype(F32)
    sub = lax.broadcasted_iota(jnp.int32, qt.shape, 0)
    q1 = jnp.where(sub < ATT_HD, qt, 0.0).astype(qt_ref.dtype)
    q2 = jnp.where(sub >= ATT_HD, qt, 0.0).astype(qt_ref.dtype)

    def chunks(seg):
        tkk = min(tk, seg_lens[seg])
        return tkk, seg_lens[seg] // tkk

    def pass_a(q, s_ref):
        m = jnp.full((1, bq), -jnp.inf, F32)
        off = 0
        for seg in range(n_seg):
            k_ref = kv[2 * seg]
            tkk, n_chunk = chunks(seg)

            def body(c, m, k_ref=k_ref, tkk=tkk, off=off):
                r0 = pl.multiple_of(c * tkk, tkk)
                s = _dot(k_ref[pl.ds(r0, tkk), :], q)
                s_ref[pl.ds(off + r0, tkk), :] = s
                return jnp.maximum(m, jnp.max(s, axis=0, keepdims=True))

            m = lax.fori_loop(0, n_chunk, body, m, unroll=min(ATT_UNROLL, n_chunk))
            off += seg_lens[seg]
        return m

    def pass_b(s_ref, m, acc_ref):
        acc_ref[...] = jnp.zeros_like(acc_ref)
        l = jnp.zeros((1, bq), F32)
        off = 0
        for seg in range(n_seg):
            vt_ref = kv[2 * seg + 1]
            tkk, n_chunk = chunks(seg)

            def body(c, l, vt_ref=vt_ref, tkk=tkk, off=off):
                r0 = pl.multiple_of(c * tkk, tkk)
                p = jnp.exp2(s_ref[pl.ds(off + r0, tkk), :] - m)
                vt = vt_ref[:, pl.ds(r0, tkk)]
                acc_ref[...] += _dot(vt, p.astype(vt.dtype))
                return l + jnp.sum(p, axis=0, keepdims=True)

            l = lax.fori_loop(0, n_chunk, body, l, unroll=min(ATT_UNROLL, n_chunk))
            off += seg_lens[seg]
        return l

    m1 = pass_a(q1, s1_ref)
    acc1_ref[...] = jnp.zeros_like(acc1_ref)
    l1 = jnp.zeros((1, bq), F32)
    m2 = jnp.full((1, bq), -jnp.inf, F32)
    off = 0
    for seg in range(n_seg):
        k_ref, vt_ref = kv[2 * seg], kv[2 * seg + 1]
        tkk, n_chunk = chunks(seg)
        for c in range(n_chunk):
            r0 = c * tkk
            s = _dot(k_ref[pl.ds(r0, tkk), :], q2)
            s2_ref[pl.ds(off + r0, tkk), :] = s
            m2 = jnp.maximum(m2, jnp.max(s, axis=0, keepdims=True))
            p = jnp.exp2(s1_ref[pl.ds(off + r0, tkk), :] - m1)
            vt = vt_ref[:, pl.ds(r0, tkk)]
            acc1_ref[...] += _dot(vt, p.astype(vt.dtype))
            l1 = l1 + jnp.sum(p, axis=0, keepdims=True)
        off += seg_lens[seg]
    l2 = pass_b(s2_ref, m2, acc2_ref)

    lq = lam_ref[...]
    lam = (jnp.exp(jnp.sum(lq[0:1] * lq[1:2], axis=-1, keepdims=True))
           - jnp.exp(jnp.sum(lq[2:3] * lq[3:4], axis=-1, keepdims=True)) + lam_init)
    ot = acc1_ref[...] * (1.0 / l1) - acc2_ref[...] * (lam / l2)
    ms = jnp.mean(ot * ot, axis=0, keepdims=True)
    y = ot * lax.rsqrt(ms + EPS) * g_ref[...] * (1.0 - lam_init)
    o_ref[...] = y.T.astype(o_ref.dtype)


def _attention(cfg, lam_qk, subln_g, qt, segs, q_col_blk0, n_q_per_batch, out_rows, lam_init, bq=256, tk=512):
    wd = cfg.branch_w
    in_specs = [
        pl.BlockSpec((4, ATT_HD), lambda b, h, i: (0, 0)),
        pl.BlockSpec((ATT_VD, 1), lambda b, h, i: (0, 0)),
        pl.BlockSpec((ATT_VD, bq), lambda b, h, i: (h, q_col_blk0 + b * n_q_per_batch + i)),
    ]
    args = [lam_qk, subln_g.reshape(ATT_VD, 1), qt]
    seg_lens = []
    for k_arr, k_blk, k_col0, vt_arr, length in segs:
        in_specs.append(pl.BlockSpec((length, ATT_VD), lambda b, h, i, f=k_blk, c0=k_col0: (f(b), c0 + h)))
        in_specs.append(pl.BlockSpec((ATT_VD, length), lambda b, h, i, f=k_blk: (h, f(b))))
        args += [k_arr, vt_arr]
        seg_lens.append(length)
    l_tot = sum(seg_lens)
    return pl.pallas_call(
        functools.partial(_attn_kernel, seg_lens=tuple(seg_lens), tk=tk, lam_init=lam_init),
        out_shape=jax.ShapeDtypeStruct((out_rows, wd), BF16),
        grid=(cfg.batch, cfg.heads, n_q_per_batch),
        in_specs=in_specs,
        out_specs=pl.BlockSpec((bq, ATT_VD), lambda b, h, i: (b * n_q_per_batch + i, h)),
        scratch_shapes=[pltpu.VMEM((l_tot, bq), F32), pltpu.VMEM((l_tot, bq), F32),
                        pltpu.VMEM((ATT_VD, bq), F32), pltpu.VMEM((ATT_VD, bq), F32)],
        compiler_params=_params(3),
        name="diff_attention",
    )(*args)


SCAN_PAD = SUBLANES


def _scan_levels(n):
    plan = []
    while n > 1:
        fan = SUBLANES if n % SUBLANES == 0 else n
        assert fan <= SUBLANES, "sequence length must factor into groups of at most 8"
        plan.append((n, fan, n // fan))
        n //= fan
    return plan


def _scan_scratch_rows(n):
    rows = 0
    for _, _, parts in _scan_levels(n):
        rows += 2 * SCAN_PAD + -(-parts // SUBLANES) * SUBLANES
    return max(rows, SUBLANES)


def _scan_inplace(a_ref, b_ref, base, n, rev, sa_ref, sb_ref, off=0):
    if n == 1:
        return
    fan = SUBLANES if n % SUBLANES == 0 else n
    parts = n // fan
    order = list(range(fan))
    if rev:
        order.reverse()

    def rows(r):
        return pl.ds(base + r, parts, stride=fan) if parts > 1 else pl.ds(base + r, 1)

    acc_a = acc_b = None
    for r in order:
        a_r = a_ref[rows(r), :]
        b_r = b_ref[rows(r), :]
        if acc_a is None:
            acc_a, acc_b = a_r, b_r
        else:
            acc_b = a_r * acc_b + b_r
            acc_a = a_r * acc_a
            a_ref[rows(r), :] = acc_a
            b_ref[rows(r), :] = acc_b
    if parts == 1:
        return
    data = off + SCAN_PAD
    ones = jnp.ones((SCAN_PAD, LANES), F32)
    zeros = jnp.zeros((SCAN_PAD, LANES), F32)
    sa_ref[pl.ds(off, SCAN_PAD), :] = ones
    sb_ref[pl.ds(off, SCAN_PAD), :] = zeros
    sa_ref[pl.ds(data + parts, SCAN_PAD), :] = ones
    sb_ref[pl.ds(data + parts, SCAN_PAD), :] = zeros
    sa_ref[pl.ds(data, parts), :] = acc_a
    sb_ref[pl.ds(data, parts), :] = acc_b
    nxt = off + 2 * SCAN_PAD + -(-parts // SUBLANES) * SUBLANES
    _scan_inplace(sa_ref, sb_ref, data, parts, rev, sa_ref, sb_ref, nxt)
    shift = data + 1 if rev else data - 1
    carry_a = sa_ref[pl.ds(shift, parts), :]
    carry_b = sb_ref[pl.ds(shift, parts), :]
    for r in order:
        a_r = a_ref[rows(r), :]
        b_ref[rows(r), :] = b_ref[rows(r), :] + a_r * carry_b
        a_ref[rows(r), :] = a_r * carry_a


def _softplus(x):
    return jnp.maximum(x, 0.0) + jnp.log1p(jnp.exp(-jnp.abs(x)))


def _gelu_tanh(x):
    return 0.5 * x * (1.0 + jnp.tanh(math.sqrt(2.0 / math.pi) * (x + 0.044715 * (x * x * x))))


def _lru_kernel(lxl_ref, lyl_ref, lxc_ref, lyc_ref, cw_ref, cb_ref, wa_ref, ba_ref, wx_ref, bx_ref, lam_ref,
                ol_ref, oc_ref, pad_ref, a0_ref, b0_ref, a1_ref, b1_ref, sa_ref, sb_ref, *, tt):
    l_ctx = lxc_ref.shape[0]
    l_lat = lxl_ref.shape[0]
    cw = cw_ref[...]
    cb = cb_ref[...]
    ab_refs = ((a0_ref, b0_ref), (a1_ref, b1_ref))
    neg_sp = [-LRU_C * _softplus(-lam_ref[d:d + 1, :]) for d in range(2)]
    wa = [wa_ref[d].astype(BF16) for d in range(2)]
    wx = [wx_ref[d].astype(BF16) for d in range(2)]

    for lx_ref, base, length, is_ctx in ((lxc_ref, 0, l_ctx, True), (lxl_ref, l_ctx, l_lat, False)):
        ttk = min(tt, length)
        pad_ref[pl.ds(0, SUBLANES), :] = jnp.zeros((SUBLANES, LANES), F32)
        pad_ref[pl.ds(SUBLANES + length, SUBLANES), :] = jnp.zeros((SUBLANES, LANES), F32)

        def fill(ci, carry, lx_ref=lx_ref, ttk=ttk):
            r0 = pl.multiple_of(ci * ttk, ttk)
            pad_ref[pl.ds(SUBLANES + r0, ttk), :] = lx_ref[pl.ds(r0, ttk), :].astype(F32)
            return carry

        lax.fori_loop(0, length // ttk, fill, 0)

        def gates(ci, carry, base=base, length=length, is_ctx=is_ctx, ttk=ttk):
            r0 = pl.multiple_of(ci * ttk, ttk)
            xx = pad_ref[pl.ds(r0, ttk + 2 * SUBLANES), :]
            u = (cw[0:1, :] * xx[6:6 + ttk] + cw[1:2, :] * xx[7:7 + ttk]
                 + cw[2:3, :] * xx[8:8 + ttk] + cw[3:4, :] * xx[9:9 + ttk]) + cb
            ub = u.astype(BF16)
            row = r0 + lax.broadcasted_iota(jnp.int32, u.shape, 0)
            for d in range(2):
                r_gate = _sigmoid(_dot(ub, wa[d]) + ba_ref[d:d + 1, :])
                i_gate = _sigmoid(_dot(ub, wx[d]) + bx_ref[d:d + 1, :])
                log_a = neg_sp[d] * r_gate
                a = jnp.exp(log_a)
                mult = jnp.sqrt(-jnp.tanh(log_a) * (a * a + 1.0))
                if is_ctx:
                    mult = jnp.where(row == (length - 1 if d == 1 else 0), 1.0, mult)
                a_ref, b_ref = ab_refs[d]
                a_ref[pl.ds(base + r0, ttk), :] = a
                b_ref[pl.ds(base + r0, ttk), :] = mult * (i_gate * u)
            return carry

        lax.fori_loop(0, length // ttk, gates, 0, unroll=min(2, length // ttk))

    for d, rev in ((0, False), (1, True)):
        a_ref, b_ref = ab_refs[d]
        _scan_inplace(a_ref, b_ref, 0, l_ctx, rev, sa_ref, sb_ref)
        _scan_inplace(a_ref, b_ref, l_ctx, l_lat, rev, sa_ref, sb_ref)

    h0 = (b0_ref[pl.ds(l_ctx - 1, 1), :], b1_ref[pl.ds(0, 1), :])

    for ly_ref, o_ref, base, length, is_ctx in ((lyc_ref, oc_ref, 0, l_ctx, True), (lyl_ref, ol_ref, l_ctx, l_lat, False)):
        ttk = min(tt, length)

        def emit(ci, carry, ly_ref=ly_ref, o_ref=o_ref, base=base, is_ctx=is_ctx, ttk=ttk):
            r0 = pl.multiple_of(ci * ttk, ttk)
            h = b0_ref[pl.ds(base + r0, ttk), :] + b1_ref[pl.ds(base + r0, ttk), :]
            if not is_ctx:
                h = h + a0_ref[pl.ds(base + r0, ttk), :] * h0[0] + a1_ref[pl.ds(base + r0, ttk), :] * h0[1]
            o_ref[pl.ds(r0, ttk), :] = (h * _gelu_tanh(ly_ref[pl.ds(r0, ttk), :].astype(F32))).astype(o_ref.dtype)
            return carry

        lax.fori_loop(0, length // ttk, emit, 0)


def _lru(cfg, p, conv_w, conv_b, w_a, b_a, w_x, b_x, lam_p, tt=256):
    wd, s, lc = cfg.branch_w, cfg.seq, cfg.ctx_len
    nb = wd // LANES
    l_tot = s + lc
    ctx_blk0 = cfg.m_lat // lc
    lx_col, ly_col = 6 * nb, 7 * nb
    vec2 = pl.BlockSpec((2, LANES), lambda b, n: (0, n))
    wspec = pl.BlockSpec((2, None, LANES, LANES), lambda b, n: (0, n, 0, 0))
    scan_rows = max(_scan_scratch_rows(s), _scan_scratch_rows(lc))
    return pl.pallas_call(
        functools.partial(_lru_kernel, tt=tt),
        out_shape=(jax.ShapeDtypeStruct((cfg.m_lat, wd), BF16), jax.ShapeDtypeStruct((cfg.batch * lc, wd), BF16)),
        grid=(cfg.batch, nb),
        in_specs=[
            pl.BlockSpec((s, LANES), lambda b, n: (b, lx_col + n)),
            pl.BlockSpec((s, LANES), lambda b, n: (b, ly_col + n)),
            pl.BlockSpec((lc, LANES), lambda b, n: (ctx_blk0 + b, lx_col + n)),
            pl.BlockSpec((lc, LANES), lambda b, n: (ctx_blk0 + b, ly_col + n)),
            pl.BlockSpec((4, LANES), lambda b, n: (0, n)),
            pl.BlockSpec((1, LANES), lambda b, n: (0, n)),
            wspec, vec2, wspec, vec2, vec2,
        ],
        out_specs=(pl.BlockSpec((s, LANES), lambda b, n: (b, n)), pl.BlockSpec((lc, LANES), lambda b, n: (b, n))),
        scratch_shapes=[pltpu.VMEM((s + 2 * SUBLANES, LANES), F32)]
        + [pltpu.VMEM((l_tot, LANES), F32)] * 4
        + [pltpu.VMEM((scan_rows, LANES), F32)] * 2,
        compiler_params=_params(2),
        name="rglru",
    )(p, p, p, p, conv_w, conv_b.reshape(1, wd), w_a, b_a, w_x, b_x, lam_p)


def _merge_kernel(*refs, wd, n_lat_blk):
    branch_refs, (ga_ref, gb_ref, gc_ref, w_ref, o_ref, wbf_ref) = refs[:6], refs[6:]
    _cast_weight_block(w_ref, wbf_ref)
    is_lat = pl.program_id(1) < n_lat_blk
    acc = None
    for i, g_ref in enumerate((ga_ref, gb_ref, gc_ref)):
        x = jnp.where(is_lat, branch_refs[2 * i][...], branch_refs[2 * i + 1][...])
        g = _sigmoid(g_ref[...].astype(F32))
        term = g * _dot(x, wbf_ref[i * wd:(i + 1) * wd, :])
        acc = term if acc is None else acc + term
    o_ref[...] = acc.astype(o_ref.dtype)


def _merge(cfg, branches, p, wb, layer, bm, bn):
    m, wd, d = cfg.m_all, cfg.branch_w, cfg.d_model
    g_blk0 = 8 * wd // bn
    nd = d // bn
    n_lat_blk = cfg.m_lat // bm
    lat_spec = pl.BlockSpec((bm, wd), lambda j, i: (jnp.minimum(i, n_lat_blk - 1), 0))
    ctx_spec = pl.BlockSpec((bm, wd), lambda j, i: (jnp.maximum(i - n_lat_blk, 0), 0))
    gspec = lambda br: pl.BlockSpec((bm, bn), lambda j, i: (i, g_blk0 + br * nd + j))
    args = [a for pair in branches for a in pair]
    return pl.pallas_call(
        functools.partial(_merge_kernel, wd=wd, n_lat_blk=n_lat_blk),
        out_shape=jax.ShapeDtypeStruct((m, d), BF16),
        grid=(nd, m // bm),
        in_specs=[lat_spec, ctx_spec] * 3 + [gspec(0), gspec(1), gspec(2),
                  pl.BlockSpec((None, 3 * wd, bn), lambda j, i: (layer, 0, j))],
        out_specs=pl.BlockSpec((bm, bn), lambda j, i: (i, j)),
        scratch_shapes=[pltpu.VMEM((3 * wd, bn), BF16)],
        compiler_params=_params(2),
        name="merge_branches",
    )(*args, p, p, p, wb)


MOE_TILE = 256
MOE_BLOCK = 256


def _moe_slots(cfg):
    n_tiles = cfg.m_all * cfg.top_k // MOE_TILE + cfg.n_experts
    return n_tiles, n_tiles * MOE_TILE


def _assignment_index(t, top_k):
    per_row = LANES // top_k
    assert per_row * top_k == LANES and per_row & (per_row - 1) == 0
    shift = per_row.bit_length() - 1
    return lax.shift_right_logical(t, shift), (t & (per_row - 1)) * top_k


def _dispatch_kernel(fill_ref, nchunk_ref, nv_ref, pos_ref, xq_ref, xs_hbm, zero_ref, pos_smem, sem, idx_sem,
                     *, n_experts, top_k, n_tiles):
    i = pl.program_id(0)
    n_assign = pos_ref.shape[0] * pos_ref.shape[1]
    bt = n_assign // top_k

    @pl.when(i == 0)
    def _():
        zero_ref[...] = jnp.zeros_like(zero_ref)
        piece = zero_ref.at[pl.ds(0, SUBLANES)]
        total = 0
        for e in range(n_experts):
            def fill_piece(c, carry, e=e):
                start = pl.multiple_of(fill_ref[e] + c * SUBLANES, SUBLANES)
                pltpu.make_async_copy(piece, xs_hbm.at[pl.ds(start, SUBLANES)], sem).start()
                return carry

            lax.fori_loop(0, nchunk_ref[e], fill_piece, 0)
            total = total + nchunk_ref[e]

        def idle_tile(j, carry):
            start = pl.multiple_of(j * MOE_TILE, MOE_TILE)
            pltpu.make_async_copy(zero_ref, xs_hbm.at[pl.ds(start, MOE_TILE)], sem).start()
            return carry

        lax.fori_loop(nv_ref[0], n_tiles, idle_tile, 0)

        def drain_piece(c, carry):
            pltpu.make_async_copy(piece, xs_hbm.at[pl.ds(0, SUBLANES)], sem).wait()
            return carry

        lax.fori_loop(0, total, drain_piece, 0)

        def drain_tile(j, carry):
            pltpu.make_async_copy(zero_ref, xs_hbm.at[pl.ds(0, MOE_TILE)], sem).wait()
            return carry

        lax.fori_loop(nv_ref[0], n_tiles, drain_tile, 0)

    idx_copy = pltpu.make_async_copy(pos_ref, pos_smem, idx_sem)
    idx_copy.start()
    idx_copy.wait()

    def issue(t, carry):
        row, col = _assignment_index(t, top_k)
        for k in range(top_k):
            dst = pos_smem[row, col + k]
            pltpu.make_async_copy(xq_ref.at[pl.ds(t, 1)], xs_hbm.at[pl.ds(dst, 1)], sem).start()
        return carry

    lax.fori_loop(0, bt, issue, 0, unroll=2)
    pltpu.make_async_copy(xs_hbm.at[pl.ds(0, n_assign)], xs_hbm.at[pl.ds(0, n_assign)], sem).wait()


def _dispatch(cfg, fill, nchunk, n_valid, pos2d, xq):
    m, half = xq.shape
    n_tiles, slot_rows = _moe_slots(cfg)
    bt = 2 * MOE_BLOCK if m % (2 * MOE_BLOCK) == 0 else MOE_BLOCK
    rows_per_step = bt * cfg.top_k // LANES
    return pl.pallas_call(
        functools.partial(_dispatch_kernel, n_experts=cfg.n_experts, top_k=cfg.top_k, n_tiles=n_tiles),
        out_shape=jax.ShapeDtypeStruct((slot_rows, half), jnp.uint32),
        grid_spec=pltpu.PrefetchScalarGridSpec(
            num_scalar_prefetch=3,
            grid=(m // bt,),
            in_specs=[pl.BlockSpec((rows_per_step, LANES), lambda i, *_: (i, 0)),
                      pl.BlockSpec((bt, half), lambda i, *_: (i, 0))],
            out_specs=pl.BlockSpec(memory_space=pl.ANY),
            scratch_shapes=[pltpu.VMEM((MOE_TILE, half), jnp.uint32),
                            pltpu.SMEM((rows_per_step, LANES), jnp.int32),
                            pltpu.SemaphoreType.DMA(()), pltpu.SemaphoreType.DMA(())]),
        compiler_params=_params(1),
        name="moe_dispatch",
    )(fill, nchunk, n_valid, pos2d, xq)


def _expert_kernel(te_ref, nv_ref, xs_ref, perm_ref, wgu_ref, bgu_ref, wdn_ref, bdn_ref, ys_ref,
                   wgu_bf, wdn_bf, *, ff):
    j = pl.program_id(0)
    d = wgu_ref.shape[0]
    half = d // 2
    e = te_ref[j]
    e_prev = te_ref[jnp.maximum(j - 1, 0)]

    @pl.when((j == 0) | (e != e_prev))
    def _():
        rows = min(512, d)

        def cast_gu(c, carry):
            r0 = pl.multiple_of(c * rows, rows)
            w = wgu_ref[pl.ds(r0, rows), :].astype(BF16)
            wgu_bf[pl.ds(r0, rows), :] = _dot(w, perm_ref[...]).astype(BF16)
            return carry

        lax.fori_loop(0, d // rows, cast_gu, 0, unroll=min(4, d // rows))
        rows_dn = 32

        def cast_dn(c, carry):
            r0 = pl.multiple_of(c * rows_dn, rows_dn)
            wdn_bf[pl.ds(r0, rows_dn), :] = wdn_ref[pl.ds(r0, rows_dn), :].astype(BF16)
            return carry

        lax.fori_loop(0, ff // rows_dn, cast_dn, 0)

    @pl.when(j < nv_ref[0])
    def _():
        lo, hi = _unpack_pair(xs_ref[...])
        h = (_dot(lo.astype(BF16), wgu_bf[pl.ds(0, half), :]) + _dot(hi.astype(BF16), wgu_bf[pl.ds(half, half), :])
             + bgu_ref[...])
        h_glu = jnp.minimum(h[:, :ff], SWIGLU_LIMIT)
        h_lin = jnp.clip(h[:, ff:], -SWIGLU_LIMIT, SWIGLU_LIMIT)
        act = (h_glu * _sigmoid(SWIGLU_ALPHA * h_glu) * (h_lin + 1.0)).astype(BF16)
        cw = min(512, half)
        for c in range(half // cw):
            lo_sl = slice(c * cw, (c + 1) * cw)
            hi_sl = slice(half + c * cw, half + (c + 1) * cw)
            y_lo = _dot(act, wdn_bf[:, lo_sl]) + bdn_ref[:, lo_sl]
            y_hi = _dot(act, wdn_bf[:, hi_sl]) + bdn_ref[:, hi_sl]
            ys_ref[:, lo_sl] = _pack_pair(y_lo, y_hi)

    @pl.when(j >= nv_ref[0])
    def _():
        ys_ref[...] = jnp.zeros_like(ys_ref)


def _experts(cfg, tile_e, n_valid, xs, perm, w_gu, b_gu_perm, w_dn, b_dn, layer):
    d, ff, ne = cfg.d_model, cfg.expert_ff, cfg.n_experts
    half = d // 2
    n_tiles, slot_rows = _moe_slots(cfg)
    tile_blk = lambda j, te, nv: (jnp.minimum(j, nv[0] - 1), 0)
    by_expert = lambda j, te, nv: (te[j], 0, 0)
    by_layer_expert = lambda j, te, nv: (layer, te[j], 0, 0)
    return pl.pallas_call(
        functools.partial(_expert_kernel, ff=ff),
        out_shape=jax.ShapeDtypeStruct((slot_rows, half), jnp.uint32),
        grid_spec=pltpu.PrefetchScalarGridSpec(
            num_scalar_prefetch=2,
            grid=(n_tiles,),
            in_specs=[
                pl.BlockSpec((MOE_TILE, half), tile_blk),
                pl.BlockSpec((2 * ff, 2 * ff), lambda j, te, nv: (0, 0)),
                pl.BlockSpec((None, None, d, 2 * ff), by_layer_expert),
                pl.BlockSpec((None, 1, 2 * ff), by_expert),
                pl.BlockSpec((None, None, ff, d), by_layer_expert),
                pl.BlockSpec((None, 1, d), by_expert),
            ],
            out_specs=pl.BlockSpec((MOE_TILE, half), lambda j, te, nv: (j, 0)),
            scratch_shapes=[pltpu.VMEM((d, 2 * ff), BF16), pltpu.VMEM((ff, d), BF16)]),
        compiler_params=_params(1),
        name="moe_experts",
    )(tile_e, n_valid, xs, perm, w_gu, b_gu_perm.reshape(ne, 1, 2 * ff), w_dn, b_dn.reshape(ne, 1, d))


def _combine_kernel(pos_ref, pos_next_ref, route_ref, res_ref, mod_ref, ys_hbm, o_ref, buf_ref, pos_smem,
                    sems, idx_sem, *, gate, top_k):
    i = pl.program_id(0)
    n = pl.num_programs(0)
    bt, d = res_ref.shape
    half = d // 2
    slot = lax.rem(i, 2)
    other = 1 - slot

    def load_positions(src_ref):
        idx_copy = pltpu.make_async_copy(src_ref, pos_smem, idx_sem)
        idx_copy.start()
        idx_copy.wait()

    def gather_token(t, to_slot):
        row, col = _assignment_index(t, top_k)
        for k in range(top_k):
            src = pos_smem[row, col + k]
            pltpu.make_async_copy(ys_hbm.at[pl.ds(src, 1)], buf_ref.at[to_slot, k, pl.ds(t, 1)],
                                  sems.at[to_slot]).start()

    def wait_slot(s):
        pltpu.make_async_copy(buf_ref.at[s], buf_ref.at[s], sems.at[s]).wait()

    @pl.when(i == 0)
    def _():
        load_positions(pos_ref)

        def first(t, carry):
            gather_token(t, 0)
            return carry

        lax.fori_loop(0, bt, first, 0, unroll=2)

    wait_slot(slot)
    load_positions(pos_next_ref)

    rb = 2 * SUBLANES
    cw = min(512, half)

    def rows(r, carry):
        r0 = pl.multiple_of(r * rb, rb)
        rs = pl.ds(r0, rb)
        for u in range(rb):
            gather_token(r0 + u, other)
        weights = [route_ref[rs, top_k + k:top_k + k + 1] for k in range(top_k)]
        for c in range(half // cw):
            lo_sl = slice(c * cw, (c + 1) * cw)
            hi_sl = slice(half + c * cw, half + (c + 1) * cw)
            acc_lo = acc_hi = None
            for k in range(top_k):
                lo, hi = _unpack_pair(buf_ref[slot, k, rs, lo_sl])
                acc_lo = weights[k] * lo if acc_lo is None else acc_lo + weights[k] * lo
                acc_hi = weights[k] * hi if acc_hi is None else acc_hi + weights[k] * hi
            o_ref[rs, lo_sl] = res_ref[rs, lo_sl] + mod_ref[0, gate:gate + 1, lo_sl] * acc_lo
            o_ref[rs, hi_sl] = res_ref[rs, hi_sl] + mod_ref[0, gate:gate + 1, hi_sl] * acc_hi
        return carry

    lax.fori_loop(0, bt // rb, rows, 0)

    @pl.when(i == n - 1)
    def _():
        wait_slot(other)


def _combine(cfg, pos2d, route, res, mod, ys, gate):
    m, d = res.shape
    half = d // 2
    ep = route.shape[1]
    bt = MOE_BLOCK
    rows_per_step = bt * cfg.top_k // LANES
    return pl.pallas_call(
        functools.partial(_combine_kernel, gate=gate, top_k=cfg.top_k),
        out_shape=jax.ShapeDtypeStruct((m, d), F32),
        grid=(m // bt,),
        in_specs=[
            pl.BlockSpec((rows_per_step, LANES), lambda i: (i, 0)),
            pl.BlockSpec((rows_per_step, LANES), lambda i: (jnp.minimum(i + 1, m // bt - 1), 0)),
            pl.BlockSpec((bt, ep), lambda i: (i, 0)),
            pl.BlockSpec((bt, d), lambda i: (i, 0)),
            pl.BlockSpec((1, N_MOD, d), lambda i: (_seg_of_block(cfg, i, bt), 0, 0)),
            pl.BlockSpec(memory_space=pl.ANY),
        ],
        out_specs=pl.BlockSpec((bt, d), lambda i: (i, 0)),
        scratch_shapes=[pltpu.VMEM((2, cfg.top_k, bt, half), jnp.uint32),
                        pltpu.SMEM((rows_per_step, LANES), jnp.int32),
                        pltpu.SemaphoreType.DMA((2,)), pltpu.SemaphoreType.DMA(())],
        compiler_params=_params(1),
        name="moe_combine",
    )(pos2d, pos2d, route, res, mod, ys)


def _moe_plan(cfg, route, counts8):
    k, ne = cfg.top_k, cfg.n_experts
    n_tiles, _ = _moe_slots(cfg)
    expert = route[:, :k].astype(jnp.int32)
    rank = route[:, 2 * k:3 * k].astype(jnp.int32)
    counts = counts8[0, :ne].astype(jnp.int32)
    padded = (counts + MOE_TILE - 1) // MOE_TILE * MOE_TILE
    region_end = jnp.cumsum(padded)
    region_start = region_end - padded
    pos = region_start[expert] + rank
    fill = (region_start + counts) // SUBLANES * SUBLANES
    nchunk = (region_end - fill) // SUBLANES
    n_valid = region_end[-1] // MOE_TILE
    tile_start = jnp.arange(n_tiles, dtype=jnp.int32) * MOE_TILE
    tile_e = jnp.sum(region_end[None, :] <= tile_start[:, None], axis=1)
    tile_e = jnp.minimum(tile_e, ne - 1).astype(jnp.int32)
    tile_e = tile_e[jnp.minimum(jnp.arange(n_tiles), n_valid - 1)]
    return (pos.reshape(-1, LANES), fill.astype(jnp.int32), nchunk.astype(jnp.int32), tile_e,
            n_valid.reshape(1).astype(jnp.int32))


def _deinterleave_perm(ff):
    col = jnp.arange(2 * ff)
    src = jnp.where(col < ff, 2 * col, 2 * (col - ff) + 1)
    return (jnp.arange(2 * ff)[:, None] == src[None, :]).astype(BF16)


def _forward(cfg, x, c, ctx, c_ctx, ada_a, ada_b, ada_bias, norm1_g, norm2_g, w_in, conv_a_w, lam_qk,
             subln_g, lru_conv_w, lru_conv_b, lru_w_a, lru_b_a, lru_w_x, lru_b_x, lru_lambda,
             w_branch, w_out, router_w, router_b, w_gu, b_gu, w_dn, b_dn, final_g):
    d, wd, bsz, s, lc = cfg.d_model, cfg.branch_w, cfg.batch, cfg.seq, cfg.ctx_len
    m_lat, m_all, ne = cfg.m_lat, cfg.m_all, cfg.n_experts
    assert bsz * lc <= s and s % lc == 0 and bsz + 1 <= SUBLANES
    bm = min(512, bsz * lc)
    bn = min(1024, d)
    bt = min(256, lc)
    ep = -(-ne // LANES) * LANES

    xs = jnp.concatenate([x.reshape(m_lat, d), ctx.reshape(bsz * lc, d)], axis=0)
    cond8 = jnp.zeros((SUBLANES, d), F32).at[:bsz].set(c).at[bsz].set(c_ctx)
    ctab, stab = _rope_tables(cfg, bt)
    ctx_blk = lambda b: m_lat // lc + b
    perm = _deinterleave_perm(cfg.expert_ff)

    for l in range(cfg.depth):
        lam_init = LAMBDA_INIT_BASE - LAMBDA_INIT_AMP * math.exp(-LAMBDA_INIT_RATE * l)
        mod = _ada(cfg, cond8, ada_a[l], ada_b[l], ada_bias[l])

        xn = _norm_mod(cfg, xs, norm1_g[l], mod, 0, 1)
        p = _matmul(xn, w_in, l, bm, bn)
        oa = (_conv_mixer(cfg, p, conv_a_w[l], s, 0, bsz, m_lat),
              _conv_mixer(cfg, p, conv_a_w[l], lc, m_lat // lc, bsz, bsz * lc))
        qt, kr, vt = _att_prep(cfg, p, ctab, stab, bt)
        nh = wd // ATT_VD
        ctx_seg = (p, ctx_blk, 4 * nh, vt, lc)
        lat_seg = (kr, lambda b: b, 0, vt, s)
        bq_lat = min(ATT_LATENT_BQ, s)
        ob_lat = _attention(cfg, lam_qk[l], subln_g[l], qt, [ctx_seg, lat_seg], 0, s // bq_lat, m_lat, lam_init,
                            bq=bq_lat)
        ob_ctx = _attention(cfg, lam_qk[l], subln_g[l], qt, [ctx_seg], m_lat // bt, lc // bt, bsz * lc, lam_init, bq=bt)
        oc = _lru(cfg, p, lru_conv_w[l], lru_conv_b[l], lru_w_a[l], lru_b_a[l],
                  lru_w_x[l], lru_b_x[l], lru_lambda[l])
        y = _merge(cfg, (oa, (ob_lat, ob_ctx), oc), p, w_branch, l, bm, bn)
        xs = _matmul_residual(cfg, y, w_out, l, xs, mod, 2, bm, bn)

        wr_pad = jnp.zeros((d, ep), F32).at[:, :ne].set(router_w[l])
        br_pad = jnp.full((1, ep), -jnp.inf, F32).at[0, :ne].set(router_b[l])
        xq, route, counts8 = _norm_router(cfg, xs, norm2_g[l], mod, 3, 4, wr_pad, br_pad)
        pos2d, fill, nchunk, tile_e, n_valid = _moe_plan(cfg, route, counts8)
        slots = _dispatch(cfg, fill, nchunk, n_valid, pos2d, xq)
        b_gu_perm = jnp.concatenate([b_gu[l][:, 0::2], b_gu[l][:, 1::2]], axis=-1)
        ys = _experts(cfg, tile_e, n_valid, slots, perm, w_gu, b_gu_perm, w_dn, b_dn[l], l)
        xs = _combine(cfg, pos2d, route, xs, mod, ys, 5)

    return _norm_plain(xs, final_g, m_lat).reshape(bsz, s, d)


def kernel(x, c, ctx, c_ctx, ada_a, ada_b, ada_bias, norm1_g, norm2_g, w_in, conv_a_w, lam_qk, subln_g, lru_conv_w, lru_conv_b, lru_w_a, lru_b_a, lru_w_x, lru_b_x, lru_lambda, w_branch, w_out, router_w, router_b, w_gu, b_gu, w_dn, b_dn, final_g):
    return _forward(_Cfg(), x, c, ctx, c_ctx, ada_a, ada_b, ada_bias, norm1_g, norm2_g, w_in, conv_a_w, lam_qk,
                    subln_g, lru_conv_w, lru_conv_b, lru_w_a, lru_b_a, lru_w_x, lru_b_x, lru_lambda,
                    w_branch, w_out, router_w, router_b, w_gu, b_gu, w_dn, b_dn, final_g)
```

```python
import functools
import math
from typing import NamedTuple

import jax
import jax.numpy as jnp
from jax import lax
from jax.experimental import pallas as pl
from jax.experimental.pallas import tpu as pltpu

F32 = jnp.float32
BF16 = jnp.bfloat16

LANES = 128
SUBLANES = 8
EPS = 1e-6
ATT_HD = 64
ATT_VD = 2 * ATT_HD
ATT_UNROLL = 8
ATT_LATENT_BQ = 512
ATT_Q_SCALE = ATT_HD ** -0.5 * math.log2(math.e)
ROPE_THETA = 10000.0
ROPE_AXIS_DIM = ATT_HD // 2
ROPE_PAIRS = ROPE_AXIS_DIM // 2
LAMBDA_INIT_BASE = 0.8
LAMBDA_INIT_AMP = 0.6
LAMBDA_INIT_RATE = 0.3
LRU_C = 8.0
SWIGLU_ALPHA = 1.702
SWIGLU_LIMIT = 7.0
N_MOD = 6
VMEM_LIMIT = 56 * 1024 * 1024


class _Cfg(NamedTuple):
    d_model: int = 4096
    batch: int = 2
    seq: int = 4096
    depth: int = 4
    grid_w: int = 64
    ctx_len: int = 256
    branch_w: int = 1024
    n_experts: int = 32
    top_k: int = 4
    expert_ff: int = 256

    @property
    def heads(self):
        return self.branch_w // ATT_VD

    @property
    def m_lat(self):
        return self.batch * self.seq

    @property
    def m_all(self):
        return self.batch * (self.seq + self.ctx_len)

    @property
    def in_cols(self):
        return 8 * self.branch_w + 3 * self.d_model


def _params(n_grid_dims):
    return pltpu.CompilerParams(
        dimension_semantics=("arbitrary",) * n_grid_dims, vmem_limit_bytes=VMEM_LIMIT)


def _dot(a, b):
    return jnp.dot(a, b, preferred_element_type=F32)


def _split_bf16(x):
    hi = x.astype(BF16)
    lo = (x - hi.astype(F32)).astype(BF16)
    return hi, lo


def _dot3(x, w):
    xh, xl = _split_bf16(x)
    wh, wl = _split_bf16(w)
    return _dot(xh, wh) + (_dot(xl, wh) + _dot(xh, wl))


def _sigmoid(x):
    return 1.0 / (1.0 + jnp.exp(-x))


def _ada_kernel(cond_ref, a_ref, b_ref, bias_ref, o_ref):
    cond = cond_ref[...]
    t = _dot3(cond * _sigmoid(cond), a_ref[...])
    o_ref[...] = _dot3(t, b_ref[...]) + bias_ref[...]


def _ada(cfg, cond8, a, b, bias):
    d = cfg.d_model
    rank = a.shape[1]
    out = pl.pallas_call(
        _ada_kernel,
        out_shape=jax.ShapeDtypeStruct((SUBLANES, N_MOD * d), F32),
        grid=(N_MOD,),
        in_specs=[
            pl.BlockSpec((SUBLANES, d), lambda j: (0, 0)),
            pl.BlockSpec((d, rank), lambda j: (0, 0)),
            pl.BlockSpec((rank, d), lambda j: (0, j)),
            pl.BlockSpec((1, d), lambda j: (0, j)),
        ],
        out_specs=pl.BlockSpec((SUBLANES, d), lambda j: (0, j)),
        compiler_params=_params(1),
        name="ada_modulation",
    )(cond8, a, b, bias.reshape(1, N_MOD * d))
    return out.reshape(SUBLANES, N_MOD, d)[: cfg.batch + 1]


def _rms(x, g):
    return x * lax.rsqrt(jnp.mean(x * x, axis=-1, keepdims=True) + EPS) * g


def _norm_mod_kernel(x_ref, g_ref, mod_ref, o_ref, *, shift, scale):
    y = _rms(x_ref[...], g_ref[...])
    y = y * (1.0 + mod_ref[0, scale:scale + 1, :]) + mod_ref[0, shift:shift + 1, :]
    o_ref[...] = y.astype(o_ref.dtype)


def _norm_plain_kernel(x_ref, g_ref, o_ref):
    o_ref[...] = _rms(x_ref[...], g_ref[...]).astype(o_ref.dtype)


def _pack_pair(lo, hi):
    lo_bits = lax.bitcast_convert_type(lo.astype(BF16).astype(F32), jnp.uint32)
    hi_bits = lax.bitcast_convert_type(hi.astype(BF16).astype(F32), jnp.uint32)
    return (lo_bits >> 16) | (hi_bits & jnp.uint32(0xFFFF0000))


def _unpack_pair(word):
    lo = lax.bitcast_convert_type(word << 16, F32)
    hi = lax.bitcast_convert_type(word & jnp.uint32(0xFFFF0000), F32)
    return lo, hi


def _norm_router_kernel(x_ref, g_ref, mod_ref, wr_ref, br_ref, xq_ref, route_ref, counts_ref, carry_ref,
                        *, shift, scale, top_k):
    @pl.when(pl.program_id(0) == 0)
    def _():
        carry_ref[...] = jnp.zeros_like(carry_ref)

    y = _rms(x_ref[...], g_ref[...])
    y = y * (1.0 + mod_ref[0, scale:scale + 1, :]) + mod_ref[0, shift:shift + 1, :]
    half = y.shape[1] // 2
    xq_ref[...] = _pack_pair(y[:, :half], y[:, half:])
    work = _dot3(y, wr_ref[...]) + br_ref[...]
    bm, n_lane = work.shape
    lane = lax.broadcasted_iota(jnp.int32, work.shape, 1).astype(F32)
    vals, hots, ids = [], [], []
    for _ in range(top_k):
        m = jnp.max(work, axis=-1, keepdims=True)
        idx = jnp.min(jnp.where(work == m, lane, float(n_lane)), axis=-1, keepdims=True)
        hot = lane == idx
        vals.append(m)
        hots.append(hot)
        ids.append(idx)
        work = jnp.where(hot, -jnp.inf, work)
    exps = [jnp.exp(v - vals[0]) for v in vals]
    denom = exps[0]
    for e in exps[1:]:
        denom = denom + e
    picked = jnp.zeros(work.shape, F32)
    for hot in hots:
        picked = jnp.where(hot, 1.0, picked)
    tri = (lax.broadcasted_iota(jnp.int32, (bm, bm), 0) > lax.broadcasted_iota(jnp.int32, (bm, bm), 1))
    before = _dot(jnp.where(tri, 1.0, 0.0).astype(BF16), picked.astype(BF16)) + carry_ref[0:1, :]
    carry_ref[...] = carry_ref[...] + jnp.sum(picked, axis=0, keepdims=True)
    counts_ref[...] = carry_ref[...]
    route = jnp.zeros(work.shape, F32)
    for k in range(top_k):
        rank = jnp.sum(jnp.where(hots[k], before, 0.0), axis=-1, keepdims=True)
        route = jnp.where(lane == float(k), ids[k], route)
        route = jnp.where(lane == float(top_k + k), exps[k] / denom, route)
        route = jnp.where(lane == float(2 * top_k + k), rank, route)
    route_ref[...] = route


def _seg_of_block(cfg, i, bm):
    return (i * bm) // cfg.seq


def _norm_mod(cfg, x, g, mod, shift, scale, bm=256):
    m, d = x.shape
    return pl.pallas_call(
        functools.partial(_norm_mod_kernel, shift=shift, scale=scale),
        out_shape=jax.ShapeDtypeStruct((m, d), BF16),
        grid=(m // bm,),
        in_specs=[
            pl.BlockSpec((bm, d), lambda i: (i, 0)),
            pl.BlockSpec((1, d), lambda i: (0, 0)),
            pl.BlockSpec((1, N_MOD, d), lambda i: (_seg_of_block(cfg, i, bm), 0, 0)),
        ],
        out_specs=pl.BlockSpec((bm, d), lambda i: (i, 0)),
        compiler_params=_params(1),
        name="norm_mod",
    )(x, g.reshape(1, d), mod)


def _norm_plain(x, g, m_rows, bm=256):
    d = x.shape[1]
    return pl.pallas_call(
        _norm_plain_kernel,
        out_shape=jax.ShapeDtypeStruct((m_rows, d), F32),
        grid=(m_rows // bm,),
        in_specs=[
            pl.BlockSpec((bm, d), lambda i: (i, 0)),
            pl.BlockSpec((1, d), lambda i: (0, 0)),
        ],
        out_specs=pl.BlockSpec((bm, d), lambda i: (i, 0)),
        compiler_params=_params(1),
        name="final_norm",
    )(x, g.reshape(1, d))


def _norm_router(cfg, x, g, mod, shift, scale, wr_pad, br_pad, bm=256):
    m, d = x.shape
    ep = wr_pad.shape[1]
    return pl.pallas_call(
        functools.partial(_norm_router_kernel, shift=shift, scale=scale, top_k=cfg.top_k),
        out_shape=(jax.ShapeDtypeStruct((m, d // 2), jnp.uint32), jax.ShapeDtypeStruct((m, ep), F32),
                   jax.ShapeDtypeStruct((SUBLANES, ep), F32)),
        grid=(m // bm,),
        in_specs=[
            pl.BlockSpec((bm, d), lambda i: (i, 0)),
            pl.BlockSpec((1, d), lambda i: (0, 0)),
            pl.BlockSpec((1, N_MOD, d), lambda i: (_seg_of_block(cfg, i, bm), 0, 0)),
            pl.BlockSpec((d, ep), lambda i: (0, 0)),
            pl.BlockSpec((1, ep), lambda i: (0, 0)),
        ],
        out_specs=(pl.BlockSpec((bm, d // 2), lambda i: (i, 0)), pl.BlockSpec((bm, ep), lambda i: (i, 0)),
                   pl.BlockSpec((SUBLANES, ep), lambda i: (0, 0))),
        scratch_shapes=[pltpu.VMEM((SUBLANES, ep), F32)],
        compiler_params=_params(1),
        name="norm_router",
    )(x, g.reshape(1, d), mod, wr_pad, br_pad)


CAST_ROWS = 256


def _cast_weight_block(w_ref, wbf_ref):
    @pl.when(pl.program_id(1) == 0)
    def _():
        rows = min(CAST_ROWS, w_ref.shape[0])

        def body(c, carry):
            r0 = pl.multiple_of(c * rows, rows)
            wbf_ref[pl.ds(r0, rows), :] = w_ref[pl.ds(r0, rows), :].astype(BF16)
            return carry

        lax.fori_loop(0, w_ref.shape[0] // rows, body, 0)


def _mm_kernel(x_ref, w_ref, o_ref, wbf_ref):
    _cast_weight_block(w_ref, wbf_ref)
    o_ref[...] = _dot(x_ref[...], wbf_ref[...]).astype(o_ref.dtype)


def _mm_res_kernel(x_ref, w_ref, res_ref, mod_ref, o_ref, wbf_ref, *, gate):
    _cast_weight_block(w_ref, wbf_ref)
    o_ref[...] = res_ref[...] + mod_ref[0, gate:gate + 1, :] * _dot(x_ref[...], wbf_ref[...])


def _matmul(x, w, layer, bm, bn, out_dtype=BF16):
    m, k = x.shape
    n = w.shape[2]
    return pl.pallas_call(
        _mm_kernel,
        out_shape=jax.ShapeDtypeStruct((m, n), out_dtype),
        grid=(n // bn, m // bm),
        in_specs=[
            pl.BlockSpec((bm, k), lambda j, i: (i, 0)),
            pl.BlockSpec((None, k, bn), lambda j, i: (layer, 0, j)),
        ],
        out_specs=pl.BlockSpec((bm, bn), lambda j, i: (i, j)),
        scratch_shapes=[pltpu.VMEM((k, bn), BF16)],
        compiler_params=_params(2),
        name="matmul",
    )(x, w)


def _matmul_residual(cfg, x, w, layer, res, mod, gate, bm, bn):
    m, k = x.shape
    n = w.shape[2]
    return pl.pallas_call(
        functools.partial(_mm_res_kernel, gate=gate),
        out_shape=jax.ShapeDtypeStruct((m, n), F32),
        grid=(n // bn, m // bm),
        in_specs=[
            pl.BlockSpec((bm, k), lambda j, i: (i, 0)),
            pl.BlockSpec((None, k, bn), lambda j, i: (layer, 0, j), pipeline_mode=pl.Buffered(1)),
            pl.BlockSpec((bm, bn), lambda j, i: (i, j)),
            pl.BlockSpec((1, N_MOD, bn), lambda j, i: (_seg_of_block(cfg, i, bm), 0, j)),
        ],
        out_specs=pl.BlockSpec((bm, bn), lambda j, i: (i, j)),
        scratch_shapes=[pltpu.VMEM((k, bn), BF16)],
        compiler_params=_params(2),
        name="matmul_residual",
    )(x, w, res, mod)


def _conv_a_kernel(b_ref, c_ref, h_ref, w_ref, o_ref, *, tt):
    t_len = b_ref.shape[0]
    w = w_ref[...]
    halo = 16

    def prod(r0, n):
        return c_ref[pl.ds(r0, n), :].astype(F32) * h_ref[pl.ds(r0, n), :].astype(F32)

    def chunk(ci, carry):
        r0 = pl.multiple_of(ci * tt, tt)
        u = prod(r0, tt)
        prev = prod(pl.multiple_of(jnp.maximum(r0 - halo, 0), halo), halo)[halo - 1:halo, :]
        prev = jnp.where(r0 == 0, 0.0, prev)
        nxt = prod(pl.multiple_of(jnp.minimum(r0 + tt, t_len - halo), halo), halo)[0:1, :]
        nxt = jnp.where(r0 + tt == t_len, 0.0, nxt)
        row = lax.broadcasted_iota(jnp.int32, u.shape, 0)
        up = jnp.where(row == 0, prev, pltpu.roll(u, 1, 0))
        dn = jnp.where(row == tt - 1, nxt, pltpu.roll(u, tt - 1, 0))
        conv = w[0:1, :] * up + w[1:2, :] * u + w[2:3, :] * dn
        o_ref[pl.ds(r0, tt), :] = (b_ref[pl.ds(r0, tt), :].astype(F32) * conv).astype(o_ref.dtype)
        return carry

    lax.fori_loop(0, t_len // tt, chunk, 0)


def _conv_mixer(cfg, p, w, t_len, row_blk0, n_seq, out_rows, bc=256):
    wd = cfg.branch_w
    ncb = wd // bc
    tt = min(256, t_len)
    spec = lambda off: pl.BlockSpec((t_len, bc), lambda s, j: (row_blk0 + s, off * ncb + j))
    return pl.pallas_call(
        functools.partial(_conv_a_kernel, tt=tt),
        out_shape=jax.ShapeDtypeStruct((out_rows, wd), BF16),
        grid=(n_seq, ncb),
        in_specs=[spec(0), spec(1), spec(2), pl.BlockSpec((3, bc), lambda s, j: (0, j))],
        out_specs=pl.BlockSpec((t_len, bc), lambda s, j: (s, j)),
        compiler_params=_params(2),
        name="conv_mixer",
    )(p, p, p, w)


def _rope_tables(cfg, bt):
    t = jnp.arange(cfg.seq)
    pos = jnp.stack([t // cfg.grid_w, t % cfg.grid_w], axis=-1).astype(F32)
    inv = ROPE_THETA ** (-jnp.arange(ROPE_PAIRS, dtype=F32) * 2.0 / ROPE_AXIS_DIM)
    ang = pos[:, :, None] * inv
    cos, sin = jnp.cos(ang), jnp.sin(ang)
    cmap = jnp.concatenate([cos[:, 0], cos[:, 0], cos[:, 1], cos[:, 1]], axis=-1)
    smap = jnp.concatenate([-sin[:, 0], sin[:, 0], -sin[:, 1], sin[:, 1]], axis=-1)
    ctab = jnp.concatenate([jnp.tile(cmap, (1, 2)), jnp.ones((bt, ATT_VD), F32)], axis=0)
    stab = jnp.concatenate([jnp.tile(smap, (1, 2)), jnp.zeros((bt, ATT_VD), F32)], axis=0)
    return ctab, stab


def _att_prep_kernel(q_ref, k_ref, v_ref, c_ref, s_ref, qt_ref, kr_ref, vt_ref, *, heads):
    cos = c_ref[...]
    sin = s_ref[...]
    lane = lax.broadcasted_iota(jnp.int32, cos.shape, 1)
    low = (lane % ROPE_AXIS_DIM) < ROPE_PAIRS

    def rope(u):
        swapped = jnp.where(low, pltpu.roll(u, ATT_VD - ROPE_PAIRS, 1), pltpu.roll(u, ROPE_PAIRS, 1))
        return u * cos + swapped * sin

    for h in range(heads):
        sl = slice(h * ATT_VD, (h + 1) * ATT_VD)
        qt_ref[sl, :] = (rope(q_ref[:, sl].astype(F32)) * ATT_Q_SCALE).T.astype(qt_ref.dtype)
        kr_ref[:, sl] = rope(k_ref[:, sl].astype(F32)).astype(kr_ref.dtype)
        vt_ref[sl, :] = v_ref[:, sl].astype(F32).T.astype(vt_ref.dtype)


def _att_prep(cfg, p, ctab, stab, bt=256):
    m, wd = cfg.m_all, cfg.branch_w
    n_lat_blk = cfg.m_lat // bt
    n_seq_blk = cfg.seq // bt
    tab_map = lambda i: (jnp.where(i < n_lat_blk, i % n_seq_blk, n_seq_blk), 0)
    return pl.pallas_call(
        functools.partial(_att_prep_kernel, heads=cfg.heads),
        out_shape=(jax.ShapeDtypeStruct((wd, m), BF16), jax.ShapeDtypeStruct((m, wd), BF16),
                   jax.ShapeDtypeStruct((wd, m), BF16)),
        grid=(m // bt,),
        in_specs=[
            pl.BlockSpec((bt, wd), lambda i: (i, 3)),
            pl.BlockSpec((bt, wd), lambda i: (i, 4)),
            pl.BlockSpec((bt, wd), lambda i: (i, 5)),
            pl.BlockSpec((bt, ATT_VD), tab_map),
            pl.BlockSpec((bt, ATT_VD), tab_map),
        ],
        out_specs=(pl.BlockSpec((wd, bt), lambda i: (0, i)), pl.BlockSpec((bt, wd), lambda i: (i, 0)),
                   pl.BlockSpec((wd, bt), lambda i: (0, i))),
        compiler_params=_params(1),
        name="att_prep",
    )(p, p, p, ctab, stab)


def _attn_kernel(lam_ref, g_ref, qt_ref, *refs, seg_lens, tk, lam_init):
    n_seg = len(seg_lens)
    kv = refs[:2 * n_seg]
    o_ref = refs[2 * n_seg]
    s1_ref, s2_ref, acc1_ref, acc2_ref = refs[2 * n_seg + 1:]
    bq = qt_ref.shape[1]

    qt = qt_ref[...].astype(F32)
    sub = lax.broadcasted_iota(jnp.int32, qt.shape, 0)
    q1 = jnp.where(sub < ATT_HD, qt, 0.0).astype(qt_ref.dtype)
    q2 = jnp.where(sub >= ATT_HD, qt, 0.0).astype(qt_ref.dtype)

    def chunks(seg):
        tkk = min(tk, seg_lens[seg])
        return tkk, seg_lens[seg] // tkk

    def pass_a(q, s_ref):
        m = jnp.full((1, bq), -jnp.inf, F32)
        off = 0
        for seg in range(n_seg):
            k_ref = kv[2 * seg]
            tkk, n_chunk = chunks(seg)

            def body(c, m, k_ref=k_ref, tkk=tkk, off=off):
                r0 = pl.multiple_of(c * tkk, tkk)
                s = _dot(k_ref[pl.ds(r0, tkk), :], q)
                s_ref[pl.ds(off + r0, tkk), :] = s
                return jnp.maximum(m, jnp.max(s, axis=0, keepdims=True))

            m = lax.fori_loop(0, n_chunk, body, m, unroll=min(ATT_UNROLL, n_chunk))
            off += seg_lens[seg]
        return m

    def pass_b(s_ref, m, acc_ref):
        acc_ref[...] = jnp.zeros_like(acc_ref)
        l = jnp.zeros((1, bq), F32)
        off = 0
        for seg in range(n_seg):
            vt_ref = kv[2 * seg + 1]
            tkk, n_chunk = chunks(seg)

            def body(c, l, vt_ref=vt_ref, tkk=tkk, off=off):
                r0 = pl.multiple_of(c * tkk, tkk)
                p = jnp.exp2(s_ref[pl.ds(off + r0, tkk), :] - m)
                vt = vt_ref[:, pl.ds(r0, tkk)]
                acc_ref[...] += _dot(vt, p.astype(vt.dtype))
                return l + jnp.sum(p, axis=0, keepdims=True)

            l = lax.fori_loop(0, n_chunk, body, l, unroll=min(ATT_UNROLL, n_chunk))
            off += seg_lens[seg]
        return l

    m1 = pass_a(q1, s1_ref)
    acc1_ref[...] = jnp.zeros_like(acc1_ref)
    l1 = jnp.zeros((1, bq), F32)
    m2 = jnp.full((1, bq), -jnp.inf, F32)
    off = 0
    for seg in range(n_seg):
        k_ref, vt_ref = kv[2 * seg], kv[2 * seg + 1]
        tkk, n_chunk = chunks(seg)
        for c in range(n_chunk):
            r0 = c * tkk
            s = _dot(k_ref[pl.ds(r0, tkk), :], q2)
            s2_ref[pl.ds(off + r0, tkk), :] = s
            m2 = jnp.maximum(m2, jnp.max(s, axis=0, keepdims=True))
            p = jnp.exp2(s1_ref[pl.ds(off + r0, tkk), :] - m1)
            vt = vt_ref[:, pl.ds(r0, tkk)]
            acc1_ref[...] += _dot(vt, p.astype(vt.dtype))
            l1 = l1 + jnp.sum(p, axis=0, keepdims=True)
        off += seg_lens[seg]
    l2 = pass_b(s2_ref, m2, acc2_ref)

    lq = lam_ref[...]
    lam = (jnp.exp(jnp.sum(lq[0:1] * lq[1:2], axis=-1, keepdims=True))
           - jnp.exp(jnp.sum(lq[2:3] * lq[3:4], axis=-1, keepdims=True)) + lam_init)
    ot = acc1_ref[...] * (1.0 / l1) - acc2_ref[...] * (lam / l2)
    ms = jnp.mean(ot * ot, axis=0, keepdims=True)
    y = ot * lax.rsqrt(ms + EPS) * g_ref[...] * (1.0 - lam_init)
    o_ref[...] = y.T.astype(o_ref.dtype)


def _attention(cfg, lam_qk, subln_g, qt, segs, q_col_blk0, n_q_per_batch, out_rows, lam_init, bq=256, tk=512):
    wd = cfg.branch_w
    in_specs = [
        pl.BlockSpec((4, ATT_HD), lambda b, h, i: (0, 0)),
        pl.BlockSpec((ATT_VD, 1), lambda b, h, i: (0, 0)),
        pl.BlockSpec((ATT_VD, bq), lambda b, h, i: (h, q_col_blk0 + b * n_q_per_batch + i)),
    ]
    args = [lam_qk, subln_g.reshape(ATT_VD, 1), qt]
    seg_lens = []
    for k_arr, k_blk, k_col0, vt_arr, length in segs:
        in_specs.append(pl.BlockSpec((length, ATT_VD), lambda b, h, i, f=k_blk, c0=k_col0: (f(b), c0 + h)))
        in_specs.append(pl.BlockSpec((ATT_VD, length), lambda b, h, i, f=k_blk: (h, f(b))))
        args += [k_arr, vt_arr]
        seg_lens.append(length)
    l_tot = sum(seg_lens)
    return pl.pallas_call(
        functools.partial(_attn_kernel, seg_lens=tuple(seg_lens), tk=tk, lam_init=lam_init),
        out_shape=jax.ShapeDtypeStruct((out_rows, wd), BF16),
        grid=(cfg.batch, cfg.heads, n_q_per_batch),
        in_specs=in_specs,
        out_specs=pl.BlockSpec((bq, ATT_VD), lambda b, h, i: (b * n_q_per_batch + i, h)),
        scratch_shapes=[pltpu.VMEM((l_tot, bq), F32), pltpu.VMEM((l_tot, bq), F32),
                        pltpu.VMEM((ATT_VD, bq), F32), pltpu.VMEM((ATT_VD, bq), F32)],
        compiler_params=_params(3),
        name="diff_attention",
    )(*args)


SCAN_PAD = SUBLANES


def _scan_levels(n):
    plan = []
    while n > 1:
        fan = SUBLANES if n % SUBLANES == 0 else n
        assert fan <= SUBLANES, "sequence length must factor into groups of at most 8"
        plan.append((n, fan, n // fan))
        n //= fan
    return plan


def _scan_scratch_rows(n):
    rows = 0
    for _, _, parts in _scan_levels(n):
        rows += 2 * SCAN_PAD + -(-parts // SUBLANES) * SUBLANES
    return max(rows, SUBLANES)


def _scan_inplace(a_ref, b_ref, base, n, rev, sa_ref, sb_ref, off=0):
    if n == 1:
        return
    fan = SUBLANES if n % SUBLANES == 0 else n
    parts = n // fan
    order = list(range(fan))
    if rev:
        order.reverse()

    def rows(r):
        return pl.ds(base + r, parts, stride=fan) if parts > 1 else pl.ds(base + r, 1)

    acc_a = acc_b = None
    for r in order:
        a_r = a_ref[rows(r), :]
        b_r = b_ref[rows(r), :]
        if acc_a is None:
            acc_a, acc_b = a_r, b_r
        else:
            acc_b = a_r * acc_b + b_r
            acc_a = a_r * acc_a
            a_ref[rows(r), :] = acc_a
            b_ref[rows(r), :] = acc_b
    if parts == 1:
        return
    data = off + SCAN_PAD
    ones = jnp.ones((SCAN_PAD, LANES), F32)
    zeros = jnp.zeros((SCAN_PAD, LANES), F32)
    sa_ref[pl.ds(off, SCAN_PAD), :] = ones
    sb_ref[pl.ds(off, SCAN_PAD), :] = zeros
    sa_ref[pl.ds(data + parts, SCAN_PAD), :] = ones
    sb_ref[pl.ds(data + parts, SCAN_PAD), :] = zeros
    sa_ref[pl.ds(data, parts), :] = acc_a
    sb_ref[pl.ds(data, parts), :] = acc_b
    nxt = off + 2 * SCAN_PAD + -(-parts // SUBLANES) * SUBLANES
    _scan_inplace(sa_ref, sb_ref, data, parts, rev, sa_ref, sb_ref, nxt)
    shift = data + 1 if rev else data - 1
    carry_a = sa_ref[pl.ds(shift, parts), :]
    carry_b = sb_ref[pl.ds(shift, parts), :]
    for r in order:
        a_r = a_ref[rows(r), :]
        b_ref[rows(r), :] = b_ref[rows(r), :] + a_r * carry_b
        a_ref[rows(r), :] = a_r * carry_a


def _softplus(x):
    return jnp.maximum(x, 0.0) + jnp.log1p(jnp.exp(-jnp.abs(x)))


def _gelu_tanh(x):
    return 0.5 * x * (1.0 + jnp.tanh(math.sqrt(2.0 / math.pi) * (x + 0.044715 * (x * x * x))))


def _lru_kernel(lxl_ref, lyl_ref, lxc_ref, lyc_ref, cw_ref, cb_ref, wa_ref, ba_ref, wx_ref, bx_ref, lam_ref,
                ol_ref, oc_ref, pad_ref, a0_ref, b0_ref, a1_ref, b1_ref, sa_ref, sb_ref, *, tt):
    l_ctx = lxc_ref.shape[0]
    l_lat = lxl_ref.shape[0]
    cw = cw_ref[...]
    cb = cb_ref[...]
    ab_refs = ((a0_ref, b0_ref), (a1_ref, b1_ref))
    neg_sp = [-LRU_C * _softplus(-lam_ref[d:d + 1, :]) for d in range(2)]
    wa = [wa_ref[d].astype(BF16) for d in range(2)]
    wx = [wx_ref[d].astype(BF16) for d in range(2)]

    for lx_ref, base, length, is_ctx in ((lxc_ref, 0, l_ctx, True), (lxl_ref, l_ctx, l_lat, False)):
        ttk = min(tt, length)
        pad_ref[pl.ds(0, SUBLANES), :] = jnp.zeros((SUBLANES, LANES), F32)
        pad_ref[pl.ds(SUBLANES + length, SUBLANES), :] = jnp.zeros((SUBLANES, LANES), F32)

        def fill(ci, carry, lx_ref=lx_ref, ttk=ttk):
            r0 = pl.multiple_of(ci * ttk, ttk)
            pad_ref[pl.ds(SUBLANES + r0, ttk), :] = lx_ref[pl.ds(r0, ttk), :].astype(F32)
            return carry

        lax.fori_loop(0, length // ttk, fill, 0)

        def gates(ci, carry, base=base, length=length, is_ctx=is_ctx, ttk=ttk):
            r0 = pl.multiple_of(ci * ttk, ttk)
            xx = pad_ref[pl.ds(r0, ttk + 2 * SUBLANES), :]
            u = (cw[0:1, :] * xx[6:6 + ttk] + cw[1:2, :] * xx[7:7 + ttk]
                 + cw[2:3, :] * xx[8:8 + ttk] + cw[3:4, :] * xx[9:9 + ttk]) + cb
            ub = u.astype(BF16)
            row = r0 + lax.broadcasted_iota(jnp.int32, u.shape, 0)
            for d in range(2):
                r_gate = _sigmoid(_dot(ub, wa[d]) + ba_ref[d:d + 1, :])
                i_gate = _sigmoid(_dot(ub, wx[d]) + bx_ref[d:d + 1, :])
                log_a = neg_sp[d] * r_gate
                a = jnp.exp(log_a)
                mult = jnp.sqrt(-jnp.tanh(log_a) * (a * a + 1.0))
                if is_ctx:
                    mult = jnp.where(row == (length - 1 if d == 1 else 0), 1.0, mult)
                a_ref, b_ref = ab_refs[d]
                a_ref[pl.ds(base + r0, ttk), :] = a
                b_ref[pl.ds(base + r0, ttk), :] = mult * (i_gate * u)
            return carry

        lax.fori_loop(0, length // ttk, gates, 0, unroll=min(2, length // ttk))

    for d, rev in ((0, False), (1, True)):
        a_ref, b_ref = ab_refs[d]
        _scan_inplace(a_ref, b_ref, 0, l_ctx, rev, sa_ref, sb_ref)
        _scan_inplace(a_ref, b_ref, l_ctx, l_lat, rev, sa_ref, sb_ref)

    h0 = (b0_ref[pl.ds(l_ctx - 1, 1), :], b1_ref[pl.ds(0, 1), :])

    for ly_ref, o_ref, base, length, is_ctx in ((lyc_ref, oc_ref, 0, l_ctx, True), (lyl_ref, ol_ref, l_ctx, l_lat, False)):
        ttk = min(tt, length)

        def emit(ci, carry, ly_ref=ly_ref, o_ref=o_ref, base=base, is_ctx=is_ctx, ttk=ttk):
            r0 = pl.multiple_of(ci * ttk, ttk)
            h = b0_ref[pl.ds(base + r0, ttk), :] + b1_ref[pl.ds(base + r0, ttk), :]
            if not is_ctx:
                h = h + a0_ref[pl.ds(base + r0, ttk), :] * h0[0] + a1_ref[pl.ds(base + r0, ttk), :] * h0[1]
            o_ref[pl.ds(r0, ttk), :] = (h * _gelu_tanh(ly_ref[pl.ds(r0, ttk), :].astype(F32))).astype(o_ref.dtype)
            return carry

        lax.fori_loop(0, length // ttk, emit, 0)


def _lru(cfg, p, conv_w, conv_b, w_a, b_a, w_x, b_x, lam_p, tt=256):
    wd, s, lc = cfg.branch_w, cfg.seq, cfg.ctx_len
    nb = wd // LANES
    l_tot = s + lc
    ctx_blk0 = cfg.m_lat // lc
    lx_col, ly_col = 6 * nb, 7 * nb
    vec2 = pl.BlockSpec((2, LANES), lambda b, n: (0, n))
    wspec = pl.BlockSpec((2, None, LANES, LANES), lambda b, n: (0, n, 0, 0))
    scan_rows = max(_scan_scratch_rows(s), _scan_scratch_rows(lc))
    return pl.pallas_call(
        functools.partial(_lru_kernel, tt=tt),
        out_shape=(jax.ShapeDtypeStruct((cfg.m_lat, wd), BF16), jax.ShapeDtypeStruct((cfg.batch * lc, wd), BF16)),
        grid=(cfg.batch, nb),
        in_specs=[
            pl.BlockSpec((s, LANES), lambda b, n: (b, lx_col + n)),
            pl.BlockSpec((s, LANES), lambda b, n: (b, ly_col + n)),
            pl.BlockSpec((lc, LANES), lambda b, n: (ctx_blk0 + b, lx_col + n)),
            pl.BlockSpec((lc, LANES), lambda b, n: (ctx_blk0 + b, ly_col + n)),
            pl.BlockSpec((4, LANES), lambda b, n: (0, n)),
            pl.BlockSpec((1, LANES), lambda b, n: (0, n)),
            wspec, vec2, wspec, vec2, vec2,
        ],
        out_specs=(pl.BlockSpec((s, LANES), lambda b, n: (b, n)), pl.BlockSpec((lc, LANES), lambda b, n: (b, n))),
        scratch_shapes=[pltpu.VMEM((s + 2 * SUBLANES, LANES), F32)]
        + [pltpu.VMEM((l_tot, LANES), F32)] * 4
        + [pltpu.VMEM((scan_rows, LANES), F32)] * 2,
        compiler_params=_params(2),
        name="rglru",
    )(p, p, p, p, conv_w, conv_b.reshape(1, wd), w_a, b_a, w_x, b_x, lam_p)


def _merge_kernel(*refs, wd, n_lat_blk):
    branch_refs, (ga_ref, gb_ref, gc_ref, w_ref, o_ref, wbf_ref) = refs[:6], refs[6:]
    _cast_weight_block(w_ref, wbf_ref)
    is_lat = pl.program_id(1) < n_lat_blk
    acc = None
    for i, g_ref in enumerate((ga_ref, gb_ref, gc_ref)):
        x = jnp.where(is_lat, branch_refs[2 * i][...], branch_refs[2 * i + 1][...])
        g = _sigmoid(g_ref[...].astype(F32))
        term = g * _dot(x, wbf_ref[i * wd:(i + 1) * wd, :])
        acc = term if acc is None else acc + term
    o_ref[...] = acc.astype(o_ref.dtype)


def _merge(cfg, branches, p, wb, layer, bm, bn):
    m, wd, d = cfg.m_all, cfg.branch_w, cfg.d_model
    g_blk0 = 8 * wd // bn
    nd = d // bn
    n_lat_blk = cfg.m_lat // bm
    lat_spec = pl.BlockSpec((bm, wd), lambda j, i: (jnp.minimum(i, n_lat_blk - 1), 0))
    ctx_spec = pl.BlockSpec((bm, wd), lambda j, i: (jnp.maximum(i - n_lat_blk, 0), 0))
    gspec = lambda br: pl.BlockSpec((bm, bn), lambda j, i: (i, g_blk0 + br * nd + j))
    args = [a for pair in branches for a in pair]
    return pl.pallas_call(
        functools.partial(_merge_kernel, wd=wd, n_lat_blk=n_lat_blk),
        out_shape=jax.ShapeDtypeStruct((m, d), BF16),
        grid=(nd, m // bm),
        in_specs=[lat_spec, ctx_spec] * 3 + [gspec(0), gspec(1), gspec(2),
                  pl.BlockSpec((None, 3 * wd, bn), lambda j, i: (layer, 0, j))],
        out_specs=pl.BlockSpec((bm, bn), lambda j, i: (i, j)),
        scratch_shapes=[pltpu.VMEM((3 * wd, bn), BF16)],
        compiler_params=_params(2),
        name="merge_branches",
    )(*args, p, p, p, wb)


MOE_TILE = 256
MOE_BLOCK = 256


def _moe_slots(cfg):
    n_tiles = cfg.m_all * cfg.top_k // MOE_TILE + cfg.n_experts
    return n_tiles, n_tiles * MOE_TILE


def _assignment_index(t, top_k):
    per_row = LANES // top_k
    assert per_row * top_k == LANES and per_row & (per_row - 1) == 0
    shift = per_row.bit_length() - 1
    return lax.shift_right_logical(t, shift), (t & (per_row - 1)) * top_k


def _dispatch_kernel(fill_ref, nchunk_ref, nv_ref, pos_ref, xq_ref, xs_hbm, zero_ref, pos_smem, sem, idx_sem,
                     *, n_experts, top_k, n_tiles):
    i = pl.program_id(0)
    n_assign = pos_ref.shape[0] * pos_ref.shape[1]
    bt = n_assign // top_k

    @pl.when(i == 0)
    def _():
        zero_ref[...] = jnp.zeros_like(zero_ref)
        piece = zero_ref.at[pl.ds(0, SUBLANES)]
        total = 0
        for e in range(n_experts):
            def fill_piece(c, carry, e=e):
                start = pl.multiple_of(fill_ref[e] + c * SUBLANES, SUBLANES)
                pltpu.make_async_copy(piece, xs_hbm.at[pl.ds(start, SUBLANES)], sem).start()
                return carry

            lax.fori_loop(0, nchunk_ref[e], fill_piece, 0)
            total = total + nchunk_ref[e]

        def idle_tile(j, carry):
            start = pl.multiple_of(j * MOE_TILE, MOE_TILE)
            pltpu.make_async_copy(zero_ref, xs_hbm.at[pl.ds(start, MOE_TILE)], sem).start()
            return carry

        lax.fori_loop(nv_ref[0], n_tiles, idle_tile, 0)

        def drain_piece(c, carry):
            pltpu.make_async_copy(piece, xs_hbm.at[pl.ds(0, SUBLANES)], sem).wait()
            return carry

        lax.fori_loop(0, total, drain_piece, 0)

        def drain_tile(j, carry):
            pltpu.make_async_copy(zero_ref, xs_hbm.at[pl.ds(0, MOE_TILE)], sem).wait()
            return carry

        lax.fori_loop(nv_ref[0], n_tiles, drain_tile, 0)

    idx_copy = pltpu.make_async_copy(pos_ref, pos_smem, idx_sem)
    idx_copy.start()
    idx_copy.wait()

    def issue(t, carry):
        row, col = _assignment_index(t, top_k)
        for k in range(top_k):
            dst = pos_smem[row, col + k]
            pltpu.make_async_copy(xq_ref.at[pl.ds(t, 1)], xs_hbm.at[pl.ds(dst, 1)], sem).start(priority=k % 2)
        return carry

    lax.fori_loop(0, bt, issue, 0, unroll=2)
    pltpu.make_async_copy(xs_hbm.at[pl.ds(0, n_assign)], xs_hbm.at[pl.ds(0, n_assign)], sem).wait()


def _dispatch(cfg, fill, nchunk, n_valid, pos2d, xq):
    m, half = xq.shape
    n_tiles, slot_rows = _moe_slots(cfg)
    bt = 2 * MOE_BLOCK if m % (2 * MOE_BLOCK) == 0 else MOE_BLOCK
    rows_per_step = bt * cfg.top_k // LANES
    return pl.pallas_call(
        functools.partial(_dispatch_kernel, n_experts=cfg.n_experts, top_k=cfg.top_k, n_tiles=n_tiles),
        out_shape=jax.ShapeDtypeStruct((slot_rows, half), jnp.uint32),
        grid_spec=pltpu.PrefetchScalarGridSpec(
            num_scalar_prefetch=3,
            grid=(m // bt,),
            in_specs=[pl.BlockSpec((rows_per_step, LANES), lambda i, *_: (i, 0)),
                      pl.BlockSpec((bt, half), lambda i, *_: (i, 0))],
            out_specs=pl.BlockSpec(memory_space=pl.ANY),
            scratch_shapes=[pltpu.VMEM((MOE_TILE, half), jnp.uint32),
                            pltpu.SMEM((rows_per_step, LANES), jnp.int32),
                            pltpu.SemaphoreType.DMA(()), pltpu.SemaphoreType.DMA(())]),
        compiler_params=_params(1),
        name="moe_dispatch",
    )(fill, nchunk, n_valid, pos2d, xq)


def _expert_kernel(te_ref, nv_ref, xs_ref, perm_ref, wgu_ref, bgu_ref, wdn_ref, bdn_ref, ys_ref,
                   wgu_bf, wdn_bf, *, ff):
    j = pl.program_id(0)
    d = wgu_ref.shape[0]
    half = d // 2
    e = te_ref[j]
    e_prev = te_ref[jnp.maximum(j - 1, 0)]

    @pl.when((j == 0) | (e != e_prev))
    def _():
        rows = min(512, d)

        def cast_gu(c, carry):
            r0 = pl.multiple_of(c * rows, rows)
            w = wgu_ref[pl.ds(r0, rows), :].astype(BF16)
            wgu_bf[pl.ds(r0, rows), :] = _dot(w, perm_ref[...]).astype(BF16)
            return carry

        lax.fori_loop(0, d // rows, cast_gu, 0, unroll=min(4, d // rows))
        rows_dn = 32

        def cast_dn(c, carry):
            r0 = pl.multiple_of(c * rows_dn, rows_dn)
            wdn_bf[pl.ds(r0, rows_dn), :] = wdn_ref[pl.ds(r0, rows_dn), :].astype(BF16)
            return carry

        lax.fori_loop(0, ff // rows_dn, cast_dn, 0)

    @pl.when(j < nv_ref[0])
    def _():
        lo, hi = _unpack_pair(xs_ref[...])
        h = (_dot(lo.astype(BF16), wgu_bf[pl.ds(0, half), :]) + _dot(hi.astype(BF16), wgu_bf[pl.ds(half, half), :])
             + bgu_ref[...])
        h_glu = jnp.minimum(h[:, :ff], SWIGLU_LIMIT)
        h_lin = jnp.clip(h[:, ff:], -SWIGLU_LIMIT, SWIGLU_LIMIT)
        act = (h_glu * _sigmoid(SWIGLU_ALPHA * h_glu) * (h_lin + 1.0)).astype(BF16)
        cw = min(512, half)
        for c in range(half // cw):
            lo_sl = slice(c * cw, (c + 1) * cw)
            hi_sl = slice(half + c * cw, half + (c + 1) * cw)
            y_lo = _dot(act, wdn_bf[:, lo_sl]) + bdn_ref[:, lo_sl]
            y_hi = _dot(act, wdn_bf[:, hi_sl]) + bdn_ref[:, hi_sl]
            ys_ref[:, lo_sl] = _pack_pair(y_lo, y_hi)

    @pl.when(j >= nv_ref[0])
    def _():
        ys_ref[...] = jnp.zeros_like(ys_ref)


def _experts(cfg, tile_e, n_valid, xs, perm, w_gu, b_gu_perm, w_dn, b_dn, layer):
    d, ff, ne = cfg.d_model, cfg.expert_ff, cfg.n_experts
    half = d // 2
    n_tiles, slot_rows = _moe_slots(cfg)
    tile_blk = lambda j, te, nv: (jnp.minimum(j, nv[0] - 1), 0)
    by_expert = lambda j, te, nv: (te[j], 0, 0)
    by_layer_expert = lambda j, te, nv: (layer, te[j], 0, 0)
    return pl.pallas_call(
        functools.partial(_expert_kernel, ff=ff),
        out_shape=jax.ShapeDtypeStruct((slot_rows, half), jnp.uint32),
        grid_spec=pltpu.PrefetchScalarGridSpec(
            num_scalar_prefetch=2,
            grid=(n_tiles,),
            in_specs=[
                pl.BlockSpec((MOE_TILE, half), tile_blk),
                pl.BlockSpec((2 * ff, 2 * ff), lambda j, te, nv: (0, 0)),
                pl.BlockSpec((None, None, d, 2 * ff), by_layer_expert),
                pl.BlockSpec((None, 1, 2 * ff), by_expert),
                pl.BlockSpec((None, None, ff, d), by_layer_expert),
                pl.BlockSpec((None, 1, d), by_expert),
            ],
            out_specs=pl.BlockSpec((MOE_TILE, half), lambda j, te, nv: (j, 0)),
            scratch_shapes=[pltpu.VMEM((d, 2 * ff), BF16), pltpu.VMEM((ff, d), BF16)]),
        compiler_params=_params(1),
        name="moe_experts",
    )(tile_e, n_valid, xs, perm, w_gu, b_gu_perm.reshape(ne, 1, 2 * ff), w_dn, b_dn.reshape(ne, 1, d))


def _combine_kernel(pos_ref, pos_next_ref, route_ref, res_ref, mod_ref, ys_hbm, o_ref, buf_ref, pos_smem,
                    sems, idx_sem, *, gate, top_k):
    i = pl.program_id(0)
    n = pl.num_programs(0)
    bt, d = res_ref.shape
    half = d // 2
    slot = lax.rem(i, 2)
    other = 1 - slot

    def load_positions(src_ref):
        idx_copy = pltpu.make_async_copy(src_ref, pos_smem, idx_sem)
        idx_copy.start()
        idx_copy.wait()

    def gather_token(t, to_slot):
        row, col = _assignment_index(t, top_k)
        for k in range(top_k):
            src = pos_smem[row, col + k]
            pltpu.make_async_copy(ys_hbm.at[pl.ds(src, 1)], buf_ref.at[to_slot, k, pl.ds(t, 1)],
                                  sems.at[to_slot]).start()

    def wait_slot(s):
        pltpu.make_async_copy(buf_ref.at[s], buf_ref.at[s], sems.at[s]).wait()

    @pl.when(i == 0)
    def _():
        load_positions(pos_ref)

        def first(t, carry):
            gather_token(t, 0)
            return carry

        lax.fori_loop(0, bt, first, 0, unroll=2)

    wait_slot(slot)
    load_positions(pos_next_ref)

    rb = 2 * SUBLANES
    cw = min(512, half)

    def rows(r, carry):
        r0 = pl.multiple_of(r * rb, rb)
        rs = pl.ds(r0, rb)
        for u in range(rb):
            gather_token(r0 + u, other)
        weights = [route_ref[rs, top_k + k:top_k + k + 1] for k in range(top_k)]
        for c in range(half // cw):
            lo_sl = slice(c * cw, (c + 1) * cw)
            hi_sl = slice(half + c * cw, half + (c + 1) * cw)
            acc_lo = acc_hi = None
            for k in range(top_k):
                lo, hi = _unpack_pair(buf_ref[slot, k, rs, lo_sl])
                acc_lo = weights[k] * lo if acc_lo is None else acc_lo + weights[k] * lo
                acc_hi = weights[k] * hi if acc_hi is None else acc_hi + weights[k] * hi
            o_ref[rs, lo_sl] = res_ref[rs, lo_sl] + mod_ref[0, gate:gate + 1, lo_sl] * acc_lo
            o_ref[rs, hi_sl] = res_ref[rs, hi_sl] + mod_ref[0, gate:gate + 1, hi_sl] * acc_hi
        return carry

    lax.fori_loop(0, bt // rb, rows, 0)

    @pl.when(i == n - 1)
    def _():
        wait_slot(other)


def _combine(cfg, pos2d, route, res, mod, ys, gate):
    m, d = res.shape
    half = d // 2
    ep = route.shape[1]
    bt = MOE_BLOCK
    rows_per_step = bt * cfg.top_k // LANES
    return pl.pallas_call(
        functools.partial(_combine_kernel, gate=gate, top_k=cfg.top_k),
        out_shape=jax.ShapeDtypeStruct((m, d), F32),
        grid=(m // bt,),
        in_specs=[
            pl.BlockSpec((rows_per_step, LANES), lambda i: (i, 0)),
            pl.BlockSpec((rows_per_step, LANES), lambda i: (jnp.minimum(i + 1, m // bt - 1), 0)),
            pl.BlockSpec((bt, ep), lambda i: (i, 0)),
            pl.BlockSpec((bt, d), lambda i: (i, 0)),
            pl.BlockSpec((1, N_MOD, d), lambda i: (_seg_of_block(cfg, i, bt), 0, 0)),
            pl.BlockSpec(memory_space=pl.ANY),
        ],
        out_specs=pl.BlockSpec((bt, d), lambda i: (i, 0)),
        scratch_shapes=[pltpu.VMEM((2, cfg.top_k, bt, half), jnp.uint32),
                        pltpu.SMEM((rows_per_step, LANES), jnp.int32),
                        pltpu.SemaphoreType.DMA((2,)), pltpu.SemaphoreType.DMA(())],
        compiler_params=_params(1),
        name="moe_combine",
    )(pos2d, pos2d, route, res, mod, ys)


def _moe_plan(cfg, route, counts8):
    k, ne = cfg.top_k, cfg.n_experts
    n_tiles, _ = _moe_slots(cfg)
    expert = route[:, :k].astype(jnp.int32)
    rank = route[:, 2 * k:3 * k].astype(jnp.int32)
    counts = counts8[0, :ne].astype(jnp.int32)
    padded = (counts + MOE_TILE - 1) // MOE_TILE * MOE_TILE
    region_end = jnp.cumsum(padded)
    region_start = region_end - padded
    pos = region_start[expert] + rank
    fill = (region_start + counts) // SUBLANES * SUBLANES
    nchunk = (region_end - fill) // SUBLANES
    n_valid = region_end[-1] // MOE_TILE
    tile_start = jnp.arange(n_tiles, dtype=jnp.int32) * MOE_TILE
    tile_e = jnp.sum(region_end[None, :] <= tile_start[:, None], axis=1)
    tile_e = jnp.minimum(tile_e, ne - 1).astype(jnp.int32)
    tile_e = tile_e[jnp.minimum(jnp.arange(n_tiles), n_valid - 1)]
    return (pos.reshape(-1, LANES), fill.astype(jnp.int32), nchunk.astype(jnp.int32), tile_e,
            n_valid.reshape(1).astype(jnp.int32))


def _deinterleave_perm(ff):
    col = jnp.arange(2 * ff)
    src = jnp.where(col < ff, 2 * col, 2 * (col - ff) + 1)
    return (jnp.arange(2 * ff)[:, None] == src[None, :]).astype(BF16)


def _forward(cfg, x, c, ctx, c_ctx, ada_a, ada_b, ada_bias, norm1_g, norm2_g, w_in, conv_a_w, lam_qk,
             subln_g, lru_conv_w, lru_conv_b, lru_w_a, lru_b_a, lru_w_x, lru_b_x, lru_lambda,
             w_branch, w_out, router_w, router_b, w_gu, b_gu, w_dn, b_dn, final_g):
    d, wd, bsz, s, lc = cfg.d_model, cfg.branch_w, cfg.batch, cfg.seq, cfg.ctx_len
    m_lat, m_all, ne = cfg.m_lat, cfg.m_all, cfg.n_experts
    assert bsz * lc <= s and s % lc == 0 and bsz + 1 <= SUBLANES
    bm = min(512, bsz * lc)
    bn = min(1024, d)
    bt = min(256, lc)
    ep = -(-ne // LANES) * LANES

    xs = jnp.concatenate([x.reshape(m_lat, d), ctx.reshape(bsz * lc, d)], axis=0)
    cond8 = jnp.zeros((SUBLANES, d), F32).at[:bsz].set(c).at[bsz].set(c_ctx)
    ctab, stab = _rope_tables(cfg, bt)
    ctx_blk = lambda b: m_lat // lc + b
    perm = _deinterleave_perm(cfg.expert_ff)

    for l in range(cfg.depth):
        lam_init = LAMBDA_INIT_BASE - LAMBDA_INIT_AMP * math.exp(-LAMBDA_INIT_RATE * l)
        mod = _ada(cfg, cond8, ada_a[l], ada_b[l], ada_bias[l])

        xn = _norm_mod(cfg, xs, norm1_g[l], mod, 0, 1)
        p = _matmul(xn, w_in, l, bm, bn)
        oa = (_conv_mixer(cfg, p, conv_a_w[l], s, 0, bsz, m_lat),
              _conv_mixer(cfg, p, conv_a_w[l], lc, m_lat // lc, bsz, bsz * lc))
        qt, kr, vt = _att_prep(cfg, p, ctab, stab, bt)
        nh = wd // ATT_VD
        ctx_seg = (p, ctx_blk, 4 * nh, vt, lc)
        lat_seg = (kr, lambda b: b, 0, vt, s)
        bq_lat = min(ATT_LATENT_BQ, s)
        ob_lat = _attention(cfg, lam_qk[l], subln_g[l], qt, [ctx_seg, lat_seg], 0, s // bq_lat, m_lat, lam_init,
                            bq=bq_lat)
        ob_ctx = _attention(cfg, lam_qk[l], subln_g[l], qt, [ctx_seg], m_lat // bt, lc // bt, bsz * lc, lam_init, bq=bt)
        oc = _lru(cfg, p, lru_conv_w[l], lru_conv_b[l], lru_w_a[l], lru_b_a[l],
                  lru_w_x[l], lru_b_x[l], lru_lambda[l])
        y = _merge(cfg, (oa, (ob_lat, ob_ctx), oc), p, w_branch, l, bm, bn)
        xs = _matmul_residual(cfg, y, w_out, l, xs, mod, 2, bm, bn)

        wr_pad = jnp.zeros((d, ep), F32).at[:, :ne].set(router_w[l])
        br_pad = jnp.full((1, ep), -jnp.inf, F32).at[0, :ne].set(router_b[l])
        xq, route, counts8 = _norm_router(cfg, xs, norm2_g[l], mod, 3, 4, wr_pad, br_pad)
        pos2d, fill, nchunk, tile_e, n_valid = _moe_plan(cfg, route, counts8)
        slots = _dispatch(cfg, fill, nchunk, n_valid, pos2d, xq)
        b_gu_perm = jnp.concatenate([b_gu[l][:, 0::2], b_gu[l][:, 1::2]], axis=-1)
        ys = _experts(cfg, tile_e, n_valid, slots, perm, w_gu, b_gu_perm, w_dn, b_dn[l], l)
        xs = _combine(cfg, pos2d, route, xs, mod, ys, 5)

    return _norm_plain(xs, final_g, m_lat).reshape(bsz, s, d)


def kernel(x, c, ctx, c_ctx, ada_a, ada_b, ada_bias, norm1_g, norm2_g, w_in, conv_a_w, lam_qk, subln_g, lru_conv_w, lru_conv_b, lru_w_a, lru_b_a, lru_w_x, lru_b_x, lru_lambda, w_branch, w_out, router_w, router_b, w_gu, b_gu, w_dn, b_dn, final_g):
    return _forward(_Cfg(), x, c, ctx, c_ctx, ada_a, ada_b, ada_bias, norm1_g, norm2_g, w_in, conv_a_w, lam_qk,
                    subln_g, lru_conv_w, lru_conv_b, lru_w_a, lru_b_a, lru_w_x, lru_b_x, lru_lambda,
                    w_branch, w_out, router_w, router_b, w_gu, b_gu, w_dn, b_dn, final_g)
```
